```python
import jax, jax.numpy as jnp
from jax import lax
import numpy as np

D_MODEL = 1024
BATCH = 4
SEQ = 8192
DEPTH = 1
DEC_BATCH = 8
DEC_SEQ = 4096
PAST_LEN = 128

D_RNN = D_MODEL
RG_BLOCKS = 16
RG_BW = D_RNN // RG_BLOCKS
RG_C = 8.0
CONV_W = 4
CONV_LEFT = 2
HEAD_DIM = 64
HEADS_PER_GROUP = 8
ATT_GROUPS = ((128, 1), (512, 4), (2048, 16))
N_GROUPS = len(ATT_GROUPS)
ATT_W = N_GROUPS * HEADS_PER_GROUP * HEAD_DIM
ATT_OUT = HEADS_PER_GROUP * HEAD_DIM
ATT_BLOCK = 64
ROT_DIM = HEAD_DIM // 4
ROPE_THETA = 500000.0
D_FF = ((-(-8 * D_MODEL // 3) + 255) // 256) * 256
N_MOD = 6
EPS = 1e-6
NEG_INF = -1e30
IN_SPLITS = (D_RNN, 2 * D_RNN, 2 * D_RNN + ATT_W, 2 * D_RNN + 2 * ATT_W, 2 * D_RNN + 3 * ATT_W)
IN_COLS = 2 * D_RNN + 3 * ATT_W + 2 * D_MODEL

kernel_name = 'griffin_dilated_window_adaln_encoder'


def _rmsnorm(x, g):
    xf = x.astype(jnp.float32)
    y = xf * lax.rsqrt(jnp.mean(xf * xf, axis=-1, keepdims=True) + EPS)
    return (y * g.astype(jnp.float32)).astype(x.dtype)


def _rope(x):
    S = x.shape[1]
    half = ROT_DIM // 2
    inv = ROPE_THETA ** (-(jnp.arange(0, ROT_DIM, 2, dtype=jnp.float32) / ROT_DIM))
    ang = jnp.arange(S, dtype=jnp.float32)[:, None] * inv[None, :]
    cos = jnp.cos(ang)[None, :, None, None, :]
    sin = jnp.sin(ang)[None, :, None, None, :]
    xf = x.astype(jnp.float32)
    x1 = xf[..., :half]
    x2 = xf[..., half:ROT_DIM]
    out = jnp.concatenate([x1 * cos - x2 * sin, x2 * cos + x1 * sin, xf[..., ROT_DIM:]], axis=-1)
    return out.astype(x.dtype)


def _centred_dwconv(x, w, b):
    S = x.shape[1]
    xp = jnp.pad(x, ((0, 0), (CONV_LEFT, CONV_W - 1 - CONV_LEFT), (0, 0)))
    y = b[None, None, :] + xp[:, 0:S] * w[0]
    for k in range(1, CONV_W):
        y = y + xp[:, k:k + S] * w[k]
    return y


def _rglru(xc, wa, ba, wx, bx, lam, reverse):
    B, S, _ = xc.shape
    xf = xc.astype(jnp.float32)
    xb = xf.reshape(B, S, RG_BLOCKS, RG_BW)
    r = jax.nn.sigmoid(jnp.einsum('bsnc,ncd->bsnd', xb, wa.astype(jnp.float32)) + ba.astype(jnp.float32)).reshape(B, S, D_RNN)
    i = jax.nn.sigmoid(jnp.einsum('bsnc,ncd->bsnd', xb, wx.astype(jnp.float32)) + bx.astype(jnp.float32)).reshape(B, S, D_RNN)
    log_a = -RG_C * r * jax.nn.softplus(-lam.astype(jnp.float32))
    a = jnp.exp(log_a)
    u = jnp.sqrt(-jnp.expm1(2.0 * log_a)) * (i * xf)

    def comb(e1, e2):
        a1, b1 = e1
        a2, b2 = e2
        return a1 * a2, a2 * b1 + b2

    _, h = lax.associative_scan(comb, (a, u), reverse=reverse, axis=1)
    return h


def _dilated_window_attention(q, k, v, dil, radius):
    B, S, H, E = q.shape
    L = S // dil
    nb = -(-L // ATT_BLOCK)
    Lp = nb * ATT_BLOCK

    def fold(t):
        t = t.reshape(B, L, dil, H, E)
        return jnp.pad(t, ((0, 0), (0, Lp - L), (0, 0), (0, 0), (0, 0)))

    def windows(t):
        tp = jnp.pad(fold(t), ((0, 0), (ATT_BLOCK, ATT_BLOCK), (0, 0), (0, 0), (0, 0)))
        tp = tp.reshape(B, nb + 2, ATT_BLOCK, dil, H, E)
        return jnp.concatenate([tp[:, :-2], tp[:, 1:-1], tp[:, 2:]], axis=2)

    qf = fold(q).reshape(B, nb, ATT_BLOCK, dil, H, E)
    kw = windows(k)
    vw = windows(v)
    s = jnp.einsum('bnqrhe,bnkrhe->bnrhqk', qf, kw, preferred_element_type=jnp.float32) * (HEAD_DIM ** -0.5)
    blk = jnp.arange(nb)[:, None, None]
    mq = blk * ATT_BLOCK + jnp.arange(ATT_BLOCK)[None, :, None]
    mk = (blk - 1) * ATT_BLOCK + jnp.arange(3 * ATT_BLOCK)[None, None, :]
    valid = (jnp.abs(mq - mk) <= radius) & (mk >= 0) & (mk < L)
    s = jnp.where(valid[None, :, None, None, :, :], s, NEG_INF)
    lse = jax.nn.logsumexp(s, axis=-1)
    p = jnp.exp(s - lse[..., None])
    o = jnp.einsum('bnrhqk,bnkrhe->bnqrhe', p, vw.astype(jnp.float32))
    o = o.reshape(B, Lp, dil, H, E)[:, :L].reshape(B, S, H, E)
    lse = lse.transpose(0, 1, 4, 2, 3).reshape(B, Lp, dil, H)[:, :L].reshape(B, S, H)
    return o, lse


def _layer(x, c, w_ada, b_ada, norm1_g, w_in, conv_w, conv_b, rg_wa, rg_ba, rg_wx, rg_bx, rg_lambda,
           w_br_rnn, w_br_attn, w_out, norm2_g, w_ffn_in, w_ffn_out):
    B, S, _ = x.shape
    mod = jax.nn.silu(c.astype(jnp.float32)) @ w_ada.astype(jnp.float32) + b_ada.astype(jnp.float32)
    sh1, sc1, gt1, sh2, sc2, gt2 = jnp.split(mod, N_MOD, axis=-1)

    h = (_rmsnorm(x, norm1_g) * (1.0 + sc1[:, None]) + sh1[:, None]).astype(x.dtype)
    z = h @ w_in
    xr, gr, q, k, v, mg = jnp.split(z, IN_SPLITS, axis=-1)

    xc = _centred_dwconv(xr, conv_w, conv_b)
    rec = _rglru(xc, rg_wa[0], rg_ba[0], rg_wx[0], rg_bx[0], rg_lambda[0], False) \
        + _rglru(xc, rg_wa[1], rg_ba[1], rg_wx[1], rg_bx[1], rg_lambda[1], True)
    rnn_out = (rec * jax.nn.gelu(gr.astype(jnp.float32), approximate=True)).astype(x.dtype)

    q = _rope(q.reshape(B, S, N_GROUPS, HEADS_PER_GROUP, HEAD_DIM))
    k = _rope(k.reshape(B, S, N_GROUPS, HEADS_PER_GROUP, HEAD_DIM))
    v = v.reshape(B, S, N_GROUPS, HEADS_PER_GROUP, HEAD_DIM)
    outs = []
    lses = []
    for g, (win, dil) in enumerate(ATT_GROUPS):
        o_g, lse_g = _dilated_window_attention(q[:, :, g], k[:, :, g], v[:, :, g], dil, (win // 2) // dil)
        outs.append(o_g)
        lses.append(lse_g)
    wts = jax.nn.softmax(jnp.stack(lses, axis=0), axis=0)
    att = jnp.einsum('gbsh,gbshe->bshe', wts, jnp.stack(outs, axis=0))
    att_out = att.reshape(B, S, ATT_OUT).astype(x.dtype)

    gate_r, gate_a = jnp.split(jax.nn.sigmoid(mg.astype(jnp.float32)), 2, axis=-1)
    merged = gate_r * (rnn_out @ w_br_rnn) + gate_a * (att_out @ w_br_attn)
    mix = merged.astype(x.dtype) @ w_out
    x = x + (gt1[:, None] * mix).astype(x.dtype)

    h2 = (_rmsnorm(x, norm2_g) * (1.0 + sc2[:, None]) + sh2[:, None]).astype(x.dtype)
    fg, fu = jnp.split(h2 @ w_ffn_in, 2, axis=-1)
    ff = (jax.nn.silu(fg.astype(jnp.float32)) * fu.astype(jnp.float32)).astype(x.dtype) @ w_ffn_out
    x = x + (gt2[:, None] * ff).astype(x.dtype)
    return x


def _encode(x, c, w_ada, b_ada, norm1_g, w_in, conv_w, conv_b, rg_wa, rg_ba, rg_wx, rg_bx, rg_lambda,
            w_br_rnn, w_br_attn, w_out, norm2_g, w_ffn_in, w_ffn_out, final_g):
    for l in range(DEPTH):
        x = _layer(x, c, w_ada[l], b_ada[l], norm1_g[l], w_in[l], conv_w[l], conv_b[l], rg_wa[l], rg_ba[l],
                   rg_wx[l], rg_bx[l], rg_lambda[l], w_br_rnn[l], w_br_attn[l], w_out[l], norm2_g[l],
                   w_ffn_in[l], w_ffn_out[l])
    return _rmsnorm(x, final_g)


def setup_inputs(seed: int = 0) -> dict:
    key = jax.random.key(seed)
    ks = jax.random.split(key, 24)

    def nrm(k, shape, s):
        return jax.random.normal(k, shape, jnp.float32) * s

    a0 = jax.random.uniform(ks[15], (DEPTH, 2, D_RNN), jnp.float32, minval=0.9, maxval=0.999)
    return {
        'x_prompt': nrm(ks[0], (BATCH, SEQ, D_MODEL), 1.0),
        'x_sample': nrm(ks[1], (DEC_BATCH, DEC_SEQ, D_MODEL), 1.0),
        'c_prompt': nrm(ks[2], (BATCH, D_MODEL), 1.0),
        'c_sample': nrm(ks[3], (DEC_BATCH, D_MODEL), 1.0),
        'w_ada': nrm(ks[4], (DEPTH, D_MODEL, N_MOD * D_MODEL), 0.5 * D_MODEL ** -0.5),
        'b_ada': nrm(ks[5], (DEPTH, N_MOD * D_MODEL), 0.01),
        'norm1_g': 1.0 + nrm(ks[6], (DEPTH, D_MODEL), 0.02),
        'w_in': nrm(ks[7], (DEPTH, D_MODEL, IN_COLS), D_MODEL ** -0.5),
        'conv_w': nrm(ks[8], (DEPTH, CONV_W, D_RNN), CONV_W ** -0.5),
        'conv_b': nrm(ks[9], (DEPTH, D_RNN), 0.01),
        'rg_wa': nrm(ks[10], (DEPTH, 2, RG_BLOCKS, RG_BW, RG_BW), RG_BW ** -0.5),
        'rg_ba': nrm(ks[11], (DEPTH, 2, RG_BLOCKS, RG_BW), 0.01),
        'rg_wx': nrm(ks[12], (DEPTH, 2, RG_BLOCKS, RG_BW, RG_BW), RG_BW ** -0.5),
        'rg_bx': nrm(ks[13], (DEPTH, 2, RG_BLOCKS, RG_BW), 0.01),
        'rg_lambda': jnp.log(a0) - jnp.log1p(-a0),
        'w_br_rnn': nrm(ks[16], (DEPTH, D_RNN, D_MODEL), D_RNN ** -0.5),
        'w_br_attn': nrm(ks[17], (DEPTH, ATT_OUT, D_MODEL), ATT_OUT ** -0.5),
        'w_out': nrm(ks[18], (DEPTH, D_MODEL, D_MODEL), D_MODEL ** -0.5),
        'norm2_g': 1.0 + nrm(ks[19], (DEPTH, D_MODEL), 0.02),
        'w_ffn_in': nrm(ks[20], (DEPTH, D_MODEL, 2 * D_FF), D_MODEL ** -0.5),
        'w_ffn_out': nrm(ks[21], (DEPTH, D_FF, D_MODEL), D_FF ** -0.5),
        'final_g': 1.0 + nrm(ks[22], (D_MODEL,), 0.02),
    }


def reference(x_prompt, x_sample, c_prompt, c_sample, w_ada, b_ada, norm1_g, w_in, conv_w, conv_b,
              rg_wa, rg_ba, rg_wx, rg_bx, rg_lambda, w_br_rnn, w_br_attn, w_out, norm2_g,
              w_ffn_in, w_ffn_out, final_g):
    y_prompt = _encode(x_prompt, c_prompt, w_ada, b_ada, norm1_g, w_in, conv_w, conv_b, rg_wa, rg_ba, rg_wx,
                       rg_bx, rg_lambda, w_br_rnn, w_br_attn, w_out, norm2_g, w_ffn_in, w_ffn_out, final_g)
    y_sample = _encode(x_sample, c_sample, w_ada, b_ada, norm1_g, w_in, conv_w, conv_b, rg_wa, rg_ba, rg_wx,
                       rg_bx, rg_lambda, w_br_rnn, w_br_attn, w_out, norm2_g, w_ffn_in, w_ffn_out, final_g)
    return (y_prompt, y_sample)
```

```python
import functools

import jax
import jax.numpy as jnp
from jax import lax
from jax.experimental import pallas as pl
from jax.experimental.pallas import tpu as pltpu

F32 = jnp.float32
BF16 = jnp.bfloat16

D_MODEL = 1024
D_RNN = 1024
RG_BLOCKS = 16
RG_BW = D_RNN // RG_BLOCKS
RG_C = 8.0
CONV_W = 4
CONV_LEFT = 2
HEAD_DIM = 64
HEADS_PER_GROUP = 8
ATT_GROUPS = ((128, 1), (512, 4), (2048, 16))
N_GROUPS = len(ATT_GROUPS)
GROUP_W = HEADS_PER_GROUP * HEAD_DIM
ATT_W = N_GROUPS * GROUP_W
ROT_DIM = HEAD_DIM // 4
ROPE_THETA = 500000.0
D_FF = 2816
N_MOD = 6
EPS = 1e-6
NEG_INF = -1e30
IN_COLS = 2 * D_RNN + 3 * ATT_W + 2 * D_MODEL
RADIUS = 64

LANES = 128
SUBLANES = 8
VMEM_LIMIT_BYTES = 56 * 1024 * 1024

COL_CHUNK = 512
RNN_CG = 128
RNN_T = 64
ATT_BQ = 128
ATT_KW = ATT_BQ + 2 * RADIUS
FF_CHUNK = 256


def _resident(shape):
    nd = len(shape)
    return pl.BlockSpec(shape, lambda *_: (0,) * nd, pipeline_mode=pl.Buffered(1))


def _params(*sem):
    return pltpu.CompilerParams(dimension_semantics=sem, vmem_limit_bytes=VMEM_LIMIT_BYTES)


def _rms_mod(x, g, sc, sh):
    y = x * lax.rsqrt(jnp.mean(x * x, axis=-1, keepdims=True) + EPS)
    return (y * g) * (1.0 + sc) + sh


def _sigmoid(x):
    return 1.0 / (1.0 + jnp.exp(-x))


def _mod_kernel(c_ref, w_ref, b_ref, o_ref):
    c = c_ref[...]
    s = (c * _sigmoid(c)).astype(BF16)
    o_ref[...] = jnp.dot(s, w_ref[...], preferred_element_type=F32) + b_ref[...]


def _modulation(c, w_ada, b_ada):
    B = c.shape[0]
    rows = -(-B // 16) * 16
    cp = jnp.pad(c, ((0, rows - B), (0, 0)))
    out = pl.pallas_call(
        _mod_kernel,
        out_shape=jax.ShapeDtypeStruct((rows, N_MOD * D_MODEL), F32),
        grid=(N_MOD,),
        in_specs=[
            pl.BlockSpec((rows, D_MODEL), lambda j: (0, 0)),
            pl.BlockSpec((D_MODEL, D_MODEL), lambda j: (0, j)),
            pl.BlockSpec((1, D_MODEL), lambda j: (0, j)),
        ],
        out_specs=pl.BlockSpec((rows, D_MODEL), lambda j: (0, j)),
        compiler_params=_params("parallel"),
        name="mod",
    )(cp, w_ada, b_ada.reshape(1, -1))
    return out[:B].reshape(B, N_MOD, 1, D_MODEL)


def _inproj_kernel(x_ref, sc_ref, sh_ref, g_ref, w_ref, cos_ref, sa_ref, sb_ref,
                   z_ref, q0_ref, q1_ref, q2_ref, fold_ref, *, tm):
    h = _rms_mod(x_ref[0], g_ref[...], sc_ref[0], sh_ref[0]).astype(BF16)

    def proj(j):
        return jnp.dot(h, w_ref[:, j * COL_CHUNK:(j + 1) * COL_CHUNK], preferred_element_type=F32)

    def rope(v):
        parts = []
        for s in range(COL_CHUNK // LANES):
            p = v[:, s * LANES:(s + 1) * LANES]
            up = pltpu.roll(p, LANES - ROT_DIM // 2, axis=1)
            dn = pltpu.roll(p, ROT_DIM // 2, axis=1)
            parts.append(p * cos_ref[...] + up * sa_ref[...] + dn * sb_ref[...])
        return jnp.concatenate(parts, axis=1)

    qkv_refs = (q0_ref, q1_ref, q2_ref)

    def emit_folded(val, g, off):
        dil = ATT_GROUPS[g][1]
        ref = qkv_refs[g]
        if dil == 1:
            ref[0, 0, :, off:off + COL_CHUNK] = val.astype(BF16)
            return
        for s in range(COL_CHUNK // LANES):
            fold_ref[s] = val[:, s * LANES:(s + 1) * LANES]
        for r in range(dil):
            for s in range(COL_CHUNK // LANES):
                lo = off + s * LANES
                ref[0, r, :, lo:lo + LANES] = fold_ref[s, pl.ds(r, tm // dil, stride=dil), :].astype(BF16)

    n_rnn = 2 * D_RNN // COL_CHUNK
    for j in range(n_rnn):
        z_ref[0, :, j * COL_CHUNK:(j + 1) * COL_CHUNK] = proj(j).astype(BF16)
    for g in range(N_GROUPS):
        emit_folded(rope(proj(n_rnn + g)) * (HEAD_DIM ** -0.5), g, 0)
        emit_folded(rope(proj(n_rnn + N_GROUPS + g)), g, GROUP_W)
        emit_folded(proj(n_rnn + 2 * N_GROUPS + g), g, 2 * GROUP_W)
    n_gate0 = n_rnn + 3 * N_GROUPS
    for j in range(2 * D_MODEL // COL_CHUNK):
        z_ref[0, :, (n_rnn + j) * COL_CHUNK:(n_rnn + j + 1) * COL_CHUNK] = proj(n_gate0 + j).astype(BF16)


def _rope_tables(S):
    half = ROT_DIM // 2
    inv = ROPE_THETA ** (-(jnp.arange(0, ROT_DIM, 2, dtype=F32) / ROT_DIM))
    ang = jnp.arange(S, dtype=F32)[:, None] * inv[None, :]
    cos, sin = jnp.cos(ang), jnp.sin(ang)
    zeros = jnp.zeros((S, HEAD_DIM - ROT_DIM), F32)
    z8 = jnp.zeros((S, half), F32)
    c = jnp.concatenate([cos, cos, zeros + 1.0], axis=1)
    sa = jnp.concatenate([-sin, z8, zeros], axis=1)
    sb = jnp.concatenate([z8, sin, zeros], axis=1)
    rep = LANES // HEAD_DIM
    return tuple(jnp.tile(t, (1, rep)) for t in (c, sa, sb))


def _inproj(x, sc, sh, g, w_in, tm):
    B, S, D = x.shape
    cos, sa, sb = _rope_tables(S)
    dils = [d for _, d in ATT_GROUPS]
    row = lambda b, i: (b, 0, 0)
    out_shape = [jax.ShapeDtypeStruct((B, S, 2 * D_RNN + 2 * D_MODEL), BF16)]
    out_specs = [pl.BlockSpec((1, tm, 2 * D_RNN + 2 * D_MODEL), lambda b, i: (b, i, 0))]
    for d in dils:
        out_shape.append(jax.ShapeDtypeStruct((B, d, S // d, 3 * GROUP_W), BF16))
        out_specs.append(pl.BlockSpec((1, d, tm // d, 3 * GROUP_W), lambda b, i: (b, 0, i, 0)))
    tab = pl.BlockSpec((tm, LANES), lambda b, i: (i, 0))
    return pl.pallas_call(
        functools.partial(_inproj_kernel, tm=tm),
        out_shape=out_shape,
        grid=(B, S // tm),
        in_specs=[
            pl.BlockSpec((1, tm, D), lambda b, i: (b, i, 0)),
            pl.BlockSpec((1, 1, D), row),
            pl.BlockSpec((1, 1, D), row),
            _resident((1, D)),
            _resident((D, IN_COLS)),
            tab, tab, tab,
        ],
        out_specs=out_specs,
        scratch_shapes=[pltpu.VMEM((COL_CHUNK // LANES, tm, LANES), F32)],
        compiler_params=_params("parallel", "parallel"),
        name="inproj",
    )(x, sc, sh, g, w_in, cos, sa, sb)


def _scan_chunk(a, u, reverse):
    T = a.shape[0]
    row = lax.broadcasted_iota(jnp.int32, a.shape, 0)
    d = 1
    while d < T:
        shift = (T - d) if reverse else d
        a_s = pltpu.roll(a, shift, axis=0)
        u_s = pltpu.roll(u, shift, axis=0)
        live = (row < T - d) if reverse else (row >= d)
        u = jnp.where(live, a * u_s + u, u)
        a = jnp.where(live, a * a_s, a)
        d *= 2
    return a, u


def _rglru_kernel(x_ref, gate_ref, cw_ref, cb_ref, wg_ref, bg_ref, lam_ref, o_ref,
                  xpad_ref, hf_ref, *, S):
    C = RNN_CG
    T = RNN_T
    PAD = SUBLANES
    n_chunks = S // T

    xpad_ref[0:PAD, :] = jnp.zeros((PAD, C), F32)
    xpad_ref[S + PAD:S + 2 * PAD, :] = jnp.zeros((PAD, C), F32)

    def fill(c, carry):
        r0 = pl.multiple_of(c * T, T)
        xpad_ref[pl.ds(r0 + PAD, T), :] = x_ref[0, pl.ds(r0, T), :].astype(F32)
        return carry

    lax.fori_loop(0, n_chunks, fill, 0)

    neg_lam = -lam_ref[0]
    softplus = jnp.maximum(neg_lam, 0.0) + jnp.log1p(jnp.exp(-jnp.abs(neg_lam)))
    coef = -RG_C * softplus
    cw = cw_ref[...]
    cb = cb_ref[...]

    def gates(r0, direction):
        win = xpad_ref[pl.ds(r0, T + 2 * PAD), :]
        base = PAD - CONV_LEFT
        xc = cb + win[base:base + T] * cw[0:1]
        for k in range(1, CONV_W):
            xc = xc + win[base + k:base + k + T] * cw[k:k + 1]
        lo = direction * 2 * C
        gz = jnp.dot(xc.astype(BF16), wg_ref[0, :, lo:lo + 2 * C], preferred_element_type=F32)
        gz = gz + bg_ref[0, :, lo:lo + 2 * C]
        r = _sigmoid(gz[:, :C])
        i = _sigmoid(gz[:, C:])
        log_a = coef[direction:direction + 1] * r
        a = jnp.exp(log_a)
        u = jnp.sqrt(1.0 - a * a) * (i * xc)
        return a, u

    def fwd(c, carry):
        r0 = pl.multiple_of(c * T, T)
        a, u = _scan_chunk(*gates(r0, 0), reverse=False)
        h = u + a * carry
        hf_ref[pl.ds(r0, T), :] = h
        return h[T - 1:T]

    lax.fori_loop(0, n_chunks, fwd, jnp.zeros((1, C), F32))

    def bwd(c, carry):
        r0 = pl.multiple_of((n_chunks - 1 - c) * T, T)
        a, u = _scan_chunk(*gates(r0, 1), reverse=True)
        h = u + a * carry
        gate = gate_ref[0, pl.ds(r0, T), :].astype(F32)
        gelu = 0.5 * gate * (1.0 + jnp.tanh(0.7978845608028654 * (gate + 0.044715 * (gate * gate * gate))))
        o_ref[0, pl.ds(r0, T), :] = ((hf_ref[pl.ds(r0, T), :] + h) * gelu).astype(BF16)
        return h[0:1]

    lax.fori_loop(0, n_chunks, bwd, jnp.zeros((1, C), F32))


def _block_diag_pairs(w):
    per = RNN_CG // RG_BW
    w = w.reshape(RG_BLOCKS // per, per, RG_BW, RG_BW)
    rows = []
    for p in range(per):
        cols = [w[:, p] if q == p else jnp.zeros_like(w[:, p]) for q in range(per)]
        rows.append(jnp.concatenate(cols, axis=-1))
    return jnp.concatenate(rows, axis=1)


def _rglru(z, conv_w, conv_b, rg_wa, rg_ba, rg_wx, rg_bx, rg_lambda):
    B, S, _ = z.shape
    C = RNN_CG
    n_grp = D_RNN // C
    wg = jnp.concatenate([_block_diag_pairs(rg_wa[0]), _block_diag_pairs(rg_wx[0]),
                          _block_diag_pairs(rg_wa[1]), _block_diag_pairs(rg_wx[1])], axis=-1).astype(BF16)
    bg = jnp.concatenate([rg_ba[0].reshape(n_grp, 1, C), rg_bx[0].reshape(n_grp, 1, C),
                          rg_ba[1].reshape(n_grp, 1, C), rg_bx[1].reshape(n_grp, 1, C)], axis=-1)
    lam = rg_lambda.reshape(2, n_grp, C).transpose(1, 0, 2)
    return pl.pallas_call(
        functools.partial(_rglru_kernel, S=S),
        out_shape=jax.ShapeDtypeStruct((B, S, D_RNN), BF16),
        grid=(B, n_grp),
        in_specs=[
            pl.BlockSpec((1, S, C), lambda b, c: (b, 0, c)),
            pl.BlockSpec((1, S, C), lambda b, c: (b, 0, n_grp + c)),
            pl.BlockSpec((CONV_W, C), lambda b, c: (0, c)),
            pl.BlockSpec((1, C), lambda b, c: (0, c)),
            pl.BlockSpec((1, C, 4 * C), lambda b, c: (c, 0, 0)),
            pl.BlockSpec((1, 1, 4 * C), lambda b, c: (c, 0, 0)),
            pl.BlockSpec((1, 2, C), lambda b, c: (c, 0, 0)),
        ],
        out_specs=pl.BlockSpec((1, S, C), lambda b, c: (b, 0, c)),
        scratch_shapes=[pltpu.VMEM((S + 2 * SUBLANES, C), F32), pltpu.VMEM((S, C), F32)],
        compiler_params=_params("parallel", "parallel"),
        name="rglru",
    )(z, z, conv_w, conv_b.reshape(1, -1), wg, bg, lam)


def _attn_kernel(q_ref, k_ref, v_ref, o_ref, l_ref, *, L):
    BQ, KW = ATT_BQ, ATT_KW
    lane = lax.broadcasted_iota(jnp.int32, (BQ, LANES), 1)
    first = lane < HEAD_DIM
    rel = lax.broadcasted_iota(jnp.int32, (BQ, KW), 0) - lax.broadcasted_iota(jnp.int32, (BQ, KW), 1)

    def block(i, carry):
        q0 = pl.multiple_of(i * BQ, BQ)
        k0 = pl.multiple_of(jnp.clip(q0 - RADIUS, 0, L - KW), RADIUS)
        q = q_ref[0, 0, pl.ds(q0, BQ), :]
        k = k_ref[0, 0, pl.ds(k0, KW), :]
        v = v_ref[0, 0, pl.ds(k0, KW), :]
        valid = jnp.abs(rel + (q0 - k0)) <= RADIUS
        outs, lses = [], []
        for sel in (first, jnp.logical_not(first)):
            qh = jnp.where(sel, q, jnp.zeros_like(q))
            s = lax.dot_general(qh, k, (((1,), (1,)), ((), ())), preferred_element_type=F32)
            s = jnp.where(valid, s, NEG_INF)
            m = jnp.max(s, axis=-1, keepdims=True)
            p = jnp.exp(s - m)
            den = jnp.sum(p, axis=-1, keepdims=True)
            o = jnp.dot(p.astype(BF16), v, preferred_element_type=F32)
            outs.append(o / den)
            lses.append(m + jnp.log(den))
        o_ref[0, 0, pl.ds(q0, BQ), :] = jnp.where(first, outs[0], outs[1]).astype(BF16)
        l_ref[0, 0, pl.ds(q0, BQ), :] = jnp.where(first, lses[0], lses[1])
        return carry

    lax.fori_loop(0, L // BQ, block, 0)


def _attention(qkv):
    B, dil, L, _ = qkv.shape
    n_pair = GROUP_W // LANES
    spec = lambda off: pl.BlockSpec((1, 1, L, LANES), lambda b, r, p: (b, r, 0, off + p))
    return pl.pallas_call(
        functools.partial(_attn_kernel, L=L),
        out_shape=[jax.ShapeDtypeStruct((B, dil, L, GROUP_W), BF16),
                   jax.ShapeDtypeStruct((B, dil, L, GROUP_W), F32)],
        grid=(B, dil, n_pair),
        in_specs=[spec(0), spec(n_pair), spec(2 * n_pair)],
        out_specs=[spec(0), spec(0)],
        compiler_params=_params("parallel", "parallel", "parallel"),
        name="attn",
    )(qkv, qkv, qkv)


def _merge_kernel(x_ref, gt_ref, mg_ref, rnn_ref, o0_ref, l0_ref, o1_ref, l1_ref, o2_ref, l2_ref,
                  wr_ref, wa_ref, wo_ref, y_ref, fo1_ref, fl1_ref, fo2_ref, fl2_ref, *, tm):
    n_tiles = GROUP_W // LANES

    def unfold(src_ref, dst_ref, dil):
        for r in range(dil):
            for s in range(n_tiles):
                dst_ref[s, pl.ds(r, tm // dil, stride=dil), :] = src_ref[0, r, :, s * LANES:(s + 1) * LANES].astype(F32)
        return jnp.concatenate([dst_ref[s] for s in range(n_tiles)], axis=1)

    o1 = unfold(o1_ref, fo1_ref, ATT_GROUPS[1][1])
    l1 = unfold(l1_ref, fl1_ref, ATT_GROUPS[1][1])
    o2 = unfold(o2_ref, fo2_ref, ATT_GROUPS[2][1])
    l2 = unfold(l2_ref, fl2_ref, ATT_GROUPS[2][1])
    lses = (l0_ref[0, 0], l1, l2)
    outs = (o0_ref[0, 0].astype(F32), o1, o2)
    top = jnp.maximum(jnp.maximum(lses[0], lses[1]), lses[2])
    es = [jnp.exp(l - top) for l in lses]
    att = (es[0] * outs[0] + es[1] * outs[1] + es[2] * outs[2]) / (es[0] + es[1] + es[2])

    gate = _sigmoid(mg_ref[0].astype(F32))
    br_r = jnp.dot(rnn_ref[0], wr_ref[...], preferred_element_type=F32)
    br_a = jnp.dot(att.astype(BF16), wa_ref[...], preferred_element_type=F32)
    merged = gate[:, :D_MODEL] * br_r + gate[:, D_MODEL:] * br_a
    mix = jnp.dot(merged.astype(BF16), wo_ref[...], preferred_element_type=F32)
    y_ref[0] = x_ref[0] + gt_ref[0] * mix


def _merge(x, gt, z, rnn, attn_outs, w_br_rnn, w_br_attn, w_out, tm):
    B, S, D = x.shape
    row = lambda b, i: (b, 0, 0)
    in_specs = [
        pl.BlockSpec((1, tm, D), lambda b, i: (b, i, 0)),
        pl.BlockSpec((1, 1, D), row),
        pl.BlockSpec((1, tm, 2 * D_MODEL), lambda b, i: (b, i, 1)),
        pl.BlockSpec((1, tm, D_RNN), lambda b, i: (b, i, 0)),
    ]
    args = [x, gt, z, rnn]
    for (o, l), (_, d) in zip(attn_outs, ATT_GROUPS):
        blk = pl.BlockSpec((1, d, tm // d, GROUP_W), lambda b, i: (b, 0, i, 0))
        in_specs += [blk, blk]
        args += [o, l]
    in_specs += [_resident(w_br_rnn.shape), _resident(w_br_attn.shape), _resident(w_out.shape)]
    args += [w_br_rnn, w_br_attn, w_out]
    return pl.pallas_call(
        functools.partial(_merge_kernel, tm=tm),
        out_shape=jax.ShapeDtypeStruct((B, S, D), F32),
        grid=(B, S // tm),
        in_specs=in_specs,
        out_specs=pl.BlockSpec((1, tm, D), lambda b, i: (b, i, 0)),
        scratch_shapes=[pltpu.VMEM((GROUP_W // LANES, tm, LANES), F32)] * 4,
        compiler_params=_params("parallel", "parallel"),
        name="merge",
    )(*args)


def _ffn_kernel(x_ref, sc_ref, sh_ref, gt_ref, g2_ref, gf_ref, wi_ref, wo_ref, y_ref):
    x = x_ref[0]
    h = _rms_mod(x, g2_ref[...], sc_ref[0], sh_ref[0]).astype(BF16)
    ff = None
    for c in range(D_FF // FF_CHUNK):
        lo = c * FF_CHUNK
        fg = jnp.dot(h, wi_ref[:, lo:lo + FF_CHUNK], preferred_element_type=F32)
        fu = jnp.dot(h, wi_ref[:, D_FF + lo:D_FF + lo + FF_CHUNK], preferred_element_type=F32)
        act = ((fg * _sigmoid(fg)) * fu).astype(BF16)
        part = jnp.dot(act, wo_ref[lo:lo + FF_CHUNK, :], preferred_element_type=F32)
        ff = part if ff is None else ff + part
    x2 = x + gt_ref[0] * ff
    y = x2 * lax.rsqrt(jnp.mean(x2 * x2, axis=-1, keepdims=True) + EPS)
    y_ref[0] = y * gf_ref[...]


def _ffn(x, sc, sh, gt, g2, gf, w_ffn_in, w_ffn_out, tm):
    B, S, D = x.shape
    row = lambda b, i: (b, 0, 0)
    return pl.pallas_call(
        _ffn_kernel,
        out_shape=jax.ShapeDtypeStruct((B, S, D), F32),
        grid=(B, S // tm),
        in_specs=[
            pl.BlockSpec((1, tm, D), lambda b, i: (b, i, 0)),
            pl.BlockSpec((1, 1, D), row),
            pl.BlockSpec((1, 1, D), row),
            pl.BlockSpec((1, 1, D), row),
            _resident((1, D)),
            _resident((1, D)),
            _resident(w_ffn_in.shape),
            _resident(w_ffn_out.shape),
        ],
        out_specs=pl.BlockSpec((1, tm, D), lambda b, i: (b, i, 0)),
        compiler_params=_params("parallel", "parallel"),
        name="ffn",
    )(x, sc, sh, gt, g2, gf, w_ffn_in, w_ffn_out)


def _encode(x, c, w, tm_in, tm_merge, tm_ffn):
    mod = _modulation(c, w["w_ada"], w["b_ada"])
    sh1, sc1, gt1, sh2, sc2, gt2 = (mod[:, k] for k in range(N_MOD))
    outs = _inproj(x, sc1, sh1, w["norm1_g"], w["w_in"], tm_in)
    z, qkvs = outs[0], outs[1:]
    rnn = _rglru(z, w["conv_w"], w["conv_b"], w["rg_wa"], w["rg_ba"], w["rg_wx"], w["rg_bx"], w["rg_lambda"])
    attn_outs = [_attention(qkv) for qkv in qkvs]
    x1 = _merge(x, gt1, z, rnn, attn_outs, w["w_br_rnn"], w["w_br_attn"], w["w_out"], tm_merge)
    return _ffn(x1, sc2, sh2, gt2, w["norm2_g"], w["final_g"], w["w_ffn_in"], w["w_ffn_out"], tm_ffn)


def kernel(x_prompt, x_sample, c_prompt, c_sample, w_ada, b_ada, norm1_g, w_in, conv_w, conv_b, rg_wa, rg_ba, rg_wx, rg_bx, rg_lambda, w_br_rnn, w_br_attn, w_out, norm2_g, w_ffn_in, w_ffn_out, final_g):
    assert w_ada.shape[0] == 1, "single layer"
    w = dict(
        w_ada=w_ada[0].astype(BF16), b_ada=b_ada[0], norm1_g=norm1_g[0].reshape(1, -1),
        w_in=w_in[0].astype(BF16), conv_w=conv_w[0], conv_b=conv_b[0],
        rg_wa=rg_wa[0], rg_ba=rg_ba[0], rg_wx=rg_wx[0], rg_bx=rg_bx[0], rg_lambda=rg_lambda[0],
        w_br_rnn=w_br_rnn[0].astype(BF16), w_br_attn=w_br_attn[0].astype(BF16), w_out=w_out[0].astype(BF16),
        norm2_g=norm2_g[0].reshape(1, -1), w_ffn_in=w_ffn_in[0].astype(BF16),
        w_ffn_out=w_ffn_out[0].astype(BF16), final_g=final_g.reshape(1, -1),
    )
    tiles = dict(tm_in=512, tm_merge=512, tm_ffn=512)
    return (_encode(x_prompt, c_prompt, w, **tiles), _encode(x_sample, c_sample, w, **tiles))
```

```python
import functools

import jax
import jax.numpy as jnp
from jax import lax
from jax.experimental import pallas as pl
from jax.experimental.pallas import tpu as pltpu

F32 = jnp.float32
BF16 = jnp.bfloat16

D_MODEL = 1024
D_RNN = 1024
RG_BLOCKS = 16
RG_BW = D_RNN // RG_BLOCKS
RG_C = 8.0
CONV_W = 4
CONV_LEFT = 2
HEAD_DIM = 64
HEADS_PER_GROUP = 8
ATT_GROUPS = ((128, 1), (512, 4), (2048, 16))
N_GROUPS = len(ATT_GROUPS)
GROUP_W = HEADS_PER_GROUP * HEAD_DIM
ATT_W = N_GROUPS * GROUP_W
ROT_DIM = HEAD_DIM // 4
ROPE_THETA = 500000.0
D_FF = 2816
N_MOD = 6
EPS = 1e-6
NEG_INF = -1e30
IN_COLS = 2 * D_RNN + 3 * ATT_W + 2 * D_MODEL
RADIUS = 64

LANES = 128
SUBLANES = 8
VMEM_LIMIT_BYTES = 56 * 1024 * 1024

COL_CHUNK = 512
RNN_CG = 128
RNN_T = 256
RNN_TB = 32
RNN_SEG_EXTRA = 4
ATT_BQ = 128
ATT_KW = ATT_BQ + 2 * RADIUS
ATT_UNROLL = 2
FF_CHUNK = 256


def _resident(shape):
    nd = len(shape)
    return pl.BlockSpec(shape, lambda *_: (0,) * nd, pipeline_mode=pl.Buffered(1))


def _params(*sem):
    return pltpu.CompilerParams(dimension_semantics=sem, vmem_limit_bytes=VMEM_LIMIT_BYTES)


def _rms_mod(x, g, sc, sh):
    y = x * lax.rsqrt(jnp.mean(x * x, axis=-1, keepdims=True) + EPS)
    return (y * g) * (1.0 + sc) + sh


def _sigmoid(x):
    return 1.0 / (1.0 + jnp.exp(-x))


def _mod_kernel(c_ref, w_ref, b_ref, o_ref):
    c = c_ref[...]
    s = (c * _sigmoid(c)).astype(BF16)
    o_ref[...] = jnp.dot(s, w_ref[...], preferred_element_type=F32) + b_ref[...]


def _modulation(c, w_ada, b_ada):
    B = c.shape[0]
    rows = -(-B // 16) * 16
    cp = jnp.pad(c, ((0, rows - B), (0, 0)))
    out = pl.pallas_call(
        _mod_kernel,
        out_shape=jax.ShapeDtypeStruct((rows, N_MOD * D_MODEL), F32),
        grid=(N_MOD,),
        in_specs=[
            pl.BlockSpec((rows, D_MODEL), lambda j: (0, 0)),
            pl.BlockSpec((D_MODEL, D_MODEL), lambda j: (0, j)),
            pl.BlockSpec((1, D_MODEL), lambda j: (0, j)),
        ],
        out_specs=pl.BlockSpec((rows, D_MODEL), lambda j: (0, j)),
        compiler_params=_params("parallel"),
        name="mod",
    )(cp, w_ada, b_ada.reshape(1, -1))
    return out[:B].reshape(B, N_MOD, 1, D_MODEL)


def _inproj_kernel(x_ref, sc_ref, sh_ref, g_ref, w_ref, cos_ref, sa_ref, sb_ref,
                   z_ref, q0_ref, q1_ref, q2_ref, fold_ref, *, tm):
    h = _rms_mod(x_ref[0], g_ref[...], sc_ref[0], sh_ref[0]).astype(BF16)

    def proj(j):
        return jnp.dot(h, w_ref[:, j * COL_CHUNK:(j + 1) * COL_CHUNK], preferred_element_type=F32)

    def rope(v):
        parts = []
        for s in range(COL_CHUNK // LANES):
            p = v[:, s * LANES:(s + 1) * LANES]
            up = pltpu.roll(p, LANES - ROT_DIM // 2, axis=1)
            dn = pltpu.roll(p, ROT_DIM // 2, axis=1)
            parts.append(p * cos_ref[...] + up * sa_ref[...] + dn * sb_ref[...])
        return jnp.concatenate(parts, axis=1)

    qkv_refs = (q0_ref, q1_ref, q2_ref)

    def emit_folded(val, g, off):
        dil = ATT_GROUPS[g][1]
        ref = qkv_refs[g]
        if dil == 1:
            ref[0, 0, :, off:off + COL_CHUNK] = val.astype(BF16)
            return
        for s in range(COL_CHUNK // LANES):
            fold_ref[s] = val[:, s * LANES:(s + 1) * LANES]
        for r in range(dil):
            for s in range(COL_CHUNK // LANES):
                lo = off + s * LANES
                ref[0, r, :, lo:lo + LANES] = fold_ref[s, pl.ds(r, tm // dil, stride=dil), :].astype(BF16)

    n_rnn = 2 * D_RNN // COL_CHUNK
    for j in range(n_rnn):
        z_ref[0, :, j * COL_CHUNK:(j + 1) * COL_CHUNK] = proj(j).astype(BF16)
    for g in range(N_GROUPS):
        emit_folded(rope(proj(n_rnn + g)) * (HEAD_DIM ** -0.5), g, 0)
        emit_folded(rope(proj(n_rnn + N_GROUPS + g)), g, GROUP_W)
        emit_folded(proj(n_rnn + 2 * N_GROUPS + g), g, 2 * GROUP_W)
    n_gate0 = n_rnn + 3 * N_GROUPS
    for j in range(2 * D_MODEL // COL_CHUNK):
        z_ref[0, :, (n_rnn + j) * COL_CHUNK:(n_rnn + j + 1) * COL_CHUNK] = proj(n_gate0 + j).astype(BF16)


def _rope_tables(S):
    half = ROT_DIM // 2
    inv = ROPE_THETA ** (-(jnp.arange(0, ROT_DIM, 2, dtype=F32) / ROT_DIM))
    ang = jnp.arange(S, dtype=F32)[:, None] * inv[None, :]
    cos, sin = jnp.cos(ang), jnp.sin(ang)
    zeros = jnp.zeros((S, HEAD_DIM - ROT_DIM), F32)
    z8 = jnp.zeros((S, half), F32)
    c = jnp.concatenate([cos, cos, zeros + 1.0], axis=1)
    sa = jnp.concatenate([-sin, z8, zeros], axis=1)
    sb = jnp.concatenate([z8, sin, zeros], axis=1)
    rep = LANES // HEAD_DIM
    return tuple(jnp.tile(t, (1, rep)) for t in (c, sa, sb))


def _inproj(x, sc, sh, g, w_in, tm):
    B, S, D = x.shape
    cos, sa, sb = _rope_tables(S)
    dils = [d for _, d in ATT_GROUPS]
    row = lambda b, i: (b, 0, 0)
    out_shape = [jax.ShapeDtypeStruct((B, S, 2 * D_RNN + 2 * D_MODEL), BF16)]
    out_specs = [pl.BlockSpec((1, tm, 2 * D_RNN + 2 * D_MODEL), lambda b, i: (b, i, 0))]
    for d in dils:
        out_shape.append(jax.ShapeDtypeStruct((B, d, S // d, 3 * GROUP_W), BF16))
        out_specs.append(pl.BlockSpec((1, d, tm // d, 3 * GROUP_W), lambda b, i: (b, 0, i, 0)))
    tab = pl.BlockSpec((tm, LANES), lambda b, i: (i, 0))
    return pl.pallas_call(
        functools.partial(_inproj_kernel, tm=tm),
        out_shape=out_shape,
        grid=(B, S // tm),
        in_specs=[
            pl.BlockSpec((1, tm, D), lambda b, i: (b, i, 0)),
            pl.BlockSpec((1, 1, D), row),
            pl.BlockSpec((1, 1, D), row),
            _resident((1, D)),
            _resident((D, IN_COLS)),
            tab, tab, tab,
        ],
        out_specs=out_specs,
        scratch_shapes=[pltpu.VMEM((COL_CHUNK // LANES, tm, LANES), F32)],
        compiler_params=_params("parallel", "parallel"),
        name="inproj",
    )(x, sc, sh, g, w_in, cos, sa, sb)


def _rnn_seg_len(S):
    assert S % (8 * SUBLANES) == 0
    return S // SUBLANES + RNN_SEG_EXTRA


def _rglru_kernel(x_ref, gate_ref, cw_ref, cb_ref, wg_ref, bg_ref, lam_ref, o_ref,
                  xnat_ref, hnat_ref, hloc_ref, acum_ref, *, S):
    C = RNN_CG
    T = RNN_T
    TB = RNN_TB
    PAD = SUBLANES
    seg = _rnn_seg_len(S)
    n_main = (seg - RNN_SEG_EXTRA) // TB
    rows = xnat_ref.shape[0]

    xnat_ref[0:PAD, :] = jnp.zeros((PAD, C), F32)
    xnat_ref[PAD + S:rows, :] = jnp.zeros((rows - PAD - S, C), F32)

    def fill(c, carry):
        r0 = pl.multiple_of(c * T, T)
        xnat_ref[pl.ds(r0 + PAD, T), :] = x_ref[0, pl.ds(r0, T), :].astype(F32)
        return carry

    lax.fori_loop(0, S // T, fill, 0)

    neg_lam = -lam_ref[0]
    softplus = jnp.maximum(neg_lam, 0.0) + jnp.log1p(jnp.exp(-jnp.abs(neg_lam)))
    coef = -RG_C * softplus
    cwb = [jnp.broadcast_to(cw_ref[k:k + 1, :], (SUBLANES, C)) for k in range(CONV_W)]
    cbb = jnp.broadcast_to(cb_ref[...], (SUBLANES, C))
    sub = lax.broadcasted_iota(jnp.int32, (SUBLANES, C), 0)
    steps_left = S - sub * seg

    def step_rows(t):
        return (pl.ds(PAD + t, SUBLANES, stride=seg), slice(None))

    def block_gates(t0, n, direction):
        taps = [xnat_ref[step_rows(t0 + m - CONV_LEFT)] for m in range(n + CONV_W - 1)]
        xcs = []
        for j in range(n):
            acc = cbb + taps[j] * cwb[0]
            for k in range(1, CONV_W):
                acc = acc + taps[j + k] * cwb[k]
            xcs.append(acc)
        xc = jnp.concatenate(xcs, axis=0)
        lo = direction * 2 * C
        gz = jnp.dot(xc.astype(BF16), wg_ref[0, :, lo:lo + 2 * C], preferred_element_type=F32)
        gz = gz + bg_ref[0, :, lo:lo + 2 * C]
        r = _sigmoid(gz[:, :C])
        i = _sigmoid(gz[:, C:])
        a = jnp.exp(coef[direction:direction + 1] * r)
        u = jnp.sqrt(1.0 - a * a) * (i * xc)
        return a, u

    def scan_block(t0, n, direction, carry):
        h, acc = carry
        a, u = block_gates(t0, n, direction)
        order = range(n) if direction == 0 else range(n - 1, -1, -1)
        for j in order:
            aj = a[j * SUBLANES:(j + 1) * SUBLANES]
            uj = u[j * SUBLANES:(j + 1) * SUBLANES]
            if direction == 1:
                uj = jnp.where(t0 + j < steps_left, uj, 0.0)
            h = aj * h + uj
            acc = aj * acc
            row = pl.multiple_of((t0 + j) * SUBLANES, SUBLANES)
            hloc_ref[pl.ds(row, SUBLANES), :] = h
            acum_ref[pl.ds(row, SUBLANES), :] = acc
        return h, acc

    def entry_states(h_tot, a_tot, direction):
        c = jnp.zeros((1, C), F32)
        out = jnp.zeros((SUBLANES, C), F32)
        order = range(SUBLANES) if direction == 0 else range(SUBLANES - 1, -1, -1)
        for s in order:
            out = jnp.where(sub == s, c, out)
            c = a_tot[s:s + 1] * c + h_tot[s:s + 1]
        return out

    def fix_block(t0, n, direction, entry):
        for j in range(n):
            row = pl.multiple_of((t0 + j) * SUBLANES, SUBLANES)
            h = hloc_ref[pl.ds(row, SUBLANES), :] + acum_ref[pl.ds(row, SUBLANES), :] * entry
            dst = step_rows(t0 + j)
            hnat_ref[dst] = h if direction == 0 else hnat_ref[dst] + h

    tail0 = n_main * TB
    init = (jnp.zeros((SUBLANES, C), F32), jnp.ones((SUBLANES, C), F32))
    for direction in (0, 1):
        def main(b, carry, direction=direction):
            blk = b if direction == 0 else n_main - 1 - b
            return scan_block(pl.multiple_of(blk * TB, TB), TB, direction, carry)

        if direction == 0:
            carry = lax.fori_loop(0, n_main, main, init)
            h_tot, a_tot = scan_block(tail0, RNN_SEG_EXTRA, 0, carry)
        else:
            carry = scan_block(tail0, RNN_SEG_EXTRA, 1, init)
            h_tot, a_tot = lax.fori_loop(0, n_main, main, carry)
        entry = entry_states(h_tot, a_tot, direction)

        def fix(b, carry, direction=direction, entry=entry):
            fix_block(pl.multiple_of(b * TB, TB), TB, direction, entry)
            return carry

        lax.fori_loop(0, n_main, fix, 0)
        fix_block(tail0, RNN_SEG_EXTRA, direction, entry)

    def finish(c, carry):
        r0 = pl.multiple_of(c * T, T)
        gate = gate_ref[0, pl.ds(r0, T), :].astype(F32)
        gelu = 0.5 * gate * (1.0 + jnp.tanh(0.7978845608028654 * (gate + 0.044715 * (gate * gate * gate))))
        o_ref[0, pl.ds(r0, T), :] = (hnat_ref[pl.ds(r0 + PAD, T), :] * gelu).astype(BF16)
        return carry

    lax.fori_loop(0, S // T, finish, 0)


def _block_diag_pairs(w):
    per = RNN_CG // RG_BW
    w = w.reshape(RG_BLOCKS // per, per, RG_BW, RG_BW)
    rows = []
    for p in range(per):
        cols = [w[:, p] if q == p else jnp.zeros_like(w[:, p]) for q in range(per)]
        rows.append(jnp.concatenate(cols, axis=-1))
    return jnp.concatenate(rows, axis=1)


def _rglru(z, conv_w, conv_b, rg_wa, rg_ba, rg_wx, rg_bx, rg_lambda):
    B, S, _ = z.shape
    C = RNN_CG
    n_grp = D_RNN // C
    wg = jnp.concatenate([_block_diag_pairs(rg_wa[0]), _block_diag_pairs(rg_wx[0]),
                          _block_diag_pairs(rg_wa[1]), _block_diag_pairs(rg_wx[1])], axis=-1).astype(BF16)
    bg = jnp.concatenate([rg_ba[0].reshape(n_grp, 1, C), rg_bx[0].reshape(n_grp, 1, C),
                          rg_ba[1].reshape(n_grp, 1, C), rg_bx[1].reshape(n_grp, 1, C)], axis=-1)
    lam = rg_lambda.reshape(2, n_grp, C).transpose(1, 0, 2)
    step_rows = SUBLANES * _rnn_seg_len(S)
    nat_rows = step_rows + 2 * SUBLANES
    return pl.pallas_call(
        functools.partial(_rglru_kernel, S=S),
        out_shape=jax.ShapeDtypeStruct((B, S, D_RNN), BF16),
        grid=(B, n_grp),
        in_specs=[
            pl.BlockSpec((1, S, C), lambda b, c: (b, 0, c)),
            pl.BlockSpec((1, S, C), lambda b, c: (b, 0, n_grp + c)),
            pl.BlockSpec((CONV_W, C), lambda b, c: (0, c)),
            pl.BlockSpec((1, C), lambda b, c: (0, c)),
            pl.BlockSpec((1, C, 4 * C), lambda b, c: (c, 0, 0)),
            pl.BlockSpec((1, 1, 4 * C), lambda b, c: (c, 0, 0)),
            pl.BlockSpec((1, 2, C), lambda b, c: (c, 0, 0)),
        ],
        out_specs=pl.BlockSpec((1, S, C), lambda b, c: (b, 0, c)),
        scratch_shapes=[pltpu.VMEM((nat_rows, C), F32), pltpu.VMEM((nat_rows, C), F32),
                        pltpu.VMEM((step_rows, C), F32), pltpu.VMEM((step_rows, C), F32)],
        compiler_params=_params("parallel", "parallel"),
        name="rglru",
    )(z, z, conv_w, conv_b.reshape(1, -1), wg, bg, lam)


def _attn_kernel(q_ref, k_ref, v_ref, o_ref, l_ref, *, L):
    BQ, KW = ATT_BQ, ATT_KW
    lane = lax.broadcasted_iota(jnp.int32, (BQ, LANES), 1)
    first = lane < HEAD_DIM
    rel = lax.broadcasted_iota(jnp.int32, (BQ, KW), 0) - lax.broadcasted_iota(jnp.int32, (BQ, KW), 1)
    n_blocks = L // BQ

    def score_cap(offset):
        return jnp.where(jnp.abs(rel + offset) <= RADIUS, jnp.inf, NEG_INF).astype(F32)

    def block(q0, k0, cap):
        q = q_ref[0, 0, pl.ds(q0, BQ), :]
        k = k_ref[0, 0, pl.ds(k0, KW), :]
        v = v_ref[0, 0, pl.ds(k0, KW), :]
        outs, lses = [], []
        for sel in (first, jnp.logical_not(first)):
            qh = jnp.where(sel, q, jnp.zeros_like(q))
            s = lax.dot_general(qh, k, (((1,), (1,)), ((), ())), preferred_element_type=F32)
            s = jnp.minimum(s, cap)
            m = jnp.max(s, axis=-1, keepdims=True)
            p = jnp.exp(s - m)
            den = jnp.sum(p, axis=-1, keepdims=True)
            o = jnp.dot(p.astype(BF16), v, preferred_element_type=F32)
            outs.append(o / den)
            lses.append(m + jnp.log(den))
        o_ref[0, 0, pl.ds(q0, BQ), :] = jnp.where(first, outs[0], outs[1]).astype(BF16)
        l_ref[0, 0, pl.ds(q0, BQ), :] = jnp.where(first, lses[0], lses[1])

    block(0, 0, score_cap(0))
    block(L - BQ, L - KW, score_cap(KW - BQ))
    n_inner = n_blocks - 2
    assert n_inner % ATT_UNROLL == 0
    cap = score_cap(RADIUS)

    def inner(it, carry):
        for j in range(ATT_UNROLL):
            q0 = pl.multiple_of((1 + it * ATT_UNROLL + j) * BQ, BQ)
            block(q0, pl.multiple_of(q0 - RADIUS, RADIUS), cap)
        return carry

    lax.fori_loop(0, n_inner // ATT_UNROLL, inner, 0)


def _attention(qkv):
    B, dil, L, _ = qkv.shape
    n_pair = GROUP_W // LANES
    spec = lambda off: pl.BlockSpec((1, 1, L, LANES), lambda b, r, p: (b, r, 0, off + p))
    return pl.pallas_call(
        functools.partial(_attn_kernel, L=L),
        out_shape=[jax.ShapeDtypeStruct((B, dil, L, GROUP_W), BF16),
                   jax.ShapeDtypeStruct((B, dil, L, GROUP_W), F32)],
        grid=(B, dil, n_pair),
        in_specs=[spec(0), spec(n_pair), spec(2 * n_pair)],
        out_specs=[spec(0), spec(0)],
        compiler_params=_params("parallel", "parallel", "parallel"),
        name="attn",
    )(qkv, qkv, qkv)


def _merge_kernel(x_ref, gt_ref, mg_ref, rnn_ref, o0_ref, l0_ref, o1_ref, l1_ref, o2_ref, l2_ref,
                  wr_ref, wa_ref, wo_ref, y_ref, fo1_ref, fl1_ref, fo2_ref, fl2_ref, *, tm):
    n_tiles = GROUP_W // LANES

    def unfold(src_ref, dst_ref, dil):
        for r in range(dil):
            for s in range(n_tiles):
                dst_ref[s, pl.ds(r, tm // dil, stride=dil), :] = src_ref[0, r, :, s * LANES:(s + 1) * LANES].astype(F32)
        return jnp.concatenate([dst_ref[s] for s in range(n_tiles)], axis=1)

    o1 = unfold(o1_ref, fo1_ref, ATT_GROUPS[1][1])
    l1 = unfold(l1_ref, fl1_ref, ATT_GROUPS[1][1])
    o2 = unfold(o2_ref, fo2_ref, ATT_GROUPS[2][1])
    l2 = unfold(l2_ref, fl2_ref, ATT_GROUPS[2][1])
    lses = (l0_ref[0, 0], l1, l2)
    outs = (o0_ref[0, 0].astype(F32), o1, o2)
    top = jnp.maximum(jnp.maximum(lses[0], lses[1]), lses[2])
    es = [jnp.exp(l - top) for l in lses]
    att = (es[0] * outs[0] + es[1] * outs[1] + es[2] * outs[2]) / (es[0] + es[1] + es[2])

    gate = _sigmoid(mg_ref[0].astype(F32))
    br_r = jnp.dot(rnn_ref[0], wr_ref[...], preferred_element_type=F32)
    br_a = jnp.dot(att.astype(BF16), wa_ref[...], preferred_element_type=F32)
    merged = gate[:, :D_MODEL] * br_r + gate[:, D_MODEL:] * br_a
    mix = jnp.dot(merged.astype(BF16), wo_ref[...], preferred_element_type=F32)
    y_ref[0] = x_ref[0] + gt_ref[0] * mix


def _merge(x, gt, z, rnn, attn_outs, w_br_rnn, w_br_attn, w_out, tm):
    B, S, D = x.shape
    row = lambda b, i: (b, 0, 0)
    in_specs = [
        pl.BlockSpec((1, tm, D), lambda b, i: (b, i, 0)),
        pl.BlockSpec((1, 1, D), row),
        pl.BlockSpec((1, tm, 2 * D_MODEL), lambda b, i: (b, i, 1)),
        pl.BlockSpec((1, tm, D_RNN), lambda b, i: (b, i, 0)),
    ]
    args = [x, gt, z, rnn]
    for (o, l), (_, d) in zip(attn_outs, ATT_GROUPS):
        blk = pl.BlockSpec((1, d, tm // d, GROUP_W), lambda b, i: (b, 0, i, 0))
        in_specs += [blk, blk]
        args += [o, l]
    in_specs += [_resident(w_br_rnn.shape), _resident(w_br_attn.shape), _resident(w_out.shape)]
    args += [w_br_rnn, w_br_attn, w_out]
    return pl.pallas_call(
        functools.partial(_merge_kernel, tm=tm),
        out_shape=jax.ShapeDtypeStruct((B, S, D), F32),
        grid=(B, S // tm),
        in_specs=in_specs,
        out_specs=pl.BlockSpec((1, tm, D), lambda b, i: (b, i, 0)),
        scratch_shapes=[pltpu.VMEM((GROUP_W // LANES, tm, LANES), F32)] * 4,
        compiler_params=_params("parallel", "parallel"),
        name="merge",
    )(*args)


def _ffn_kernel(x_ref, sc_ref, sh_ref, gt_ref, g2_ref, gf_ref, wi_ref, wo_ref, y_ref):
    x = x_ref[0]
    h = _rms_mod(x, g2_ref[...], sc_ref[0], sh_ref[0]).astype(BF16)
    ff = None
    for c in range(D_FF // FF_CHUNK):
        lo = c * FF_CHUNK
        fg = jnp.dot(h, wi_ref[:, lo:lo + FF_CHUNK], preferred_element_type=F32)
        fu = jnp.dot(h, wi_ref[:, D_FF + lo:D_FF + lo + FF_CHUNK], preferred_element_type=F32)
        act = ((fg * _sigmoid(fg)) * fu).astype(BF16)
        part = jnp.dot(act, wo_ref[lo:lo + FF_CHUNK, :], preferred_element_type=F32)
        ff = part if ff is None else ff + part
    x2 = x + gt_ref[0] * ff
    y = x2 * lax.rsqrt(jnp.mean(x2 * x2, axis=-1, keepdims=True) + EPS)
    y_ref[0] = y * gf_ref[...]


def _ffn(x, sc, sh, gt, g2, gf, w_ffn_in, w_ffn_out, tm):
    B, S, D = x.shape
    row = lambda b, i: (b, 0, 0)
    return pl.pallas_call(
        _ffn_kernel,
        out_shape=jax.ShapeDtypeStruct((B, S, D), F32),
        grid=(B, S // tm),
        in_specs=[
            pl.BlockSpec((1, tm, D), lambda b, i: (b, i, 0)),
            pl.BlockSpec((1, 1, D), row),
            pl.BlockSpec((1, 1, D), row),
            pl.BlockSpec((1, 1, D), row),
            _resident((1, D)),
            _resident((1, D)),
            _resident(w_ffn_in.shape),
            _resident(w_ffn_out.shape),
        ],
        out_specs=pl.BlockSpec((1, tm, D), lambda b, i: (b, i, 0)),
        compiler_params=_params("parallel", "parallel"),
        name="ffn",
    )(x, sc, sh, gt, g2, gf, w_ffn_in, w_ffn_out)


def _encode(x, c, w, tm_in, tm_merge, tm_ffn):
    mod = _modulation(c, w["w_ada"], w["b_ada"])
    sh1, sc1, gt1, sh2, sc2, gt2 = (mod[:, k] for k in range(N_MOD))
    outs = _inproj(x, sc1, sh1, w["norm1_g"], w["w_in"], tm_in)
    z, qkvs = outs[0], outs[1:]
    rnn = _rglru(z, w["conv_w"], w["conv_b"], w["rg_wa"], w["rg_ba"], w["rg_wx"], w["rg_bx"], w["rg_lambda"])
    attn_outs = [_attention(qkv) for qkv in qkvs]
    x1 = _merge(x, gt1, z, rnn, attn_outs, w["w_br_rnn"], w["w_br_attn"], w["w_out"], tm_merge)
    return _ffn(x1, sc2, sh2, gt2, w["norm2_g"], w["final_g"], w["w_ffn_in"], w["w_ffn_out"], tm_ffn)


def kernel(x_prompt, x_sample, c_prompt, c_sample, w_ada, b_ada, norm1_g, w_in, conv_w, conv_b, rg_wa, rg_ba, rg_wx, rg_bx, rg_lambda, w_br_rnn, w_br_attn, w_out, norm2_g, w_ffn_in, w_ffn_out, final_g):
    assert w_ada.shape[0] == 1, "single layer"
    w = dict(
        w_ada=w_ada[0].astype(BF16), b_ada=b_ada[0], norm1_g=norm1_g[0].reshape(1, -1),
        w_in=w_in[0].astype(BF16), conv_w=conv_w[0], conv_b=conv_b[0],
        rg_wa=rg_wa[0], rg_ba=rg_ba[0], rg_wx=rg_wx[0], rg_bx=rg_bx[0], rg_lambda=rg_lambda[0],
        w_br_rnn=w_br_rnn[0].astype(BF16), w_br_attn=w_br_attn[0].astype(BF16), w_out=w_out[0].astype(BF16),
        norm2_g=norm2_g[0].reshape(1, -1), w_ffn_in=w_ffn_in[0].astype(BF16),
        w_ffn_out=w_ffn_out[0].astype(BF16), final_g=final_g.reshape(1, -1),
    )
    tiles = dict(tm_in=512, tm_merge=512, tm_ffn=512)
    return (_encode(x_prompt, c_prompt, w, **tiles), _encode(x_sample, c_sample, w, **tiles))
```

```python
import functools

import jax
import jax.numpy as jnp
from jax import lax
from jax.experimental import pallas as pl
from jax.experimental.pallas import tpu as pltpu

F32 = jnp.float32
BF16 = jnp.bfloat16

D_MODEL = 1024
D_RNN = 1024
RG_BLOCKS = 16
RG_BW = D_RNN // RG_BLOCKS
RG_C = 8.0
CONV_W = 4
CONV_LEFT = 2
HEAD_DIM = 64
HEADS_PER_GROUP = 8
ATT_GROUPS = ((128, 1), (512, 4), (2048, 16))
N_GROUPS = len(ATT_GROUPS)
GROUP_W = HEADS_PER_GROUP * HEAD_DIM
ATT_W = N_GROUPS * GROUP_W
ROT_DIM = HEAD_DIM // 4
ROPE_THETA = 500000.0
D_FF = 2816
N_MOD = 6
EPS = 1e-6
NEG_INF = -1e30
IN_COLS = 2 * D_RNN + 3 * ATT_W + 2 * D_MODEL
RADIUS = 64

LANES = 128
SUBLANES = 8
VMEM_LIMIT_BYTES = 56 * 1024 * 1024

COL_CHUNK = 512
RNN_CG = 128
RNN_T = 256
RNN_TB = 32
RNN_SEG_EXTRA = 4
ATT_BQ = 128
ATT_KW = ATT_BQ + 2 * RADIUS
ATT_UNROLL = 4
FF_CHUNK = 256


def _resident(shape):
    nd = len(shape)
    return pl.BlockSpec(shape, lambda *_: (0,) * nd, pipeline_mode=pl.Buffered(1))


def _params(*sem):
    return pltpu.CompilerParams(dimension_semantics=sem, vmem_limit_bytes=VMEM_LIMIT_BYTES)


def _rms_mod(x, g, sc, sh):
    y = x * lax.rsqrt(jnp.mean(x * x, axis=-1, keepdims=True) + EPS)
    return (y * g) * (1.0 + sc) + sh


def _sigmoid(x):
    return 1.0 / (1.0 + jnp.exp(-x))


def _mod_kernel(c_ref, w_ref, b_ref, o_ref):
    c = c_ref[...]
    s = (c * _sigmoid(c)).astype(BF16)
    o_ref[...] = jnp.dot(s, w_ref[...], preferred_element_type=F32) + b_ref[...]


def _modulation(c, w_ada, b_ada):
    B = c.shape[0]
    rows = -(-B // 16) * 16
    cp = jnp.pad(c, ((0, rows - B), (0, 0)))
    out = pl.pallas_call(
        _mod_kernel,
        out_shape=jax.ShapeDtypeStruct((rows, N_MOD * D_MODEL), F32),
        grid=(N_MOD,),
        in_specs=[
            pl.BlockSpec((rows, D_MODEL), lambda j: (0, 0)),
            pl.BlockSpec((D_MODEL, D_MODEL), lambda j: (0, j)),
            pl.BlockSpec((1, D_MODEL), lambda j: (0, j)),
        ],
        out_specs=pl.BlockSpec((rows, D_MODEL), lambda j: (0, j)),
        compiler_params=_params("parallel"),
        name="mod",
    )(cp, w_ada, b_ada.reshape(1, -1))
    return out[:B].reshape(B, N_MOD, 1, D_MODEL)


def _inproj_kernel(x_ref, sc_ref, sh_ref, g_ref, w_ref, cos_ref, sa_ref, sb_ref,
                   z_ref, q0_ref, q1_ref, q2_ref, fold_ref, *, tm):
    h = _rms_mod(x_ref[0], g_ref[...], sc_ref[0], sh_ref[0]).astype(BF16)

    def proj(j):
        return jnp.dot(h, w_ref[:, j * COL_CHUNK:(j + 1) * COL_CHUNK], preferred_element_type=F32)

    def rope(v):
        parts = []
        for s in range(COL_CHUNK // LANES):
            p = v[:, s * LANES:(s + 1) * LANES]
            up = pltpu.roll(p, LANES - ROT_DIM // 2, axis=1)
            dn = pltpu.roll(p, ROT_DIM // 2, axis=1)
            parts.append(p * cos_ref[...] + up * sa_ref[...] + dn * sb_ref[...])
        return jnp.concatenate(parts, axis=1)

    qkv_refs = (q0_ref, q1_ref, q2_ref)

    def emit_folded(val, g, off):
        dil = ATT_GROUPS[g][1]
        ref = qkv_refs[g]
        if dil == 1:
            ref[0, 0, :, off:off + COL_CHUNK] = val.astype(BF16)
            return
        for s in range(COL_CHUNK // LANES):
            fold_ref[s] = val[:, s * LANES:(s + 1) * LANES]
        for r in range(dil):
            for s in range(COL_CHUNK // LANES):
                lo = off + s * LANES
                ref[0, r, :, lo:lo + LANES] = fold_ref[s, pl.ds(r, tm // dil, stride=dil), :].astype(BF16)

    n_rnn = 2 * D_RNN // COL_CHUNK
    for j in range(n_rnn):
        z_ref[0, :, j * COL_CHUNK:(j + 1) * COL_CHUNK] = proj(j).astype(BF16)
    for g in range(N_GROUPS):
        emit_folded(rope(proj(n_rnn + g)) * (HEAD_DIM ** -0.5), g, 0)
        emit_folded(rope(proj(n_rnn + N_GROUPS + g)), g, GROUP_W)
        emit_folded(proj(n_rnn + 2 * N_GROUPS + g), g, 2 * GROUP_W)
    n_gate0 = n_rnn + 3 * N_GROUPS
    for j in range(2 * D_MODEL // COL_CHUNK):
        z_ref[0, :, (n_rnn + j) * COL_CHUNK:(n_rnn + j + 1) * COL_CHUNK] = proj(n_gate0 + j).astype(BF16)


def _rope_tables(S):
    half = ROT_DIM // 2
    inv = ROPE_THETA ** (-(jnp.arange(0, ROT_DIM, 2, dtype=F32) / ROT_DIM))
    ang = jnp.arange(S, dtype=F32)[:, None] * inv[None, :]
    cos, sin = jnp.cos(ang), jnp.sin(ang)
    zeros = jnp.zeros((S, HEAD_DIM - ROT_DIM), F32)
    z8 = jnp.zeros((S, half), F32)
    c = jnp.concatenate([cos, cos, zeros + 1.0], axis=1)
    sa = jnp.concatenate([-sin, z8, zeros], axis=1)
    sb = jnp.concatenate([z8, sin, zeros], axis=1)
    rep = LANES // HEAD_DIM
    return tuple(jnp.tile(t, (1, rep)) for t in (c, sa, sb))


def _inproj(x, sc, sh, g, w_in, tm):
    B, S, D = x.shape
    cos, sa, sb = _rope_tables(S)
    dils = [d for _, d in ATT_GROUPS]
    row = lambda b, i: (b, 0, 0)
    out_shape = [jax.ShapeDtypeStruct((B, S, 2 * D_RNN + 2 * D_MODEL), BF16)]
    out_specs = [pl.BlockSpec((1, tm, 2 * D_RNN + 2 * D_MODEL), lambda b, i: (b, i, 0))]
    for d in dils:
        out_shape.append(jax.ShapeDtypeStruct((B, d, S // d, 3 * GROUP_W), BF16))
        out_specs.append(pl.BlockSpec((1, d, tm // d, 3 * GROUP_W), lambda b, i: (b, 0, i, 0)))
    tab = pl.BlockSpec((tm, LANES), lambda b, i: (i, 0))
    return pl.pallas_call(
        functools.partial(_inproj_kernel, tm=tm),
        out_shape=out_shape,
        grid=(B, S // tm),
        in_specs=[
            pl.BlockSpec((1, tm, D), lambda b, i: (b, i, 0)),
            pl.BlockSpec((1, 1, D), row),
            pl.BlockSpec((1, 1, D), row),
            _resident((1, D)),
            _resident((D, IN_COLS)),
            tab, tab, tab,
        ],
        out_specs=out_specs,
        scratch_shapes=[pltpu.VMEM((COL_CHUNK // LANES, tm, LANES), F32)],
        compiler_params=_params("parallel", "parallel"),
        name="inproj",
    )(x, sc, sh, g, w_in, cos, sa, sb)


def _rnn_seg_len(S):
    assert S % (8 * SUBLANES) == 0
    return S // SUBLANES + RNN_SEG_EXTRA


def _rglru_kernel(x_ref, gate_ref, cw_ref, cb_ref, wg_ref, bg_ref, lam_ref, o_ref,
                  xnat_ref, hnat_ref, hloc_ref, acum_ref, *, S):
    C = RNN_CG
    T = RNN_T
    TB = RNN_TB
    PAD = SUBLANES
    seg = _rnn_seg_len(S)
    n_main = (seg - RNN_SEG_EXTRA) // TB
    rows = xnat_ref.shape[0]

    xnat_ref[0:PAD, :] = jnp.zeros((PAD, C), F32)
    xnat_ref[PAD + S:rows, :] = jnp.zeros((rows - PAD - S, C), F32)

    def fill(c, carry):
        r0 = pl.multiple_of(c * T, T)
        xnat_ref[pl.ds(r0 + PAD, T), :] = x_ref[0, pl.ds(r0, T), :].astype(F32)
        return carry

    lax.fori_loop(0, S // T, fill, 0)

    neg_lam = -lam_ref[0]
    softplus = jnp.maximum(neg_lam, 0.0) + jnp.log1p(jnp.exp(-jnp.abs(neg_lam)))
    coef = -RG_C * softplus
    cwb = [jnp.broadcast_to(cw_ref[k:k + 1, :], (SUBLANES, C)) for k in range(CONV_W)]
    cbb = jnp.broadcast_to(cb_ref[...], (SUBLANES, C))
    sub = lax.broadcasted_iota(jnp.int32, (SUBLANES, C), 0)
    steps_left = S - sub * seg

    def step_rows(t):
        return (pl.ds(PAD + t, SUBLANES, stride=seg), slice(None))

    def block_gates(t0, n, direction):
        taps = [xnat_ref[step_rows(t0 + m - CONV_LEFT)] for m in range(n + CONV_W - 1)]
        xcs = []
        for j in range(n):
            acc = cbb + taps[j] * cwb[0]
            for k in range(1, CONV_W):
                acc = acc + taps[j + k] * cwb[k]
            xcs.append(acc)
        xc = jnp.concatenate(xcs, axis=0)
        lo = direction * 2 * C
        gz = jnp.dot(xc.astype(BF16), wg_ref[0, :, lo:lo + 2 * C], preferred_element_type=F32)
        gz = gz + bg_ref[0, :, lo:lo + 2 * C]
        r = _sigmoid(gz[:, :C])
        i = _sigmoid(gz[:, C:])
        a = jnp.exp(coef[direction:direction + 1] * r)
        u = jnp.sqrt(1.0 - a * a) * (i * xc)
        return a, u

    def scan_block(t0, n, direction, carry):
        h, acc = carry
        a, u = block_gates(t0, n, direction)
        order = range(n) if direction == 0 else range(n - 1, -1, -1)
        for j in order:
            aj = a[j * SUBLANES:(j + 1) * SUBLANES]
            uj = u[j * SUBLANES:(j + 1) * SUBLANES]
            if direction == 1:
                uj = jnp.where(t0 + j < steps_left, uj, 0.0)
            h = aj * h + uj
            acc = aj * acc
            row = pl.multiple_of((t0 + j) * SUBLANES, SUBLANES)
            hloc_ref[direction, pl.ds(row, SUBLANES), :] = h
            acum_ref[direction, pl.ds(row, SUBLANES), :] = acc
        return h, acc

    def entry_states(h_tot, a_tot, direction):
        c = jnp.zeros((1, C), F32)
        out = jnp.zeros((SUBLANES, C), F32)
        order = range(SUBLANES) if direction == 0 else range(SUBLANES - 1, -1, -1)
        for s in order:
            out = jnp.where(sub == s, c, out)
            c = a_tot[s:s + 1] * c + h_tot[s:s + 1]
        return out

    def fix_block(t0, n, entries):
        for j in range(n):
            row = pl.multiple_of((t0 + j) * SUBLANES, SUBLANES)
            h = None
            for direction in (0, 1):
                part = (hloc_ref[direction, pl.ds(row, SUBLANES), :]
                        + acum_ref[direction, pl.ds(row, SUBLANES), :] * entries[direction])
                h = part if h is None else h + part
            hnat_ref[step_rows(t0 + j)] = h

    tail0 = n_main * TB
    init = (jnp.zeros((SUBLANES, C), F32), jnp.ones((SUBLANES, C), F32))

    def main(b, carry):
        fwd = scan_block(pl.multiple_of(b * TB, TB), TB, 0, carry[0])
        bwd = scan_block(pl.multiple_of((n_main - 1 - b) * TB, TB), TB, 1, carry[1])
        return fwd, bwd

    bwd_tail = scan_block(tail0, RNN_SEG_EXTRA, 1, init)
    fwd_tot, bwd_tot = lax.fori_loop(0, n_main, main, (init, bwd_tail))
    fwd_tot = scan_block(tail0, RNN_SEG_EXTRA, 0, fwd_tot)
    entries = (entry_states(*fwd_tot, 0), entry_states(*bwd_tot, 1))

    def fix(b, carry):
        fix_block(pl.multiple_of(b * TB, TB), TB, entries)
        return carry

    lax.fori_loop(0, n_main, fix, 0)
    fix_block(tail0, RNN_SEG_EXTRA, entries)

    def finish(c, carry):
        r0 = pl.multiple_of(c * T, T)
        gate = gate_ref[0, pl.ds(r0, T), :].astype(F32)
        gelu = 0.5 * gate * (1.0 + jnp.tanh(0.7978845608028654 * (gate + 0.044715 * (gate * gate * gate))))
        o_ref[0, pl.ds(r0, T), :] = (hnat_ref[pl.ds(r0 + PAD, T), :] * gelu).astype(BF16)
        return carry

    lax.fori_loop(0, S // T, finish, 0)


def _block_diag_pairs(w):
    per = RNN_CG // RG_BW
    w = w.reshape(RG_BLOCKS // per, per, RG_BW, RG_BW)
    rows = []
    for p in range(per):
        cols = [w[:, p] if q == p else jnp.zeros_like(w[:, p]) for q in range(per)]
        rows.append(jnp.concatenate(cols, axis=-1))
    return jnp.concatenate(rows, axis=1)


def _rglru(z, conv_w, conv_b, rg_wa, rg_ba, rg_wx, rg_bx, rg_lambda):
    B, S, _ = z.shape
    C = RNN_CG
    n_grp = D_RNN // C
    wg = jnp.concatenate([_block_diag_pairs(rg_wa[0]), _block_diag_pairs(rg_wx[0]),
                          _block_diag_pairs(rg_wa[1]), _block_diag_pairs(rg_wx[1])], axis=-1).astype(BF16)
    bg = jnp.concatenate([rg_ba[0].reshape(n_grp, 1, C), rg_bx[0].reshape(n_grp, 1, C),
                          rg_ba[1].reshape(n_grp, 1, C), rg_bx[1].reshape(n_grp, 1, C)], axis=-1)
    lam = rg_lambda.reshape(2, n_grp, C).transpose(1, 0, 2)
    step_rows = SUBLANES * _rnn_seg_len(S)
    nat_rows = step_rows + 2 * SUBLANES
    return pl.pallas_call(
        functools.partial(_rglru_kernel, S=S),
        out_shape=jax.ShapeDtypeStruct((B, S, D_RNN), BF16),
        grid=(B, n_grp),
        in_specs=[
            pl.BlockSpec((1, S, C), lambda b, c: (b, 0, c)),
            pl.BlockSpec((1, S, C), lambda b, c: (b, 0, n_grp + c)),
            pl.BlockSpec((CONV_W, C), lambda b, c: (0, c)),
            pl.BlockSpec((1, C), lambda b, c: (0, c)),
            pl.BlockSpec((1, C, 4 * C), lambda b, c: (c, 0, 0)),
            pl.BlockSpec((1, 1, 4 * C), lambda b, c: (c, 0, 0)),
            pl.BlockSpec((1, 2, C), lambda b, c: (c, 0, 0)),
        ],
        out_specs=pl.BlockSpec((1, S, C), lambda b, c: (b, 0, c)),
        scratch_shapes=[pltpu.VMEM((nat_rows, C), F32), pltpu.VMEM((nat_rows, C), F32),
                        pltpu.VMEM((2, step_rows, C), F32), pltpu.VMEM((2, step_rows, C), F32)],
        compiler_params=_params("parallel", "parallel"),
        name="rglru",
    )(z, z, conv_w, conv_b.reshape(1, -1), wg, bg, lam)


def _attn_kernel(q_ref, k_ref, v_ref, o_ref, l_ref, cap_ref, *, L, n_res):
    BQ, KW = ATT_BQ, ATT_KW
    lane = lax.broadcasted_iota(jnp.int32, (BQ, LANES), 1)
    first = lane < HEAD_DIM
    first_kw = lax.broadcasted_iota(jnp.int32, (KW, LANES), 1) < HEAD_DIM
    rel = lax.broadcasted_iota(jnp.int32, (BQ, KW), 0) - lax.broadcasted_iota(jnp.int32, (BQ, KW), 1)
    n_blocks = L // BQ
    total = n_res * n_blocks
    assert n_blocks & (n_blocks - 1) == 0 and total % ATT_UNROLL == 0

    for j in range(3):
        cap = jnp.where(jnp.abs(rel + j * RADIUS) <= RADIUS, jnp.inf, NEG_INF).astype(F32)
        cap_ref[j, 0:BQ, :] = cap
        cap_ref[j, BQ:2 * BQ, :] = cap

    def block(idx):
        r = lax.shift_right_logical(idx, n_blocks.bit_length() - 1)
        q0 = pl.multiple_of(jnp.bitwise_and(idx, n_blocks - 1) * BQ, BQ)
        k0 = pl.multiple_of(jnp.clip(q0 - RADIUS, 0, L - KW), RADIUS)
        cap = cap_ref[lax.shift_right_logical(q0 - k0, RADIUS.bit_length() - 1)]
        q = q_ref[0, r, pl.ds(q0, BQ), :]
        k = k_ref[0, r, pl.ds(k0, KW), :]
        v = v_ref[0, r, pl.ds(k0, KW), :]
        zero = jnp.zeros_like(q)
        q2 = jnp.concatenate([jnp.where(first, q, zero), jnp.where(first, zero, q)], axis=0)
        s = lax.dot_general(q2, k, (((1,), (1,)), ((), ())), preferred_element_type=F32)
        s = jnp.minimum(s, cap)
        m = jnp.max(s, axis=-1, keepdims=True)
        p = jnp.exp(s - m).astype(BF16)
        one = jnp.ones_like(v)
        pv_a = jnp.dot(p[:BQ], jnp.where(first_kw, v, one), preferred_element_type=F32)
        pv_b = jnp.dot(p[BQ:], jnp.where(first_kw, one, v), preferred_element_type=F32)
        num = jnp.where(first, pv_a, pv_b)
        den = pltpu.roll(jnp.where(first, pv_b, pv_a), HEAD_DIM, axis=1)
        top = jnp.where(first, m[:BQ], m[BQ:])
        o_ref[0, r, pl.ds(q0, BQ), :] = (num / den).astype(BF16)
        l_ref[0, r, pl.ds(q0, BQ), :] = top + jnp.log(den)

    def body(it, carry):
        for j in range(ATT_UNROLL):
            block(it * ATT_UNROLL + j)
        return carry

    lax.fori_loop(0, total // ATT_UNROLL, body, 0)


def _attention(qkv):
    B, dil, L, _ = qkv.shape
    n_pair = GROUP_W // LANES
    spec = lambda off: pl.BlockSpec((1, dil, L, LANES), lambda b, p: (b, 0, 0, off + p))
    return pl.pallas_call(
        functools.partial(_attn_kernel, L=L, n_res=dil),
        out_shape=[jax.ShapeDtypeStruct((B, dil, L, GROUP_W), BF16),
                   jax.ShapeDtypeStruct((B, dil, L, GROUP_W), F32)],
        grid=(B, n_pair),
        in_specs=[spec(0), spec(n_pair), spec(2 * n_pair)],
        out_specs=[spec(0), spec(0)],
        scratch_shapes=[pltpu.VMEM((3, 2 * ATT_BQ, ATT_KW), F32)],
        compiler_params=_params("parallel", "parallel"),
        name="attn",
    )(qkv, qkv, qkv)


def _merge_kernel(x_ref, gt_ref, mg_ref, rnn_ref, o0_ref, l0_ref, o1_ref, l1_ref, o2_ref, l2_ref,
                  wr_ref, wa_ref, wo_ref, y_ref, fo1_ref, fl1_ref, fo2_ref, fl2_ref, *, tm):
    n_tiles = GROUP_W // LANES

    def unfold(src_ref, dst_ref, dil):
        for r in range(dil):
            for s in range(n_tiles):
                dst_ref[s, pl.ds(r, tm // dil, stride=dil), :] = src_ref[0, r, :, s * LANES:(s + 1) * LANES].astype(F32)
        return jnp.concatenate([dst_ref[s] for s in range(n_tiles)], axis=1)

    o1 = unfold(o1_ref, fo1_ref, ATT_GROUPS[1][1])
    l1 = unfold(l1_ref, fl1_ref, ATT_GROUPS[1][1])
    o2 = unfold(o2_ref, fo2_ref, ATT_GROUPS[2][1])
    l2 = unfold(l2_ref, fl2_ref, ATT_GROUPS[2][1])
    lses = (l0_ref[0, 0], l1, l2)
    outs = (o0_ref[0, 0].astype(F32), o1, o2)
    top = jnp.maximum(jnp.maximum(lses[0], lses[1]), lses[2])
    es = [jnp.exp(l - top) for l in lses]
    att = (es[0] * outs[0] + es[1] * outs[1] + es[2] * outs[2]) / (es[0] + es[1] + es[2])

    gate = _sigmoid(mg_ref[0].astype(F32))
    br_r = jnp.dot(rnn_ref[0], wr_ref[...], preferred_element_type=F32)
    br_a = jnp.dot(att.astype(BF16), wa_ref[...], preferred_element_type=F32)
    merged = gate[:, :D_MODEL] * br_r + gate[:, D_MODEL:] * br_a
    mix = jnp.dot(merged.astype(BF16), wo_ref[...], preferred_element_type=F32)
    y_ref[0] = x_ref[0] + gt_ref[0] * mix


def _merge(x, gt, z, rnn, attn_outs, w_br_rnn, w_br_attn, w_out, tm):
    B, S, D = x.shape
    row = lambda b, i: (b, 0, 0)
    in_specs = [
        pl.BlockSpec((1, tm, D), lambda b, i: (b, i, 0)),
        pl.BlockSpec((1, 1, D), row),
        pl.BlockSpec((1, tm, 2 * D_MODEL), lambda b, i: (b, i, 1)),
        pl.BlockSpec((1, tm, D_RNN), lambda b, i: (b, i, 0)),
    ]
    args = [x, gt, z, rnn]
    for (o, l), (_, d) in zip(attn_outs, ATT_GROUPS):
        blk = pl.BlockSpec((1, d, tm // d, GROUP_W), lambda b, i: (b, 0, i, 0))
        in_specs += [blk, blk]
        args += [o, l]
    in_specs += [_resident(w_br_rnn.shape), _resident(w_br_attn.shape), _resident(w_out.shape)]
    args += [w_br_rnn, w_br_attn, w_out]
    return pl.pallas_call(
        functools.partial(_merge_kernel, tm=tm),
        out_shape=jax.ShapeDtypeStruct((B, S, D), F32),
        grid=(B, S // tm),
        in_specs=in_specs,
        out_specs=pl.BlockSpec((1, tm, D), lambda b, i: (b, i, 0)),
        scratch_shapes=[pltpu.VMEM((GROUP_W // LANES, tm, LANES), F32)] * 4,
        compiler_params=_params("parallel", "parallel"),
        name="merge",
    )(*args)


def _ffn_kernel(x_ref, sc_ref, sh_ref, gt_ref, g2_ref, gf_ref, wi_ref, wo_ref, y_ref):
    x = x_ref[0]
    h = _rms_mod(x, g2_ref[...], sc_ref[0], sh_ref[0]).astype(BF16)
    ff = None
    for c in range(D_FF // FF_CHUNK):
        lo = c * FF_CHUNK
        fg = jnp.dot(h, wi_ref[:, lo:lo + FF_CHUNK], preferred_element_type=F32)
        fu = jnp.dot(h, wi_ref[:, D_FF + lo:D_FF + lo + FF_CHUNK], preferred_element_type=F32)
        act = ((fg * _sigmoid(fg)) * fu).astype(BF16)
        part = jnp.dot(act, wo_ref[lo:lo + FF_CHUNK, :], preferred_element_type=F32)
        ff = part if ff is None else ff + part
    x2 = x + gt_ref[0] * ff
    y = x2 * lax.rsqrt(jnp.mean(x2 * x2, axis=-1, keepdims=True) + EPS)
    y_ref[0] = y * gf_ref[...]


def _ffn(x, sc, sh, gt, g2, gf, w_ffn_in, w_ffn_out, tm):
    B, S, D = x.shape
    row = lambda b, i: (b, 0, 0)
    return pl.pallas_call(
        _ffn_kernel,
        out_shape=jax.ShapeDtypeStruct((B, S, D), F32),
        grid=(B, S // tm),
        in_specs=[
            pl.BlockSpec((1, tm, D), lambda b, i: (b, i, 0)),
            pl.BlockSpec((1, 1, D), row),
            pl.BlockSpec((1, 1, D), row),
            pl.BlockSpec((1, 1, D), row),
            _resident((1, D)),
            _resident((1, D)),
            _resident(w_ffn_in.shape),
            _resident(w_ffn_out.shape),
        ],
        out_specs=pl.BlockSpec((1, tm, D), lambda b, i: (b, i, 0)),
        compiler_params=_params("parallel", "parallel"),
        name="ffn",
    )(x, sc, sh, gt, g2, gf, w_ffn_in, w_ffn_out)


def _encode(x, c, w, tm_in, tm_merge, tm_ffn):
    mod = _modulation(c, w["w_ada"], w["b_ada"])
    sh1, sc1, gt1, sh2, sc2, gt2 = (mod[:, k] for k in range(N_MOD))
    outs = _inproj(x, sc1, sh1, w["norm1_g"], w["w_in"], tm_in)
    z, qkvs = outs[0], outs[1:]
    rnn = _rglru(z, w["conv_w"], w["conv_b"], w["rg_wa"], w["rg_ba"], w["rg_wx"], w["rg_bx"], w["rg_lambda"])
    attn_outs = [_attention(qkv) for qkv in qkvs]
    x1 = _merge(x, gt1, z, rnn, attn_outs, w["w_br_rnn"], w["w_br_attn"], w["w_out"], tm_merge)
    return _ffn(x1, sc2, sh2, gt2, w["norm2_g"], w["final_g"], w["w_ffn_in"], w["w_ffn_out"], tm_ffn)


def kernel(x_prompt, x_sample, c_prompt, c_sample, w_ada, b_ada, norm1_g, w_in, conv_w, conv_b, rg_wa, rg_ba, rg_wx, rg_bx, rg_lambda, w_br_rnn, w_br_attn, w_out, norm2_g, w_ffn_in, w_ffn_out, final_g):
    assert w_ada.shape[0] == 1, "single layer"
    w = dict(
        w_ada=w_ada[0].astype(BF16), b_ada=b_ada[0], norm1_g=norm1_g[0].reshape(1, -1),
        w_in=w_in[0].astype(BF16), conv_w=conv_w[0], conv_b=conv_b[0],
        rg_wa=rg_wa[0], rg_ba=rg_ba[0], rg_wx=rg_wx[0], rg_bx=rg_bx[0], rg_lambda=rg_lambda[0],
        w_br_rnn=w_br_rnn[0].astype(BF16), w_br_attn=w_br_attn[0].astype(BF16), w_out=w_out[0].astype(BF16),
        norm2_g=norm2_g[0].reshape(1, -1), w_ffn_in=w_ffn_in[0].astype(BF16),
        w_ffn_out=w_ffn_out[0].astype(BF16), final_g=final_g.reshape(1, -1),
    )
    tiles = dict(tm_in=512, tm_merge=512, tm_ffn=512)
    return (_encode(x_prompt, c_prompt, w, **tiles), _encode(x_sample, c_sample, w, **tiles))
```

```python
import functools

import jax
import jax.numpy as jnp
from jax import lax
from jax.experimental import pallas as pl
from jax.experimental.pallas import tpu as pltpu

F32 = jnp.float32
BF16 = jnp.bfloat16

D_MODEL = 1024
D_RNN = 1024
RG_BLOCKS = 16
RG_BW = D_RNN // RG_BLOCKS
RG_C = 8.0
CONV_W = 4
CONV_LEFT = 2
HEAD_DIM = 64
HEADS_PER_GROUP = 8
ATT_GROUPS = ((128, 1), (512, 4), (2048, 16))
N_GROUPS = len(ATT_GROUPS)
GROUP_W = HEADS_PER_GROUP * HEAD_DIM
ATT_W = N_GROUPS * GROUP_W
ROT_DIM = HEAD_DIM // 4
ROPE_THETA = 500000.0
D_FF = 2816
N_MOD = 6
EPS = 1e-6
NEG_INF = -1e30
IN_COLS = 2 * D_RNN + 3 * ATT_W + 2 * D_MODEL
RADIUS = 64

LANES = 128
SUBLANES = 8
VMEM_LIMIT_BYTES = 56 * 1024 * 1024

COL_CHUNK = 512
RNN_CG = 128
RNN_T = 256
RNN_TB = 32
RNN_SEG_EXTRA = 4
ATT_BQ = 128
ATT_KW = ATT_BQ + 2 * RADIUS
ATT_UNROLL = 8
FF_CHUNK = 256


def _resident(shape):
    nd = len(shape)
    return pl.BlockSpec(shape, lambda *_: (0,) * nd, pipeline_mode=pl.Buffered(1))


def _params(*sem):
    return pltpu.CompilerParams(dimension_semantics=sem, vmem_limit_bytes=VMEM_LIMIT_BYTES)


def _rms_mod(x, g, sc, sh):
    y = x * lax.rsqrt(jnp.mean(x * x, axis=-1, keepdims=True) + EPS)
    return (y * g) * (1.0 + sc) + sh


def _sigmoid(x):
    return 1.0 / (1.0 + jnp.exp(-x))


def _mod_kernel(c_ref, w_ref, b_ref, o_ref):
    c = c_ref[...]
    s = (c * _sigmoid(c)).astype(BF16)
    o_ref[...] = jnp.dot(s, w_ref[...], preferred_element_type=F32) + b_ref[...]


def _modulation(c, w_ada, b_ada):
    B = c.shape[0]
    rows = -(-B // 16) * 16
    cp = jnp.pad(c, ((0, rows - B), (0, 0)))
    out = pl.pallas_call(
        _mod_kernel,
        out_shape=jax.ShapeDtypeStruct((rows, N_MOD * D_MODEL), F32),
        grid=(N_MOD,),
        in_specs=[
            pl.BlockSpec((rows, D_MODEL), lambda j: (0, 0)),
            pl.BlockSpec((D_MODEL, D_MODEL), lambda j: (0, j)),
            pl.BlockSpec((1, D_MODEL), lambda j: (0, j)),
        ],
        out_specs=pl.BlockSpec((rows, D_MODEL), lambda j: (0, j)),
        compiler_params=_params("parallel"),
        name="mod",
    )(cp, w_ada, b_ada.reshape(1, -1))
    return out[:B].reshape(B, N_MOD, 1, D_MODEL)


def _inproj_kernel(x_ref, sc_ref, sh_ref, g_ref, w_ref, cos_ref, sa_ref, sb_ref,
                   z_ref, q0_ref, q1_ref, q2_ref, fold_ref, *, tm):
    h = _rms_mod(x_ref[0], g_ref[...], sc_ref[0], sh_ref[0]).astype(BF16)

    def proj(j):
        return jnp.dot(h, w_ref[:, j * COL_CHUNK:(j + 1) * COL_CHUNK], preferred_element_type=F32)

    def rope(v):
        parts = []
        for s in range(COL_CHUNK // LANES):
            p = v[:, s * LANES:(s + 1) * LANES]
            up = pltpu.roll(p, LANES - ROT_DIM // 2, axis=1)
            dn = pltpu.roll(p, ROT_DIM // 2, axis=1)
            parts.append(p * cos_ref[...] + up * sa_ref[...] + dn * sb_ref[...])
        return jnp.concatenate(parts, axis=1)

    qkv_refs = (q0_ref, q1_ref, q2_ref)

    def emit_folded(val, g, off):
        dil = ATT_GROUPS[g][1]
        ref = qkv_refs[g]
        if dil == 1:
            ref[0, 0, :, off:off + COL_CHUNK] = val.astype(BF16)
            return
        for s in range(COL_CHUNK // LANES):
            fold_ref[s] = val[:, s * LANES:(s + 1) * LANES]
        for r in range(dil):
            for s in range(COL_CHUNK // LANES):
                lo = off + s * LANES
                ref[0, r, :, lo:lo + LANES] = fold_ref[s, pl.ds(r, tm // dil, stride=dil), :].astype(BF16)

    n_rnn = 2 * D_RNN // COL_CHUNK
    for j in range(n_rnn):
        z_ref[0, :, j * COL_CHUNK:(j + 1) * COL_CHUNK] = proj(j).astype(BF16)
    for g in range(N_GROUPS):
        emit_folded(rope(proj(n_rnn + g)) * (HEAD_DIM ** -0.5), g, 0)
        emit_folded(rope(proj(n_rnn + N_GROUPS + g)), g, GROUP_W)
        emit_folded(proj(n_rnn + 2 * N_GROUPS + g), g, 2 * GROUP_W)
    n_gate0 = n_rnn + 3 * N_GROUPS
    for j in range(2 * D_MODEL // COL_CHUNK):
        z_ref[0, :, (n_rnn + j) * COL_CHUNK:(n_rnn + j + 1) * COL_CHUNK] = proj(n_gate0 + j).astype(BF16)


def _rope_tables(S):
    half = ROT_DIM // 2
    inv = ROPE_THETA ** (-(jnp.arange(0, ROT_DIM, 2, dtype=F32) / ROT_DIM))
    ang = jnp.arange(S, dtype=F32)[:, None] * inv[None, :]
    cos, sin = jnp.cos(ang), jnp.sin(ang)
    zeros = jnp.zeros((S, HEAD_DIM - ROT_DIM), F32)
    z8 = jnp.zeros((S, half), F32)
    c = jnp.concatenate([cos, cos, zeros + 1.0], axis=1)
    sa = jnp.concatenate([-sin, z8, zeros], axis=1)
    sb = jnp.concatenate([z8, sin, zeros], axis=1)
    rep = LANES // HEAD_DIM
    return tuple(jnp.tile(t, (1, rep)) for t in (c, sa, sb))


def _inproj(x, sc, sh, g, w_in, tm):
    B, S, D = x.shape
    cos, sa, sb = _rope_tables(S)
    dils = [d for _, d in ATT_GROUPS]
    row = lambda b, i: (b, 0, 0)
    out_shape = [jax.ShapeDtypeStruct((B, S, 2 * D_RNN + 2 * D_MODEL), BF16)]
    out_specs = [pl.BlockSpec((1, tm, 2 * D_RNN + 2 * D_MODEL), lambda b, i: (b, i, 0))]
    for d in dils:
        out_shape.append(jax.ShapeDtypeStruct((B, d, S // d, 3 * GROUP_W), BF16))
        out_specs.append(pl.BlockSpec((1, d, tm // d, 3 * GROUP_W), lambda b, i: (b, 0, i, 0)))
    tab = pl.BlockSpec((tm, LANES), lambda b, i: (i, 0))
    return pl.pallas_call(
        functools.partial(_inproj_kernel, tm=tm),
        out_shape=out_shape,
        grid=(B, S // tm),
        in_specs=[
            pl.BlockSpec((1, tm, D), lambda b, i: (b, i, 0)),
            pl.BlockSpec((1, 1, D), row),
            pl.BlockSpec((1, 1, D), row),
            _resident((1, D)),
            _resident((D, IN_COLS)),
            tab, tab, tab,
        ],
        out_specs=out_specs,
        scratch_shapes=[pltpu.VMEM((COL_CHUNK // LANES, tm, LANES), F32)],
        compiler_params=_params("parallel", "parallel"),
        name="inproj",
    )(x, sc, sh, g, w_in, cos, sa, sb)


def _rnn_seg_len(S):
    assert S % (8 * SUBLANES) == 0
    return S // SUBLANES + RNN_SEG_EXTRA


def _rglru_kernel(x_ref, gate_ref, cw_ref, cb_ref, wg_ref, bg_ref, lam_ref, o_ref,
                  xnat_ref, hnat_ref, hloc_ref, acum_ref, *, S):
    C = RNN_CG
    T = RNN_T
    TB = RNN_TB
    PAD = SUBLANES
    seg = _rnn_seg_len(S)
    n_main = (seg - RNN_SEG_EXTRA) // TB
    rows = xnat_ref.shape[0]

    xnat_ref[0:PAD, :] = jnp.zeros((PAD, C), F32)
    xnat_ref[PAD + S:rows, :] = jnp.zeros((rows - PAD - S, C), F32)

    def fill(c, carry):
        r0 = pl.multiple_of(c * T, T)
        xnat_ref[pl.ds(r0 + PAD, T), :] = x_ref[0, pl.ds(r0, T), :].astype(F32)
        return carry

    lax.fori_loop(0, S // T, fill, 0)

    neg_lam = -lam_ref[0]
    softplus = jnp.maximum(neg_lam, 0.0) + jnp.log1p(jnp.exp(-jnp.abs(neg_lam)))
    coef = -RG_C * softplus
    cwb = [jnp.broadcast_to(cw_ref[k:k + 1, :], (SUBLANES, C)) for k in range(CONV_W)]
    cbb = jnp.broadcast_to(cb_ref[...], (SUBLANES, C))
    sub = lax.broadcasted_iota(jnp.int32, (SUBLANES, C), 0)
    steps_left = S - sub * seg

    def step_rows(t):
        return (pl.ds(PAD + t, SUBLANES, stride=seg), slice(None))

    def block_gates(t0, n, direction):
        taps = [xnat_ref[step_rows(t0 + m - CONV_LEFT)] for m in range(n + CONV_W - 1)]
        xcs = []
        for j in range(n):
            acc = cbb + taps[j] * cwb[0]
            for k in range(1, CONV_W):
                acc = acc + taps[j + k] * cwb[k]
            xcs.append(acc)
        xc = jnp.concatenate(xcs, axis=0)
        lo = direction * 2 * C
        gz = jnp.dot(xc.astype(BF16), wg_ref[0, :, lo:lo + 2 * C], preferred_element_type=F32)
        gz = gz + bg_ref[0, :, lo:lo + 2 * C]
        r = _sigmoid(gz[:, :C])
        i = _sigmoid(gz[:, C:])
        a = jnp.exp(coef[direction:direction + 1] * r)
        u = jnp.sqrt(1.0 - a * a) * (i * xc)
        return a, u

    def scan_block(t0, n, direction, carry):
        h, acc = carry
        a, u = block_gates(t0, n, direction)
        order = range(n) if direction == 0 else range(n - 1, -1, -1)
        for j in order:
            aj = a[j * SUBLANES:(j + 1) * SUBLANES]
            uj = u[j * SUBLANES:(j + 1) * SUBLANES]
            if direction == 1:
                uj = jnp.where(t0 + j < steps_left, uj, 0.0)
            h = aj * h + uj
            acc = aj * acc
            row = pl.multiple_of((t0 + j) * SUBLANES, SUBLANES)
            hloc_ref[direction, pl.ds(row, SUBLANES), :] = h
            acum_ref[direction, pl.ds(row, SUBLANES), :] = acc
        return h, acc

    def entry_states(h_tot, a_tot, direction):
        c = jnp.zeros((1, C), F32)
        out = jnp.zeros((SUBLANES, C), F32)
        order = range(SUBLANES) if direction == 0 else range(SUBLANES - 1, -1, -1)
        for s in order:
            out = jnp.where(sub == s, c, out)
            c = a_tot[s:s + 1] * c + h_tot[s:s + 1]
        return out

    def fix_block(t0, n, entries):
        for j in range(n):
            row = pl.multiple_of((t0 + j) * SUBLANES, SUBLANES)
            h = None
            for direction in (0, 1):
                part = (hloc_ref[direction, pl.ds(row, SUBLANES), :]
                        + acum_ref[direction, pl.ds(row, SUBLANES), :] * entries[direction])
                h = part if h is None else h + part
            hnat_ref[step_rows(t0 + j)] = h

    tail0 = n_main * TB
    init = (jnp.zeros((SUBLANES, C), F32), jnp.ones((SUBLANES, C), F32))

    def main(b, carry):
        fwd = scan_block(pl.multiple_of(b * TB, TB), TB, 0, carry[0])
        bwd = scan_block(pl.multiple_of((n_main - 1 - b) * TB, TB), TB, 1, carry[1])
        return fwd, bwd

    bwd_tail = scan_block(tail0, RNN_SEG_EXTRA, 1, init)
    fwd_tot, bwd_tot = lax.fori_loop(0, n_main, main, (init, bwd_tail))
    fwd_tot = scan_block(tail0, RNN_SEG_EXTRA, 0, fwd_tot)
    entries = (entry_states(*fwd_tot, 0), entry_states(*bwd_tot, 1))

    def fix(b, carry):
        fix_block(pl.multiple_of(b * TB, TB), TB, entries)
        return carry

    lax.fori_loop(0, n_main, fix, 0)
    fix_block(tail0, RNN_SEG_EXTRA, entries)

    def finish(c, carry):
        r0 = pl.multiple_of(c * T, T)
        gate = gate_ref[0, pl.ds(r0, T), :].astype(F32)
        gelu = 0.5 * gate * (1.0 + jnp.tanh(0.7978845608028654 * (gate + 0.044715 * (gate * gate * gate))))
        o_ref[0, pl.ds(r0, T), :] = (hnat_ref[pl.ds(r0 + PAD, T), :] * gelu).astype(BF16)
        return carry

    lax.fori_loop(0, S // T, finish, 0)


def _block_diag_pairs(w):
    per = RNN_CG // RG_BW
    w = w.reshape(RG_BLOCKS // per, per, RG_BW, RG_BW)
    rows = []
    for p in range(per):
        cols = [w[:, p] if q == p else jnp.zeros_like(w[:, p]) for q in range(per)]
        rows.append(jnp.concatenate(cols, axis=-1))
    return jnp.concatenate(rows, axis=1)


def _rglru(z, conv_w, conv_b, rg_wa, rg_ba, rg_wx, rg_bx, rg_lambda):
    B, S, _ = z.shape
    C = RNN_CG
    n_grp = D_RNN // C
    wg = jnp.concatenate([_block_diag_pairs(rg_wa[0]), _block_diag_pairs(rg_wx[0]),
                          _block_diag_pairs(rg_wa[1]), _block_diag_pairs(rg_wx[1])], axis=-1).astype(BF16)
    bg = jnp.concatenate([rg_ba[0].reshape(n_grp, 1, C), rg_bx[0].reshape(n_grp, 1, C),
                          rg_ba[1].reshape(n_grp, 1, C), rg_bx[1].reshape(n_grp, 1, C)], axis=-1)
    lam = rg_lambda.reshape(2, n_grp, C).transpose(1, 0, 2)
    step_rows = SUBLANES * _rnn_seg_len(S)
    nat_rows = step_rows + 2 * SUBLANES
    return pl.pallas_call(
        functools.partial(_rglru_kernel, S=S),
        out_shape=jax.ShapeDtypeStruct((B, S, D_RNN), BF16),
        grid=(B, n_grp),
        in_specs=[
            pl.BlockSpec((1, S, C), lambda b, c: (b, 0, c)),
            pl.BlockSpec((1, S, C), lambda b, c: (b, 0, n_grp + c)),
            pl.BlockSpec((CONV_W, C), lambda b, c: (0, c)),
            pl.BlockSpec((1, C), lambda b, c: (0, c)),
            pl.BlockSpec((1, C, 4 * C), lambda b, c: (c, 0, 0)),
            pl.BlockSpec((1, 1, 4 * C), lambda b, c: (c, 0, 0)),
            pl.BlockSpec((1, 2, C), lambda b, c: (c, 0, 0)),
        ],
        out_specs=pl.BlockSpec((1, S, C), lambda b, c: (b, 0, c)),
        scratch_shapes=[pltpu.VMEM((nat_rows, C), F32), pltpu.VMEM((nat_rows, C), F32),
                        pltpu.VMEM((2, step_rows, C), F32), pltpu.VMEM((2, step_rows, C), F32)],
        compiler_params=_params("parallel", "parallel"),
        name="rglru",
    )(z, z, conv_w, conv_b.reshape(1, -1), wg, bg, lam)


def _attn_kernel(q_ref, k_ref, v_ref, o_ref, l_ref, cap_ref, *, L, n_res):
    BQ, KW = ATT_BQ, ATT_KW
    lane = lax.broadcasted_iota(jnp.int32, (BQ, LANES), 1)
    first = lane < HEAD_DIM
    first_kw = lax.broadcasted_iota(jnp.int32, (KW, LANES), 1) < HEAD_DIM
    rel = lax.broadcasted_iota(jnp.int32, (BQ, KW), 0) - lax.broadcasted_iota(jnp.int32, (BQ, KW), 1)
    n_blocks = L // BQ
    total = n_res * n_blocks
    assert n_blocks & (n_blocks - 1) == 0 and total % ATT_UNROLL == 0

    for j in range(3):
        cap = jnp.where(jnp.abs(rel + j * RADIUS) <= RADIUS, jnp.inf, NEG_INF).astype(F32)
        cap_ref[j, 0:BQ, :] = cap
        cap_ref[j, BQ:2 * BQ, :] = cap

    def block(idx):
        r = lax.shift_right_logical(idx, n_blocks.bit_length() - 1)
        q0 = pl.multiple_of(jnp.bitwise_and(idx, n_blocks - 1) * BQ, BQ)
        k0 = pl.multiple_of(jnp.clip(q0 - RADIUS, 0, L - KW), RADIUS)
        cap = cap_ref[lax.shift_right_logical(q0 - k0, RADIUS.bit_length() - 1)]
        q = q_ref[0, r, pl.ds(q0, BQ), :]
        k = k_ref[0, r, pl.ds(k0, KW), :]
        v = v_ref[0, r, pl.ds(k0, KW), :]
        zero = jnp.zeros_like(q)
        q2 = jnp.concatenate([jnp.where(first, q, zero), jnp.where(first, zero, q)], axis=0)
        s = lax.dot_general(q2, k, (((1,), (1,)), ((), ())), preferred_element_type=F32)
        s = jnp.minimum(s, cap)
        m = jnp.max(s, axis=-1, keepdims=True)
        p = jnp.exp(s - m).astype(BF16)
        one = jnp.ones_like(v)
        pv_a = jnp.dot(p[:BQ], jnp.where(first_kw, v, one), preferred_element_type=F32)
        pv_b = jnp.dot(p[BQ:], jnp.where(first_kw, one, v), preferred_element_type=F32)
        num = jnp.where(first, pv_a, pv_b)
        den = pltpu.roll(jnp.where(first, pv_b, pv_a), HEAD_DIM, axis=1)
        top = jnp.where(first, m[:BQ], m[BQ:])
        o_ref[0, r, pl.ds(q0, BQ), :] = (num / den).astype(BF16)
        l_ref[0, r, pl.ds(q0, BQ), :] = top + jnp.log(den)

    def body(it, carry):
        for j in range(ATT_UNROLL):
            block(it * ATT_UNROLL + j)
        return carry

    lax.fori_loop(0, total // ATT_UNROLL, body, 0)


def _attention(qkv):
    B, dil, L, _ = qkv.shape
    n_pair = GROUP_W // LANES
    spec = lambda off: pl.BlockSpec((1, dil, L, LANES), lambda b, p: (b, 0, 0, off + p))
    return pl.pallas_call(
        functools.partial(_attn_kernel, L=L, n_res=dil),
        out_shape=[jax.ShapeDtypeStruct((B, dil, L, GROUP_W), BF16),
                   jax.ShapeDtypeStruct((B, dil, L, GROUP_W), F32)],
        grid=(B, n_pair),
        in_specs=[spec(0), spec(n_pair), spec(2 * n_pair)],
        out_specs=[spec(0), spec(0)],
        scratch_shapes=[pltpu.VMEM((3, 2 * ATT_BQ, ATT_KW), F32)],
        compiler_params=_params("parallel", "parallel"),
        name="attn",
    )(qkv, qkv, qkv)


def _mixffn_kernel(x_ref, gt1_ref, sc_ref, sh_ref, gt2_ref, g2_ref, gf_ref, mg_ref, rnn_ref,
                   o0_ref, l0_ref, o1_ref, l1_ref, o2_ref, l2_ref,
                   wr_ref, wa_ref, wo_ref, wi_ref, wf_ref, y_ref,
                   fo1_ref, fl1_ref, fo2_ref, fl2_ref, *, tm):
    n_tiles = GROUP_W // LANES

    def unfold(src_ref, dst_ref, dil):
        for r in range(dil):
            for s in range(n_tiles):
                dst_ref[s, pl.ds(r, tm // dil, stride=dil), :] = src_ref[0, r, :, s * LANES:(s + 1) * LANES].astype(F32)
        return jnp.concatenate([dst_ref[s] for s in range(n_tiles)], axis=1)

    o1 = unfold(o1_ref, fo1_ref, ATT_GROUPS[1][1])
    l1 = unfold(l1_ref, fl1_ref, ATT_GROUPS[1][1])
    o2 = unfold(o2_ref, fo2_ref, ATT_GROUPS[2][1])
    l2 = unfold(l2_ref, fl2_ref, ATT_GROUPS[2][1])
    lses = (l0_ref[0, 0], l1, l2)
    outs = (o0_ref[0, 0].astype(F32), o1, o2)
    top = jnp.maximum(jnp.maximum(lses[0], lses[1]), lses[2])
    es = [jnp.exp(l - top) for l in lses]
    att = (es[0] * outs[0] + es[1] * outs[1] + es[2] * outs[2]) / (es[0] + es[1] + es[2])

    gate = _sigmoid(mg_ref[0].astype(F32))
    br_r = jnp.dot(rnn_ref[0], wr_ref[...], preferred_element_type=F32)
    br_a = jnp.dot(att.astype(BF16), wa_ref[...], preferred_element_type=F32)
    merged = gate[:, :D_MODEL] * br_r + gate[:, D_MODEL:] * br_a
    mix = jnp.dot(merged.astype(BF16), wo_ref[...], preferred_element_type=F32)
    x1 = x_ref[0] + gt1_ref[0] * mix

    h = _rms_mod(x1, g2_ref[...], sc_ref[0], sh_ref[0]).astype(BF16)
    ff = None
    for c in range(D_FF // FF_CHUNK):
        lo = c * FF_CHUNK
        fg = jnp.dot(h, wi_ref[:, lo:lo + FF_CHUNK], preferred_element_type=F32)
        fu = jnp.dot(h, wi_ref[:, D_FF + lo:D_FF + lo + FF_CHUNK], preferred_element_type=F32)
        act = ((fg * _sigmoid(fg)) * fu).astype(BF16)
        part = jnp.dot(act, wf_ref[lo:lo + FF_CHUNK, :], preferred_element_type=F32)
        ff = part if ff is None else ff + part
    x2 = x1 + gt2_ref[0] * ff
    y = x2 * lax.rsqrt(jnp.mean(x2 * x2, axis=-1, keepdims=True) + EPS)
    y_ref[0] = y * gf_ref[...]


def _mixffn(x, mods, g2, gf, z, rnn, attn_outs, weights, tm):
    B, S, D = x.shape
    mod_spec = pl.BlockSpec((1, 1, D), lambda b, i: (b, 0, 0))
    in_specs = [pl.BlockSpec((1, tm, D), lambda b, i: (b, i, 0))] + [mod_spec] * len(mods)
    in_specs += [
        _resident((1, D)),
        _resident((1, D)),
        pl.BlockSpec((1, tm, 2 * D_MODEL), lambda b, i: (b, i, 1)),
        pl.BlockSpec((1, tm, D_RNN), lambda b, i: (b, i, 0)),
    ]
    args = [x, *mods, g2, gf, z, rnn]
    for (o, l), (_, d) in zip(attn_outs, ATT_GROUPS):
        blk = pl.BlockSpec((1, d, tm // d, GROUP_W), lambda b, i: (b, 0, i, 0))
        in_specs += [blk, blk]
        args += [o, l]
    in_specs += [_resident(w.shape) for w in weights]
    args += list(weights)
    return pl.pallas_call(
        functools.partial(_mixffn_kernel, tm=tm),
        out_shape=jax.ShapeDtypeStruct((B, S, D), F32),
        grid=(B, S // tm),
        in_specs=in_specs,
        out_specs=pl.BlockSpec((1, tm, D), lambda b, i: (b, i, 0)),
        scratch_shapes=[pltpu.VMEM((GROUP_W // LANES, tm, LANES), F32)] * 4,
        compiler_params=_params("parallel", "parallel"),
        name="mixffn",
    )(*args)


def _encode(x, c, w, tm_in, tm_mix):
    mod = _modulation(c, w["w_ada"], w["b_ada"])
    sh1, sc1, gt1, sh2, sc2, gt2 = (mod[:, k] for k in range(N_MOD))
    outs = _inproj(x, sc1, sh1, w["norm1_g"], w["w_in"], tm_in)
    z, qkvs = outs[0], outs[1:]
    rnn = _rglru(z, w["conv_w"], w["conv_b"], w["rg_wa"], w["rg_ba"], w["rg_wx"], w["rg_bx"], w["rg_lambda"])
    attn_outs = [_attention(qkv) for qkv in qkvs]
    weights = (w["w_br_rnn"], w["w_br_attn"], w["w_out"], w["w_ffn_in"], w["w_ffn_out"])
    return _mixffn(x, (gt1, sc2, sh2, gt2), w["norm2_g"], w["final_g"], z, rnn, attn_outs, weights, tm_mix)


def kernel(x_prompt, x_sample, c_prompt, c_sample, w_ada, b_ada, norm1_g, w_in, conv_w, conv_b, rg_wa, rg_ba, rg_wx, rg_bx, rg_lambda, w_br_rnn, w_br_attn, w_out, norm2_g, w_ffn_in, w_ffn_out, final_g):
    assert w_ada.shape[0] == 1, "single layer"
    w = dict(
        w_ada=w_ada[0].astype(BF16), b_ada=b_ada[0], norm1_g=norm1_g[0].reshape(1, -1),
        w_in=w_in[0].astype(BF16), conv_w=conv_w[0], conv_b=conv_b[0],
        rg_wa=rg_wa[0], rg_ba=rg_ba[0], rg_wx=rg_wx[0], rg_bx=rg_bx[0], rg_lambda=rg_lambda[0],
        w_br_rnn=w_br_rnn[0].astype(BF16), w_br_attn=w_br_attn[0].astype(BF16), w_out=w_out[0].astype(BF16),
        norm2_g=norm2_g[0].reshape(1, -1), w_ffn_in=w_ffn_in[0].astype(BF16),
        w_ffn_out=w_ffn_out[0].astype(BF16), final_g=final_g.reshape(1, -1),
    )
    tiles = dict(tm_in=512, tm_mix=512)
    return (_encode(x_prompt, c_prompt, w, **tiles), _encode(x_sample, c_sample, w, **tiles))
```

```python
import functools

import jax
import jax.numpy as jnp
from jax import lax
from jax.experimental import pallas as pl
from jax.experimental.pallas import tpu as pltpu

F32 = jnp.float32
BF16 = jnp.bfloat16

D_MODEL = 1024
D_RNN = 1024
RG_BLOCKS = 16
RG_BW = D_RNN // RG_BLOCKS
RG_C = 8.0
CONV_W = 4
CONV_LEFT = 2
HEAD_DIM = 64
HEADS_PER_GROUP = 8
ATT_GROUPS = ((128, 1), (512, 4), (2048, 16))
N_GROUPS = len(ATT_GROUPS)
GROUP_W = HEADS_PER_GROUP * HEAD_DIM
ATT_W = N_GROUPS * GROUP_W
ROT_DIM = HEAD_DIM // 4
ROPE_THETA = 500000.0
D_FF = 2816
N_MOD = 6
EPS = 1e-6
NEG_INF = -1e30
LOG2E = 1.4426950408889634
LN2 = 0.6931471805599453
IN_COLS = 2 * D_RNN + 3 * ATT_W + 2 * D_MODEL
RADIUS = 64

LANES = 128
SUBLANES = 8
VMEM_LIMIT_BYTES = 56 * 1024 * 1024

COL_CHUNK = 512
RNN_CG = 128
RNN_T = 256
RNN_TB = 128
RNN_SEG_EXTRA = 4
ATT_BQ = 128
ATT_KW = ATT_BQ + 2 * RADIUS
ATT_UNROLL = 8
FF_CHUNK = 256


def _resident(shape):
    nd = len(shape)
    return pl.BlockSpec(shape, lambda *_: (0,) * nd, pipeline_mode=pl.Buffered(1))


def _params(*sem):
    return pltpu.CompilerParams(dimension_semantics=sem, vmem_limit_bytes=VMEM_LIMIT_BYTES)


def _rms_mod(x, g, sc, sh):
    y = x * lax.rsqrt(jnp.mean(x * x, axis=-1, keepdims=True) + EPS)
    return (y * g) * (1.0 + sc) + sh


def _sigmoid(x):
    return 1.0 / (1.0 + jnp.exp2(x * -LOG2E))


def _mod_kernel(c_ref, w_ref, b_ref, o_ref):
    c = c_ref[...]
    s = (c * _sigmoid(c)).astype(BF16)
    o_ref[...] = jnp.dot(s, w_ref[...], preferred_element_type=F32) + b_ref[...]


def _modulation(c, w_ada, b_ada):
    B = c.shape[0]
    rows = -(-B // 16) * 16
    cp = jnp.pad(c, ((0, rows - B), (0, 0)))
    out = pl.pallas_call(
        _mod_kernel,
        out_shape=jax.ShapeDtypeStruct((rows, N_MOD * D_MODEL), F32),
        grid=(N_MOD,),
        in_specs=[
            pl.BlockSpec((rows, D_MODEL), lambda j: (0, 0)),
            pl.BlockSpec((D_MODEL, D_MODEL), lambda j: (0, j)),
            pl.BlockSpec((1, D_MODEL), lambda j: (0, j)),
        ],
        out_specs=pl.BlockSpec((rows, D_MODEL), lambda j: (0, j)),
        compiler_params=_params("parallel"),
        name="mod",
    )(cp, w_ada, b_ada.reshape(1, -1))
    return out[:B].reshape(B, N_MOD, 1, D_MODEL)


def _fold_pitch(dil):
    return dil if dil % (2 * SUBLANES) else dil + SUBLANES


def _inproj_kernel(x_ref, sc_ref, sh_ref, g_ref, w_ref, cos_ref, sa_ref, sb_ref,
                   z_ref, q0_ref, q1_ref, q2_ref, fold_ref, *, tm):
    h = _rms_mod(x_ref[0], g_ref[...], sc_ref[0], sh_ref[0]).astype(BF16)

    def proj(j):
        return jnp.dot(h, w_ref[:, j * COL_CHUNK:(j + 1) * COL_CHUNK], preferred_element_type=F32)

    def rope(v):
        parts = []
        for s in range(COL_CHUNK // LANES):
            p = v[:, s * LANES:(s + 1) * LANES]
            up = pltpu.roll(p, LANES - ROT_DIM // 2, axis=1)
            dn = pltpu.roll(p, ROT_DIM // 2, axis=1)
            parts.append(p * cos_ref[...] + up * sa_ref[...] + dn * sb_ref[...])
        return jnp.concatenate(parts, axis=1)

    qkv_refs = (q0_ref, q1_ref, q2_ref)

    def emit_folded(val, g, off):
        dil = ATT_GROUPS[g][1]
        ref = qkv_refs[g]
        if dil == 1:
            ref[0, 0, :, off:off + COL_CHUNK] = val.astype(BF16)
            return
        pitch = _fold_pitch(dil)
        for s in range(COL_CHUNK // LANES):
            slab = val[:, s * LANES:(s + 1) * LANES]
            if pitch == dil:
                fold_ref[s, 0:tm, :] = slab
            else:
                for m in range(tm // dil):
                    fold_ref[s, pitch * m:pitch * m + dil, :] = slab[dil * m:dil * (m + 1)]
        for r in range(dil):
            for s in range(COL_CHUNK // LANES):
                lo = off + s * LANES
                ref[0, r, :, lo:lo + LANES] = fold_ref[s, pl.ds(r, tm // dil, stride=pitch), :].astype(BF16)

    n_rnn = 2 * D_RNN // COL_CHUNK
    for j in range(n_rnn):
        z_ref[0, :, j * COL_CHUNK:(j + 1) * COL_CHUNK] = proj(j).astype(BF16)
    for g in range(N_GROUPS):
        emit_folded(rope(proj(n_rnn + g)) * (HEAD_DIM ** -0.5 * LOG2E), g, 0)
        emit_folded(rope(proj(n_rnn + N_GROUPS + g)), g, GROUP_W)
        emit_folded(proj(n_rnn + 2 * N_GROUPS + g), g, 2 * GROUP_W)
    n_gate0 = n_rnn + 3 * N_GROUPS
    for j in range(2 * D_MODEL // COL_CHUNK):
        z_ref[0, :, (n_rnn + j) * COL_CHUNK:(n_rnn + j + 1) * COL_CHUNK] = proj(n_gate0 + j).astype(BF16)


def _rope_tables(S):
    half = ROT_DIM // 2
    inv = ROPE_THETA ** (-(jnp.arange(0, ROT_DIM, 2, dtype=F32) / ROT_DIM))
    ang = jnp.arange(S, dtype=F32)[:, None] * inv[None, :]
    cos, sin = jnp.cos(ang), jnp.sin(ang)
    zeros = jnp.zeros((S, HEAD_DIM - ROT_DIM), F32)
    z8 = jnp.zeros((S, half), F32)
    c = jnp.concatenate([cos, cos, zeros + 1.0], axis=1)
    sa = jnp.concatenate([-sin, z8, zeros], axis=1)
    sb = jnp.concatenate([z8, sin, zeros], axis=1)
    rep = LANES // HEAD_DIM
    return tuple(jnp.tile(t, (1, rep)) for t in (c, sa, sb))


def _inproj(x, sc, sh, g, w_in, tm):
    B, S, D = x.shape
    cos, sa, sb = _rope_tables(S)
    dils = [d for _, d in ATT_GROUPS]
    row = lambda b, i: (b, 0, 0)
    out_shape = [jax.ShapeDtypeStruct((B, S, 2 * D_RNN + 2 * D_MODEL), BF16)]
    out_specs = [pl.BlockSpec((1, tm, 2 * D_RNN + 2 * D_MODEL), lambda b, i: (b, i, 0))]
    for d in dils:
        out_shape.append(jax.ShapeDtypeStruct((B, d, S // d, 3 * GROUP_W), BF16))
        out_specs.append(pl.BlockSpec((1, d, tm // d, 3 * GROUP_W), lambda b, i: (b, 0, i, 0)))
    tab = pl.BlockSpec((tm, LANES), lambda b, i: (i, 0))
    return pl.pallas_call(
        functools.partial(_inproj_kernel, tm=tm),
        out_shape=out_shape,
        grid=(B, S // tm),
        in_specs=[
            pl.BlockSpec((1, tm, D), lambda b, i: (b, i, 0)),
            pl.BlockSpec((1, 1, D), row),
            pl.BlockSpec((1, 1, D), row),
            _resident((1, D)),
            _resident((D, IN_COLS)),
            tab, tab, tab,
        ],
        out_specs=out_specs,
        scratch_shapes=[pltpu.VMEM((COL_CHUNK // LANES, max(_fold_pitch(d) * (tm // d) for d in dils), LANES), F32)],
        compiler_params=_params("parallel", "parallel"),
        name="inproj",
    )(x, sc, sh, g, w_in, cos, sa, sb)


def _rnn_seg_len(S):
    assert S % (8 * SUBLANES) == 0
    return S // SUBLANES + RNN_SEG_EXTRA


def _rglru_kernel(x_ref, gate_ref, cw_ref, cb_ref, wg_ref, bg_ref, lam_ref, o_ref,
                  xnat_ref, hnat_ref, hloc_ref, acum_ref, *, S):
    C = RNN_CG
    T = RNN_T
    TB = RNN_TB
    PAD = SUBLANES
    seg = _rnn_seg_len(S)
    n_main = (seg - RNN_SEG_EXTRA) // TB
    rows = xnat_ref.shape[0]

    xnat_ref[0:PAD, :] = jnp.zeros((PAD, C), F32)
    xnat_ref[PAD + S:rows, :] = jnp.zeros((rows - PAD - S, C), F32)

    def fill(c, carry):
        r0 = pl.multiple_of(c * T, T)
        xnat_ref[pl.ds(r0 + PAD, T), :] = x_ref[0, pl.ds(r0, T), :].astype(F32)
        return carry

    lax.fori_loop(0, S // T, fill, 0)

    neg_lam = -lam_ref[0]
    softplus = jnp.maximum(neg_lam, 0.0) + jnp.log1p(jnp.exp(-jnp.abs(neg_lam)))
    coef2 = (-RG_C * LOG2E) * softplus
    cwb = [jnp.broadcast_to(cw_ref[k:k + 1, :], (SUBLANES, C)) for k in range(CONV_W)]
    cbb = jnp.broadcast_to(cb_ref[...], (SUBLANES, C))
    sub = lax.broadcasted_iota(jnp.int32, (SUBLANES, C), 0)
    steps_left = S - sub * seg

    def step_rows(t):
        return (pl.ds(PAD + t, SUBLANES, stride=seg), slice(None))

    def block_gates(t0, n, direction):
        taps = [xnat_ref[step_rows(t0 + m - CONV_LEFT)] for m in range(n + CONV_W - 1)]
        xcs = []
        for j in range(n):
            acc = cbb + taps[j] * cwb[0]
            for k in range(1, CONV_W):
                acc = acc + taps[j + k] * cwb[k]
            xcs.append(acc)
        xc = jnp.concatenate(xcs, axis=0)
        lo = direction * 2 * C
        gz = jnp.dot(xc.astype(BF16), wg_ref[0, :, lo:lo + 2 * C], preferred_element_type=F32)
        gz = gz + bg_ref[0, :, lo:lo + 2 * C]
        r = _sigmoid(gz[:, :C])
        i = _sigmoid(gz[:, C:])
        a = jnp.exp2(coef2[direction:direction + 1] * r)
        y = 1.0 - a * a
        root = jnp.where(y > 0.0, y * lax.rsqrt(y), 0.0)
        u = root * (i * xc)
        return a, u

    def scan_block(t0, n, direction, carry):
        h, acc = carry
        a, u = block_gates(t0, n, direction)
        order = range(n) if direction == 0 else range(n - 1, -1, -1)
        for j in order:
            aj = a[j * SUBLANES:(j + 1) * SUBLANES]
            uj = u[j * SUBLANES:(j + 1) * SUBLANES]
            if direction == 1:
                uj = jnp.where(t0 + j < steps_left, uj, 0.0)
            h = aj * h + uj
            acc = aj * acc
            row = pl.multiple_of((t0 + j) * SUBLANES, SUBLANES)
            hloc_ref[direction, pl.ds(row, SUBLANES), :] = h
            acum_ref[direction, pl.ds(row, SUBLANES), :] = acc
        return h, acc

    def entry_states(h_tot, a_tot, direction):
        c = jnp.zeros((1, C), F32)
        out = jnp.zeros((SUBLANES, C), F32)
        order = range(SUBLANES) if direction == 0 else range(SUBLANES - 1, -1, -1)
        for s in order:
            out = jnp.where(sub == s, c, out)
            c = a_tot[s:s + 1] * c + h_tot[s:s + 1]
        return out

    def fix_block(t0, n, entries):
        for j in range(n):
            row = pl.multiple_of((t0 + j) * SUBLANES, SUBLANES)
            h = None
            for direction in (0, 1):
                part = (hloc_ref[direction, pl.ds(row, SUBLANES), :]
                        + acum_ref[direction, pl.ds(row, SUBLANES), :] * entries[direction])
                h = part if h is None else h + part
            hnat_ref[step_rows(t0 + j)] = h

    tail0 = n_main * TB
    init = (jnp.zeros((SUBLANES, C), F32), jnp.ones((SUBLANES, C), F32))

    def main(b, carry):
        fwd = scan_block(pl.multiple_of(b * TB, TB), TB, 0, carry[0])
        bwd = scan_block(pl.multiple_of((n_main - 1 - b) * TB, TB), TB, 1, carry[1])
        return fwd, bwd

    bwd_tail = scan_block(tail0, RNN_SEG_EXTRA, 1, init)
    fwd_tot, bwd_tot = lax.fori_loop(0, n_main, main, (init, bwd_tail))
    fwd_tot = scan_block(tail0, RNN_SEG_EXTRA, 0, fwd_tot)
    entries = (entry_states(*fwd_tot, 0), entry_states(*bwd_tot, 1))

    def fix(b, carry):
        fix_block(pl.multiple_of(b * TB, TB), TB, entries)
        return carry

    lax.fori_loop(0, n_main, fix, 0)
    fix_block(tail0, RNN_SEG_EXTRA, entries)

    def finish(c, carry):
        r0 = pl.multiple_of(c * T, T)
        gate = gate_ref[0, pl.ds(r0, T), :].astype(F32)
        gelu = 0.5 * gate * (1.0 + jnp.tanh(0.7978845608028654 * (gate + 0.044715 * (gate * gate * gate))))
        o_ref[0, pl.ds(r0, T), :] = (hnat_ref[pl.ds(r0 + PAD, T), :] * gelu).astype(BF16)
        return carry

    lax.fori_loop(0, S // T, finish, 0)


def _block_diag_pairs(w):
    per = RNN_CG // RG_BW
    w = w.reshape(RG_BLOCKS // per, per, RG_BW, RG_BW)
    rows = []
    for p in range(per):
        cols = [w[:, p] if q == p else jnp.zeros_like(w[:, p]) for q in range(per)]
        rows.append(jnp.concatenate(cols, axis=-1))
    return jnp.concatenate(rows, axis=1)


def _rglru(z, conv_w, conv_b, rg_wa, rg_ba, rg_wx, rg_bx, rg_lambda):
    B, S, _ = z.shape
    C = RNN_CG
    n_grp = D_RNN // C
    wg = jnp.concatenate([_block_diag_pairs(rg_wa[0]), _block_diag_pairs(rg_wx[0]),
                          _block_diag_pairs(rg_wa[1]), _block_diag_pairs(rg_wx[1])], axis=-1).astype(BF16)
    bg = jnp.concatenate([rg_ba[0].reshape(n_grp, 1, C), rg_bx[0].reshape(n_grp, 1, C),
                          rg_ba[1].reshape(n_grp, 1, C), rg_bx[1].reshape(n_grp, 1, C)], axis=-1)
    lam = rg_lambda.reshape(2, n_grp, C).transpose(1, 0, 2)
    step_rows = SUBLANES * _rnn_seg_len(S)
    nat_rows = step_rows + 2 * SUBLANES
    return pl.pallas_call(
        functools.partial(_rglru_kernel, S=S),
        out_shape=jax.ShapeDtypeStruct((B, S, D_RNN), BF16),
        grid=(B, n_grp),
        in_specs=[
            pl.BlockSpec((1, S, C), lambda b, c: (b, 0, c)),
            pl.BlockSpec((1, S, C), lambda b, c: (b, 0, n_grp + c)),
            pl.BlockSpec((CONV_W, C), lambda b, c: (0, c)),
            pl.BlockSpec((1, C), lambda b, c: (0, c)),
            pl.BlockSpec((1, C, 4 * C), lambda b, c: (c, 0, 0)),
            pl.BlockSpec((1, 1, 4 * C), lambda b, c: (c, 0, 0)),
            pl.BlockSpec((1, 2, C), lambda b, c: (c, 0, 0)),
        ],
        out_specs=pl.BlockSpec((1, S, C), lambda b, c: (b, 0, c)),
        scratch_shapes=[pltpu.VMEM((nat_rows, C), F32), pltpu.VMEM((nat_rows, C), F32),
                        pltpu.VMEM((2, step_rows, C), F32), pltpu.VMEM((2, step_rows, C), F32)],
        compiler_params=_params("parallel", "parallel"),
        name="rglru",
    )(z, z, conv_w, conv_b.reshape(1, -1), wg, bg, lam)


def _attn_kernel(q_ref, k_ref, v_ref, o_ref, l_ref, cap_ref, *, L, n_res):
    BQ, KW = ATT_BQ, ATT_KW
    lane = lax.broadcasted_iota(jnp.int32, (BQ, LANES), 1)
    first = lane < HEAD_DIM
    first_kw = lax.broadcasted_iota(jnp.int32, (KW, LANES), 1) < HEAD_DIM
    rel = lax.broadcasted_iota(jnp.int32, (BQ, KW), 0) - lax.broadcasted_iota(jnp.int32, (BQ, KW), 1)
    n_blocks = L // BQ
    total = n_res * n_blocks
    assert n_blocks & (n_blocks - 1) == 0 and total % ATT_UNROLL == 0

    for j in range(3):
        cap = jnp.where(jnp.abs(rel + j * RADIUS) <= RADIUS, jnp.inf, NEG_INF).astype(F32)
        cap_ref[j, 0:BQ, :] = cap
        cap_ref[j, BQ:2 * BQ, :] = cap

    def block(idx):
        r = lax.shift_right_logical(idx, n_blocks.bit_length() - 1)
        q0 = pl.multiple_of(jnp.bitwise_and(idx, n_blocks - 1) * BQ, BQ)
        k0 = pl.multiple_of(jnp.clip(q0 - RADIUS, 0, L - KW), RADIUS)
        cap = cap_ref[lax.shift_right_logical(q0 - k0, RADIUS.bit_length() - 1)]
        q = q_ref[0, r, pl.ds(q0, BQ), :]
        k = k_ref[0, r, pl.ds(k0, KW), :]
        v = v_ref[0, r, pl.ds(k0, KW), :]
        zero = jnp.zeros_like(q)
        q2 = jnp.concatenate([jnp.where(first, q, zero), jnp.where(first, zero, q)], axis=0)
        s = lax.dot_general(q2, k, (((1,), (1,)), ((), ())), preferred_element_type=F32)
        s = jnp.minimum(s, cap)
        m = jnp.max(s, axis=-1, keepdims=True)
        p = jnp.exp2(s - m).astype(BF16)
        one = jnp.ones_like(v)
        pv_a = jnp.dot(p[:BQ], jnp.where(first_kw, v, one), preferred_element_type=F32)
        pv_b = jnp.dot(p[BQ:], jnp.where(first_kw, one, v), preferred_element_type=F32)
        num = jnp.where(first, pv_a, pv_b)
        den = pltpu.roll(jnp.where(first, pv_b, pv_a), HEAD_DIM, axis=1)
        top = jnp.where(first, m[:BQ], m[BQ:])
        o_ref[0, r, pl.ds(q0, BQ), :] = (num / den).astype(BF16)
        l_ref[0, r, pl.ds(q0, BQ), :] = top * LN2 + jnp.log(den)

    def body(it, carry):
        for j in range(ATT_UNROLL):
            block(it * ATT_UNROLL + j)
        return carry

    lax.fori_loop(0, total // ATT_UNROLL, body, 0)


def _attention(qkv):
    B, dil, L, _ = qkv.shape
    n_pair = GROUP_W // LANES
    spec = lambda off: pl.BlockSpec((1, dil, L, LANES), lambda b, p: (b, 0, 0, off + p))
    return pl.pallas_call(
        functools.partial(_attn_kernel, L=L, n_res=dil),
        out_shape=[jax.ShapeDtypeStruct((B, dil, L, GROUP_W), BF16),
                   jax.ShapeDtypeStruct((B, dil, L, GROUP_W), F32)],
        grid=(B, n_pair),
        in_specs=[spec(0), spec(n_pair), spec(2 * n_pair)],
        out_specs=[spec(0), spec(0)],
        scratch_shapes=[pltpu.VMEM((3, 2 * ATT_BQ, ATT_KW), F32)],
        compiler_params=_params("parallel", "parallel"),
        name="attn",
    )(qkv, qkv, qkv)


def _mixffn_kernel(x_ref, gt1_ref, sc_ref, sh_ref, gt2_ref, g2_ref, gf_ref, mg_ref, rnn_ref,
                   o0_ref, l0_ref, o1_ref, l1_ref, o2_ref, l2_ref,
                   wr_ref, wa_ref, wo_ref, wi_ref, wf_ref, y_ref,
                   fo1_ref, fl1_ref, fo2_ref, fl2_ref, *, tm):
    n_tiles = GROUP_W // LANES

    def unfold(src_ref, dst_ref, dil):
        pitch = _fold_pitch(dil)
        for r in range(dil):
            for s in range(n_tiles):
                dst_ref[s, pl.ds(r, tm // dil, stride=pitch), :] = src_ref[0, r, :, s * LANES:(s + 1) * LANES].astype(F32)
        if pitch == dil:
            return jnp.concatenate([dst_ref[s, 0:tm, :] for s in range(n_tiles)], axis=1)
        return jnp.concatenate(
            [jnp.concatenate([dst_ref[s, pitch * m:pitch * m + dil, :] for m in range(tm // dil)], axis=0)
             for s in range(n_tiles)], axis=1)

    o1 = unfold(o1_ref, fo1_ref, ATT_GROUPS[1][1])
    l1 = unfold(l1_ref, fl1_ref, ATT_GROUPS[1][1])
    o2 = unfold(o2_ref, fo2_ref, ATT_GROUPS[2][1])
    l2 = unfold(l2_ref, fl2_ref, ATT_GROUPS[2][1])
    lses = (l0_ref[0, 0], l1, l2)
    outs = (o0_ref[0, 0].astype(F32), o1, o2)
    top = jnp.maximum(jnp.maximum(lses[0], lses[1]), lses[2])
    es = [jnp.exp(l - top) for l in lses]
    att = (es[0] * outs[0] + es[1] * outs[1] + es[2] * outs[2]) / (es[0] + es[1] + es[2])

    gate = _sigmoid(mg_ref[0].astype(F32))
    br_r = jnp.dot(rnn_ref[0], wr_ref[...], preferred_element_type=F32)
    br_a = jnp.dot(att.astype(BF16), wa_ref[...], preferred_element_type=F32)
    merged = gate[:, :D_MODEL] * br_r + gate[:, D_MODEL:] * br_a
    mix = jnp.dot(merged.astype(BF16), wo_ref[...], preferred_element_type=F32)
    x1 = x_ref[0] + gt1_ref[0] * mix

    h = _rms_mod(x1, g2_ref[...], sc_ref[0], sh_ref[0]).astype(BF16)
    ff = None
    for c in range(D_FF // FF_CHUNK):
        lo = c * FF_CHUNK
        fg = jnp.dot(h, wi_ref[:, lo:lo + FF_CHUNK], preferred_element_type=F32)
        fu = jnp.dot(h, wi_ref[:, D_FF + lo:D_FF + lo + FF_CHUNK], preferred_element_type=F32)
        act = ((fg * _sigmoid(fg)) * fu).astype(BF16)
        part = jnp.dot(act, wf_ref[lo:lo + FF_CHUNK, :], preferred_element_type=F32)
        ff = part if ff is None else ff + part
    x2 = x1 + gt2_ref[0] * ff
    y = x2 * lax.rsqrt(jnp.mean(x2 * x2, axis=-1, keepdims=True) + EPS)
    y_ref[0] = y * gf_ref[...]


def _mixffn(x, mods, g2, gf, z, rnn, attn_outs, weights, tm):
    B, S, D = x.shape
    mod_spec = pl.BlockSpec((1, 1, D), lambda b, i: (b, 0, 0))
    in_specs = [pl.BlockSpec((1, tm, D), lambda b, i: (b, i, 0))] + [mod_spec] * len(mods)
    in_specs += [
        _resident((1, D)),
        _resident((1, D)),
        pl.BlockSpec((1, tm, 2 * D_MODEL), lambda b, i: (b, i, 1)),
        pl.BlockSpec((1, tm, D_RNN), lambda b, i: (b, i, 0)),
    ]
    args = [x, *mods, g2, gf, z, rnn]
    for (o, l), (_, d) in zip(attn_outs, ATT_GROUPS):
        blk = pl.BlockSpec((1, d, tm // d, GROUP_W), lambda b, i: (b, 0, i, 0))
        in_specs += [blk, blk]
        args += [o, l]
    in_specs += [_resident(w.shape) for w in weights]
    args += list(weights)
    return pl.pallas_call(
        functools.partial(_mixffn_kernel, tm=tm),
        out_shape=jax.ShapeDtypeStruct((B, S, D), F32),
        grid=(B, S // tm),
        in_specs=in_specs,
        out_specs=pl.BlockSpec((1, tm, D), lambda b, i: (b, i, 0)),
        scratch_shapes=[pltpu.VMEM((GROUP_W // LANES, _fold_pitch(d) * (tm // d), LANES), F32)
                        for _, d in ATT_GROUPS[1:] for _ in range(2)],
        compiler_params=_params("parallel", "parallel"),
        name="mixffn",
    )(*args)


def _encode(x, c, w, tm_in, tm_mix):
    mod = _modulation(c, w["w_ada"], w["b_ada"])
    sh1, sc1, gt1, sh2, sc2, gt2 = (mod[:, k] for k in range(N_MOD))
    outs = _inproj(x, sc1, sh1, w["norm1_g"], w["w_in"], tm_in)
    z, qkvs = outs[0], outs[1:]
    rnn = _rglru(z, w["conv_w"], w["conv_b"], w["rg_wa"], w["rg_ba"], w["rg_wx"], w["rg_bx"], w["rg_lambda"])
    attn_outs = [_attention(qkv) for qkv in qkvs]
    weights = (w["w_br_rnn"], w["w_br_attn"], w["w_out"], w["w_ffn_in"], w["w_ffn_out"])
    return _mixffn(x, (gt1, sc2, sh2, gt2), w["norm2_g"], w["final_g"], z, rnn, attn_outs, weights, tm_mix)


def kernel(x_prompt, x_sample, c_prompt, c_sample, w_ada, b_ada, norm1_g, w_in, conv_w, conv_b, rg_wa, rg_ba, rg_wx, rg_bx, rg_lambda, w_br_rnn, w_br_attn, w_out, norm2_g, w_ffn_in, w_ffn_out, final_g):
    assert w_ada.shape[0] == 1, "single layer"
    w = dict(
        w_ada=w_ada[0].astype(BF16), b_ada=b_ada[0], norm1_g=norm1_g[0].reshape(1, -1),
        w_in=w_in[0].astype(BF16), conv_w=conv_w[0], conv_b=conv_b[0],
        rg_wa=rg_wa[0], rg_ba=rg_ba[0], rg_wx=rg_wx[0], rg_bx=rg_bx[0], rg_lambda=rg_lambda[0],
        w_br_rnn=w_br_rnn[0].astype(BF16), w_br_attn=w_br_attn[0].astype(BF16), w_out=w_out[0].astype(BF16),
        norm2_g=norm2_g[0].reshape(1, -1), w_ffn_in=w_ffn_in[0].astype(BF16),
        w_ffn_out=w_ffn_out[0].astype(BF16), final_g=final_g.reshape(1, -1),
    )
    tiles = dict(tm_in=512, tm_mix=512)
    return (_encode(x_prompt, c_prompt, w, **tiles), _encode(x_sample, c_sample, w, **tiles))
```

```python
import functools

import jax
import jax.numpy as jnp
from jax import lax
from jax.experimental import pallas as pl
from jax.experimental.pallas import tpu as pltpu

F32 = jnp.float32
BF16 = jnp.bfloat16

D_MODEL = 1024
D_RNN = 1024
RG_BLOCKS = 16
RG_BW = D_RNN // RG_BLOCKS
RG_C = 8.0
CONV_W = 4
CONV_LEFT = 2
HEAD_DIM = 64
HEADS_PER_GROUP = 8
ATT_GROUPS = ((128, 1), (512, 4), (2048, 16))
N_GROUPS = len(ATT_GROUPS)
GROUP_W = HEADS_PER_GROUP * HEAD_DIM
ATT_W = N_GROUPS * GROUP_W
ROT_DIM = HEAD_DIM // 4
ROPE_THETA = 500000.0
D_FF = 2816
N_MOD = 6
EPS = 1e-6
NEG_INF = -1e30
LOG2E = 1.4426950408889634
LN2 = 0.6931471805599453
IN_COLS = 2 * D_RNN + 3 * ATT_W + 2 * D_MODEL
RADIUS = 64

LANES = 128
SUBLANES = 8
VMEM_LIMIT_BYTES = 56 * 1024 * 1024

COL_CHUNK = 512
RNN_CG = 128
RNN_T = 256
RNN_TB = 128
RNN_SEG_EXTRA = 4
ATT_BQ = 128
ATT_KW = ATT_BQ + 2 * RADIUS
ATT_UNROLL = 8
ATT_AHEAD = 2
FF_CHUNK = 256

def _resident(shape):
    nd = len(shape)
    return pl.BlockSpec(shape, lambda *_: (0,) * nd, pipeline_mode=pl.Buffered(1))


def _params(*sem):
    return pltpu.CompilerParams(dimension_semantics=sem, vmem_limit_bytes=VMEM_LIMIT_BYTES)


def _rms_mod(x, g, sc, sh):
    y = x * lax.rsqrt(jnp.mean(x * x, axis=-1, keepdims=True) + EPS)
    return (y * g) * (1.0 + sc) + sh


def _sigmoid(x):
    return 1.0 / (1.0 + jnp.exp2(x * -LOG2E))


def _gelu_tanh(x):
    return 0.5 * x * (1.0 + jnp.tanh(0.7978845608028654 * (x + 0.044715 * (x * x * x))))


def _mod_kernel(c_ref, w_ref, b_ref, o_ref):
    c = c_ref[...]
    s = (c * _sigmoid(c)).astype(BF16)
    o_ref[...] = jnp.dot(s, w_ref[...], preferred_element_type=F32) + b_ref[...]


def _modulation(c, w_ada, b_ada):
    B = c.shape[0]
    rows = -(-B // 16) * 16
    cp = jnp.pad(c, ((0, rows - B), (0, 0)))
    out = pl.pallas_call(
        _mod_kernel,
        out_shape=jax.ShapeDtypeStruct((rows, N_MOD * D_MODEL), F32),
        grid=(N_MOD,),
        in_specs=[
            pl.BlockSpec((rows, D_MODEL), lambda j: (0, 0)),
            pl.BlockSpec((D_MODEL, D_MODEL), lambda j: (0, j)),
            pl.BlockSpec((1, D_MODEL), lambda j: (0, j)),
        ],
        out_specs=pl.BlockSpec((rows, D_MODEL), lambda j: (0, j)),
        compiler_params=_params("parallel"),
        name="mod",
    )(cp, w_ada, b_ada.reshape(1, -1))
    return out[:B].reshape(B, N_MOD, 1, D_MODEL)


def _fold_pitch(dil):
    return dil if dil % (2 * SUBLANES) else dil + SUBLANES


def _inproj_kernel(x_ref, sc_ref, sh_ref, g_ref, w_ref, cos_ref, sa_ref, sb_ref,
                   z_ref, q0_ref, q1_ref, q2_ref, fold_ref, *, tm):
    h = _rms_mod(x_ref[0], g_ref[...], sc_ref[0], sh_ref[0]).astype(BF16)

    def proj(j):
        return jnp.dot(h, w_ref[:, j * COL_CHUNK:(j + 1) * COL_CHUNK], preferred_element_type=F32)

    def rope(v):
        parts = []
        for s in range(COL_CHUNK // LANES):
            p = v[:, s * LANES:(s + 1) * LANES]
            up = pltpu.roll(p, LANES - ROT_DIM // 2, axis=1)
            dn = pltpu.roll(p, ROT_DIM // 2, axis=1)
            parts.append(p * cos_ref[...] + up * sa_ref[...] + dn * sb_ref[...])
        return jnp.concatenate(parts, axis=1)

    qkv_refs = (q0_ref, q1_ref, q2_ref)

    def emit_folded(val, g, off):
        dil = ATT_GROUPS[g][1]
        ref = qkv_refs[g]
        if dil == 1:
            ref[0, 0, :, off:off + COL_CHUNK] = val.astype(BF16)
            return
        pitch = _fold_pitch(dil)
        for s in range(COL_CHUNK // LANES):
            slab = val[:, s * LANES:(s + 1) * LANES]
            if pitch == dil:
                fold_ref[s, 0:tm, :] = slab
            else:
                for m in range(tm // dil):
                    fold_ref[s, pitch * m:pitch * m + dil, :] = slab[dil * m:dil * (m + 1)]
        for r in range(dil):
            for s in range(COL_CHUNK // LANES):
                lo = off + s * LANES
                ref[0, r, :, lo:lo + LANES] = fold_ref[s, pl.ds(r, tm // dil, stride=pitch), :].astype(BF16)

    def emit_z(j_w, j_z, fn):
        z_ref[0, :, j_z * COL_CHUNK:(j_z + 1) * COL_CHUNK] = fn(proj(j_w)).astype(BF16)

    n_x = D_RNN // COL_CHUNK
    n_rnn = 2 * n_x
    n_gate0 = n_rnn + 3 * N_GROUPS
    for j in range(2 * D_MODEL // COL_CHUNK):
        emit_z(n_gate0 + j, n_rnn + j, lambda v: v)
        if j < n_x:
            emit_z(n_x + j, n_x + j, _gelu_tanh)
    for g in range(N_GROUPS):
        emit_folded(rope(proj(n_rnn + g)) * (HEAD_DIM ** -0.5 * LOG2E), g, 0)
        emit_folded(rope(proj(n_rnn + N_GROUPS + g)), g, GROUP_W)
        emit_folded(proj(n_rnn + 2 * N_GROUPS + g), g, 2 * GROUP_W)
    for j in range(n_x):
        emit_z(j, j, lambda v: v)


def _rope_tables(S):
    half = ROT_DIM // 2
    inv = ROPE_THETA ** (-(jnp.arange(0, ROT_DIM, 2, dtype=F32) / ROT_DIM))
    ang = jnp.arange(S, dtype=F32)[:, None] * inv[None, :]
    cos, sin = jnp.cos(ang), jnp.sin(ang)
    zeros = jnp.zeros((S, HEAD_DIM - ROT_DIM), F32)
    z8 = jnp.zeros((S, half), F32)
    c = jnp.concatenate([cos, cos, zeros + 1.0], axis=1)
    sa = jnp.concatenate([-sin, z8, zeros], axis=1)
    sb = jnp.concatenate([z8, sin, zeros], axis=1)
    rep = LANES // HEAD_DIM
    return tuple(jnp.tile(t, (1, rep)) for t in (c, sa, sb))


def _inproj(x, sc, sh, g, w_in, tm):
    B, S, D = x.shape
    cos, sa, sb = _rope_tables(S)
    dils = [d for _, d in ATT_GROUPS]
    row = lambda b, i: (b, 0, 0)
    out_shape = [jax.ShapeDtypeStruct((B, S, 2 * D_RNN + 2 * D_MODEL), BF16)]
    out_specs = [pl.BlockSpec((1, tm, 2 * D_RNN + 2 * D_MODEL), lambda b, i: (b, i, 0))]
    for d in dils:
        out_shape.append(jax.ShapeDtypeStruct((B, d, S // d, 3 * GROUP_W), BF16))
        out_specs.append(pl.BlockSpec((1, d, tm // d, 3 * GROUP_W), lambda b, i: (b, 0, i, 0)))
    tab = pl.BlockSpec((tm, LANES), lambda b, i: (i, 0))
    return pl.pallas_call(
        functools.partial(_inproj_kernel, tm=tm),
        out_shape=out_shape,
        grid=(B, S // tm),
        in_specs=[
            pl.BlockSpec((1, tm, D), lambda b, i: (b, i, 0)),
            pl.BlockSpec((1, 1, D), row),
            pl.BlockSpec((1, 1, D), row),
            _resident((1, D)),
            _resident((D, IN_COLS)),
            tab, tab, tab,
        ],
        out_specs=out_specs,
        scratch_shapes=[pltpu.VMEM((COL_CHUNK // LANES, max(_fold_pitch(d) * (tm // d) for d in dils), LANES), F32)],
        compiler_params=_params("parallel", "parallel"),
        name="inproj",
    )(x, sc, sh, g, w_in, cos, sa, sb)


def _rnn_seg_len(S):
    assert S % (8 * SUBLANES) == 0
    return S // SUBLANES + RNN_SEG_EXTRA


def _rglru_kernel(x_ref, gate_ref, cw_ref, cb_ref, wg_ref, bg_ref, lam_ref, o_ref,
                  xnat_ref, hnat_ref, hloc_ref, acum_ref, *, S):
    C = RNN_CG
    T = RNN_T
    TB = RNN_TB
    PAD = SUBLANES
    seg = _rnn_seg_len(S)
    n_main = (seg - RNN_SEG_EXTRA) // TB
    rows = xnat_ref.shape[0]

    xnat_ref[0:PAD, :] = jnp.zeros((PAD, C), F32)
    xnat_ref[PAD + S:rows, :] = jnp.zeros((rows - PAD - S, C), F32)

    def fill(c, carry):
        r0 = pl.multiple_of(c * T, T)
        xnat_ref[pl.ds(r0 + PAD, T), :] = x_ref[0, pl.ds(r0, T), :].astype(F32)
        return carry

    lax.fori_loop(0, S // T, fill, 0)

    neg_lam = -lam_ref[0]
    softplus = jnp.maximum(neg_lam, 0.0) + jnp.log1p(jnp.exp(-jnp.abs(neg_lam)))
    coef2 = (-RG_C * LOG2E) * softplus
    cwb = [jnp.broadcast_to(cw_ref[k:k + 1, :], (SUBLANES, C)) for k in range(CONV_W)]
    cbb = jnp.broadcast_to(cb_ref[...], (SUBLANES, C))
    sub = lax.broadcasted_iota(jnp.int32, (SUBLANES, C), 0)
    steps_left = S - sub * seg

    def step_rows(t):
        return (pl.ds(PAD + t, SUBLANES, stride=seg), slice(None))

    def block_gates(t0, n, direction):
        taps = [xnat_ref[step_rows(t0 + m - CONV_LEFT)] for m in range(n + CONV_W - 1)]
        xcs = []
        for j in range(n):
            acc = cbb + taps[j] * cwb[0]
            for k in range(1, CONV_W):
                acc = acc + taps[j + k] * cwb[k]
            xcs.append(acc)
        xc = jnp.concatenate(xcs, axis=0)
        lo = direction * 2 * C
        gz = jnp.dot(xc.astype(BF16), wg_ref[0, :, lo:lo + 2 * C], preferred_element_type=F32)
        gz = gz + bg_ref[0, :, lo:lo + 2 * C]
        r = _sigmoid(gz[:, :C])
        i = _sigmoid(gz[:, C:])
        a = jnp.exp2(coef2[direction:direction + 1] * r)
        y = 1.0 - a * a
        root = jnp.where(y > 0.0, y * lax.rsqrt(y), 0.0)
        u = root * (i * xc)
        return a, u

    def scan_block(t0, n, direction, carry):
        h, acc = carry
        a, u = block_gates(t0, n, direction)
        order = range(n) if direction == 0 else range(n - 1, -1, -1)
        for j in order:
            aj = a[j * SUBLANES:(j + 1) * SUBLANES]
            uj = u[j * SUBLANES:(j + 1) * SUBLANES]
            if direction == 1:
                uj = jnp.where(t0 + j < steps_left, uj, 0.0)
            h = aj * h + uj
            acc = aj * acc
            row = pl.multiple_of((t0 + j) * SUBLANES, SUBLANES)
            hloc_ref[direction, pl.ds(row, SUBLANES), :] = h
            acum_ref[direction, pl.ds(row, SUBLANES), :] = acc
        return h, acc

    def entry_states(h_tot, a_tot, direction):
        c = jnp.zeros((1, C), F32)
        out = jnp.zeros((SUBLANES, C), F32)
        order = range(SUBLANES) if direction == 0 else range(SUBLANES - 1, -1, -1)
        for s in order:
            out = jnp.where(sub == s, c, out)
            c = a_tot[s:s + 1] * c + h_tot[s:s + 1]
        return out

    def fix_block(t0, n, entries):
        for j in range(n):
            row = pl.multiple_of((t0 + j) * SUBLANES, SUBLANES)
            h = None
            for direction in (0, 1):
                part = (hloc_ref[direction, pl.ds(row, SUBLANES), :]
                        + acum_ref[direction, pl.ds(row, SUBLANES), :] * entries[direction])
                h = part if h is None else h + part
            hnat_ref[step_rows(t0 + j)] = h

    tail0 = n_main * TB
    init = (jnp.zeros((SUBLANES, C), F32), jnp.ones((SUBLANES, C), F32))

    def main(b, carry):
        fwd = scan_block(pl.multiple_of(b * TB, TB), TB, 0, carry[0])
        bwd = scan_block(pl.multiple_of((n_main - 1 - b) * TB, TB), TB, 1, carry[1])
        return fwd, bwd

    bwd_tail = scan_block(tail0, RNN_SEG_EXTRA, 1, init)
    fwd_tot, bwd_tot = lax.fori_loop(0, n_main, main, (init, bwd_tail))
    fwd_tot = scan_block(tail0, RNN_SEG_EXTRA, 0, fwd_tot)
    entries = (entry_states(*fwd_tot, 0), entry_states(*bwd_tot, 1))

    def fix(b, carry):
        fix_block(pl.multiple_of(b * TB, TB), TB, entries)
        return carry

    lax.fori_loop(0, n_main, fix, 0)
    fix_block(tail0, RNN_SEG_EXTRA, entries)

    def finish(c, carry):
        r0 = pl.multiple_of(c * T, T)
        gate = gate_ref[0, pl.ds(r0, T), :].astype(F32)
        o_ref[0, pl.ds(r0, T), :] = (hnat_ref[pl.ds(r0 + PAD, T), :] * gate).astype(BF16)
        return carry

    lax.fori_loop(0, S // T, finish, 0)


def _block_diag_pairs(w):
    per = RNN_CG // RG_BW
    w = w.reshape(RG_BLOCKS // per, per, RG_BW, RG_BW)
    rows = []
    for p in range(per):
        cols = [w[:, p] if q == p else jnp.zeros_like(w[:, p]) for q in range(per)]
        rows.append(jnp.concatenate(cols, axis=-1))
    return jnp.concatenate(rows, axis=1)


def _rglru(z, conv_w, conv_b, rg_wa, rg_ba, rg_wx, rg_bx, rg_lambda):
    B, S, _ = z.shape
    C = RNN_CG
    n_grp = D_RNN // C
    wg = jnp.concatenate([_block_diag_pairs(rg_wa[0]), _block_diag_pairs(rg_wx[0]),
                          _block_diag_pairs(rg_wa[1]), _block_diag_pairs(rg_wx[1])], axis=-1).astype(BF16)
    bg = jnp.concatenate([rg_ba[0].reshape(n_grp, 1, C), rg_bx[0].reshape(n_grp, 1, C),
                          rg_ba[1].reshape(n_grp, 1, C), rg_bx[1].reshape(n_grp, 1, C)], axis=-1)
    lam = rg_lambda.reshape(2, n_grp, C).transpose(1, 0, 2)
    step_rows = SUBLANES * _rnn_seg_len(S)
    nat_rows = step_rows + 2 * SUBLANES
    return pl.pallas_call(
        functools.partial(_rglru_kernel, S=S),
        out_shape=jax.ShapeDtypeStruct((B, S, D_RNN), BF16),
        grid=(B, n_grp),
        in_specs=[
            pl.BlockSpec((1, S, C), lambda b, c: (b, 0, c)),
            pl.BlockSpec((1, S, C), lambda b, c: (b, 0, n_grp + c)),
            pl.BlockSpec((CONV_W, C), lambda b, c: (0, c)),
            pl.BlockSpec((1, C), lambda b, c: (0, c)),
            pl.BlockSpec((1, C, 4 * C), lambda b, c: (c, 0, 0)),
            pl.BlockSpec((1, 1, 4 * C), lambda b, c: (c, 0, 0)),
            pl.BlockSpec((1, 2, C), lambda b, c: (c, 0, 0)),
        ],
        out_specs=pl.BlockSpec((1, S, C), lambda b, c: (b, 0, c)),
        scratch_shapes=[pltpu.VMEM((nat_rows, C), F32), pltpu.VMEM((nat_rows, C), F32),
                        pltpu.VMEM((2, step_rows, C), F32), pltpu.VMEM((2, step_rows, C), F32)],
        compiler_params=_params("parallel", "parallel"),
        name="rglru",
    )(z, z, conv_w, conv_b.reshape(1, -1), wg, bg, lam)


def _attn_kernel(q_ref, k_ref, v_ref, o_ref, l_ref, cap_ref, *, L, n_res):
    BQ, KW = ATT_BQ, ATT_KW
    lane = lax.broadcasted_iota(jnp.int32, (BQ, LANES), 1)
    first = lane < HEAD_DIM
    first_kw = lax.broadcasted_iota(jnp.int32, (KW, LANES), 1) < HEAD_DIM
    rel = lax.broadcasted_iota(jnp.int32, (BQ, KW), 0) - lax.broadcasted_iota(jnp.int32, (BQ, KW), 1)
    n_blocks = L // BQ
    total = n_res * n_blocks
    assert n_blocks & (n_blocks - 1) == 0 and total % ATT_UNROLL == 0

    for j in range(3):
        cap = jnp.where(jnp.abs(rel + j * RADIUS) <= RADIUS, jnp.inf, NEG_INF).astype(F32)
        cap_ref[j, 0:BQ, :] = cap
        cap_ref[j, BQ:2 * BQ, :] = cap

    def scores(idx):
        r = lax.shift_right_logical(idx, n_blocks.bit_length() - 1)
        q0 = pl.multiple_of(jnp.bitwise_and(idx, n_blocks - 1) * BQ, BQ)
        k0 = pl.multiple_of(jnp.clip(q0 - RADIUS, 0, L - KW), RADIUS)
        q = q_ref[0, r, pl.ds(q0, BQ), :]
        k = k_ref[0, r, pl.ds(k0, KW), :]
        zero = jnp.zeros_like(q)
        q2 = jnp.concatenate([jnp.where(first, q, zero), jnp.where(first, zero, q)], axis=0)
        s = lax.dot_general(q2, k, (((1,), (1,)), ((), ())), preferred_element_type=F32)
        return r, q0, k0, s

    def finish(r, q0, k0, s):
        cap = cap_ref[lax.shift_right_logical(q0 - k0, RADIUS.bit_length() - 1)]
        v = v_ref[0, r, pl.ds(k0, KW), :]
        s = jnp.minimum(s, cap)
        m = jnp.max(s, axis=-1, keepdims=True)
        p = jnp.exp2(s - m).astype(BF16)
        one = jnp.ones_like(v)
        pv_a = jnp.dot(p[:BQ], jnp.where(first_kw, v, one), preferred_element_type=F32)
        pv_b = jnp.dot(p[BQ:], jnp.where(first_kw, one, v), preferred_element_type=F32)
        num = jnp.where(first, pv_a, pv_b)
        den = pltpu.roll(jnp.where(first, pv_b, pv_a), HEAD_DIM, axis=1)
        top = jnp.where(first, m[:BQ], m[BQ:])
        o_ref[0, r, pl.ds(q0, BQ), :] = (num / den).astype(BF16)
        l_ref[0, r, pl.ds(q0, BQ), :] = top * LN2 + jnp.log(den)

    def body(it, carry):
        pending = [scores(it * ATT_UNROLL + j) for j in range(ATT_AHEAD)]
        for j in range(ATT_UNROLL):
            if j + ATT_AHEAD < ATT_UNROLL:
                pending.append(scores(it * ATT_UNROLL + j + ATT_AHEAD))
            finish(*pending.pop(0))
        return carry

    lax.fori_loop(0, total // ATT_UNROLL, body, 0)


def _attention(qkv):
    B, dil, L, _ = qkv.shape
    n_pair = GROUP_W // LANES
    spec = lambda off: pl.BlockSpec((1, dil, L, LANES), lambda b, p: (b, 0, 0, off + p))
    return pl.pallas_call(
        functools.partial(_attn_kernel, L=L, n_res=dil),
        out_shape=[jax.ShapeDtypeStruct((B, dil, L, GROUP_W), BF16),
                   jax.ShapeDtypeStruct((B, dil, L, GROUP_W), F32)],
        grid=(B, n_pair),
        in_specs=[spec(0), spec(n_pair), spec(2 * n_pair)],
        out_specs=[spec(0), spec(0)],
        scratch_shapes=[pltpu.VMEM((3, 2 * ATT_BQ, ATT_KW), F32)],
        compiler_params=_params("parallel", "parallel"),
        name="attn",
    )(qkv, qkv, qkv)


def _mixffn_kernel(x_ref, gt1_ref, sc_ref, sh_ref, gt2_ref, g2_ref, gf_ref, mg_ref, rnn_ref,
                   o0_ref, l0_ref, o1_ref, l1_ref, o2_ref, l2_ref,
                   wr_ref, wa_ref, wo_ref, wi_ref, wf_ref, y_ref,
                   fo1_ref, fl1_ref, fo2_ref, fl2_ref, *, tm):
    n_tiles = GROUP_W // LANES

    def unfold(src_ref, dst_ref, dil):
        pitch = _fold_pitch(dil)
        for r in range(dil):
            for s in range(n_tiles):
                dst_ref[s, pl.ds(r, tm // dil, stride=pitch), :] = src_ref[0, r, :, s * LANES:(s + 1) * LANES].astype(F32)
        if pitch == dil:
            return jnp.concatenate([dst_ref[s, 0:tm, :] for s in range(n_tiles)], axis=1)
        return jnp.concatenate(
            [jnp.concatenate([dst_ref[s, pitch * m:pitch * m + dil, :] for m in range(tm // dil)], axis=0)
             for s in range(n_tiles)], axis=1)

    o1 = unfold(o1_ref, fo1_ref, ATT_GROUPS[1][1])
    l1 = unfold(l1_ref, fl1_ref, ATT_GROUPS[1][1])
    o2 = unfold(o2_ref, fo2_ref, ATT_GROUPS[2][1])
    l2 = unfold(l2_ref, fl2_ref, ATT_GROUPS[2][1])
    lses = (l0_ref[0, 0], l1, l2)
    outs = (o0_ref[0, 0].astype(F32), o1, o2)
    top = jnp.maximum(jnp.maximum(lses[0], lses[1]), lses[2])
    es = [jnp.exp(l - top) for l in lses]
    att = (es[0] * outs[0] + es[1] * outs[1] + es[2] * outs[2]) / (es[0] + es[1] + es[2])

    gate = _sigmoid(mg_ref[0].astype(F32))
    br_r = jnp.dot(rnn_ref[0], wr_ref[...], preferred_element_type=F32)
    br_a = jnp.dot(att.astype(BF16), wa_ref[...], preferred_element_type=F32)
    merged = gate[:, :D_MODEL] * br_r + gate[:, D_MODEL:] * br_a
    mix = jnp.dot(merged.astype(BF16), wo_ref[...], preferred_element_type=F32)
    x1 = x_ref[0] + gt1_ref[0] * mix

    h = _rms_mod(x1, g2_ref[...], sc_ref[0], sh_ref[0]).astype(BF16)
    ff = None
    for c in range(D_FF // FF_CHUNK):
        lo = c * FF_CHUNK
        fg = jnp.dot(h, wi_ref[:, lo:lo + FF_CHUNK], preferred_element_type=F32)
        fu = jnp.dot(h, wi_ref[:, D_FF + lo:D_FF + lo + FF_CHUNK], preferred_element_type=F32)
        act = ((fg * _sigmoid(fg)) * fu).astype(BF16)
        part = jnp.dot(act, wf_ref[lo:lo + FF_CHUNK, :], preferred_element_type=F32)
        ff = part if ff is None else ff + part
    x2 = x1 + gt2_ref[0] * ff
    y = x2 * lax.rsqrt(jnp.mean(x2 * x2, axis=-1, keepdims=True) + EPS)
    y_ref[0] = y * gf_ref[...]


def _mixffn(x, mods, g2, gf, z, rnn, attn_outs, weights, tm):
    B, S, D = x.shape
    mod_spec = pl.BlockSpec((1, 1, D), lambda b, i: (b, 0, 0))
    in_specs = [pl.BlockSpec((1, tm, D), lambda b, i: (b, i, 0))] + [mod_spec] * len(mods)
    in_specs += [
        _resident((1, D)),
        _resident((1, D)),
        pl.BlockSpec((1, tm, 2 * D_MODEL), lambda b, i: (b, i, 1)),
        pl.BlockSpec((1, tm, D_RNN), lambda b, i: (b, i, 0)),
    ]
    args = [x, *mods, g2, gf, z, rnn]
    for (o, l), (_, d) in zip(attn_outs, ATT_GROUPS):
        blk = pl.BlockSpec((1, d, tm // d, GROUP_W), lambda b, i: (b, 0, i, 0))
        in_specs += [blk, blk]
        args += [o, l]
    in_specs += [_resident(w.shape) for w in weights]
    args += list(weights)
    return pl.pallas_call(
        functools.partial(_mixffn_kernel, tm=tm),
        out_shape=jax.ShapeDtypeStruct((B, S, D), F32),
        grid=(B, S // tm),
        in_specs=in_specs,
        out_specs=pl.BlockSpec((1, tm, D), lambda b, i: (b, i, 0)),
        scratch_shapes=[pltpu.VMEM((GROUP_W // LANES, _fold_pitch(d) * (tm // d), LANES), F32)
                        for _, d in ATT_GROUPS[1:] for _ in range(2)],
        compiler_params=_params("parallel", "parallel"),
        name="mixffn",
    )(*args)


def _encode(x, c, w, tm_in, tm_mix):
    mod = _modulation(c, w["w_ada"], w["b_ada"])
    sh1, sc1, gt1, sh2, sc2, gt2 = (mod[:, k] for k in range(N_MOD))
    outs = _inproj(x, sc1, sh1, w["norm1_g"], w["w_in"], tm_in)
    z, qkvs = outs[0], outs[1:]
    rnn = _rglru(z, w["conv_w"], w["conv_b"], w["rg_wa"], w["rg_ba"], w["rg_wx"], w["rg_bx"], w["rg_lambda"])
    attn_outs = [_attention(qkv) for qkv in qkvs]
    weights = (w["w_br_rnn"], w["w_br_attn"], w["w_out"], w["w_ffn_in"], w["w_ffn_out"])
    return _mixffn(x, (gt1, sc2, sh2, gt2), w["norm2_g"], w["final_g"], z, rnn, attn_outs, weights, tm_mix)


def kernel(x_prompt, x_sample, c_prompt, c_sample, w_ada, b_ada, norm1_g, w_in, conv_w, conv_b, rg_wa, rg_ba, rg_wx, rg_bx, rg_lambda, w_br_rnn, w_br_attn, w_out, norm2_g, w_ffn_in, w_ffn_out, final_g):
    assert w_ada.shape[0] == 1, "single layer"
    w = dict(
        w_ada=w_ada[0].astype(BF16), b_ada=b_ada[0], norm1_g=norm1_g[0].reshape(1, -1),
        w_in=w_in[0].astype(BF16), conv_w=conv_w[0], conv_b=conv_b[0],
        rg_wa=rg_wa[0], rg_ba=rg_ba[0], rg_wx=rg_wx[0], rg_bx=rg_bx[0], rg_lambda=rg_lambda[0],
        w_br_rnn=w_br_rnn[0].astype(BF16), w_br_attn=w_br_attn[0].astype(BF16), w_out=w_out[0].astype(BF16),
        norm2_g=norm2_g[0].reshape(1, -1), w_ffn_in=w_ffn_in[0].astype(BF16),
        w_ffn_out=w_ffn_out[0].astype(BF16), final_g=final_g.reshape(1, -1),
    )
    tiles = dict(tm_in=512, tm_mix=512)
    return (_encode(x_prompt, c_prompt, w, **tiles), _encode(x_sample, c_sample, w, **tiles))
```

```python
import functools

import jax
import jax.numpy as jnp
from jax import lax
from jax.experimental import pallas as pl
from jax.experimental.pallas import tpu as pltpu

F32 = jnp.float32
BF16 = jnp.bfloat16

D_MODEL = 1024
D_RNN = 1024
RG_BLOCKS = 16
RG_BW = D_RNN // RG_BLOCKS
RG_C = 8.0
CONV_W = 4
CONV_LEFT = 2
HEAD_DIM = 64
HEADS_PER_GROUP = 8
ATT_GROUPS = ((128, 1), (512, 4), (2048, 16))
N_GROUPS = len(ATT_GROUPS)
GROUP_W = HEADS_PER_GROUP * HEAD_DIM
ATT_W = N_GROUPS * GROUP_W
ROT_DIM = HEAD_DIM // 4
ROPE_THETA = 500000.0
D_FF = 2816
N_MOD = 6
EPS = 1e-6
NEG_INF = -1e30
LOG2E = 1.4426950408889634
LN2 = 0.6931471805599453
IN_COLS = 2 * D_RNN + 3 * ATT_W + 2 * D_MODEL
RADIUS = 64

LANES = 128
SUBLANES = 8
VMEM_LIMIT_BYTES = 56 * 1024 * 1024

COL_CHUNK = 512
RNN_CG = 128
RNN_T = 256
RNN_TB = 128
RNN_SEG_EXTRA = 4
ATT_BQ = 128
ATT_KW = ATT_BQ + 2 * RADIUS
ATT_UNROLL = 8
ATT_AHEAD = 2
FF_CHUNK = 256


def _resident(shape):
    nd = len(shape)
    return pl.BlockSpec(shape, lambda *_: (0,) * nd, pipeline_mode=pl.Buffered(1))


def _params(*sem):
    return pltpu.CompilerParams(dimension_semantics=sem, vmem_limit_bytes=VMEM_LIMIT_BYTES)


def _rms_mod(x, g, sc, sh):
    y = x * lax.rsqrt(jnp.mean(x * x, axis=-1, keepdims=True) + EPS)
    return y * (g * (1.0 + sc)) + sh


def _sigmoid(x):
    return 1.0 / (1.0 + jnp.exp2(x * -LOG2E))


def _gelu_tanh(x):
    return 0.5 * x * (1.0 + jnp.tanh(0.7978845608028654 * (x + 0.044715 * (x * x * x))))


def _mod_kernel(c_ref, w_ref, b_ref, o_ref):
    c = c_ref[...]
    s = (c * _sigmoid(c)).astype(BF16)
    o_ref[...] = jnp.dot(s, w_ref[...].astype(BF16), preferred_element_type=F32) + b_ref[...]


def _modulation(c, w_ada, b_ada):
    B = c.shape[0]
    rows = -(-B // 16) * 16
    cp = jnp.pad(c, ((0, rows - B), (0, 0)))
    out = pl.pallas_call(
        _mod_kernel,
        out_shape=jax.ShapeDtypeStruct((rows, N_MOD * D_MODEL), F32),
        grid=(N_MOD,),
        in_specs=[
            pl.BlockSpec((rows, D_MODEL), lambda j: (0, 0)),
            pl.BlockSpec((D_MODEL, D_MODEL), lambda j: (0, j)),
            pl.BlockSpec((1, D_MODEL), lambda j: (0, j)),
        ],
        out_specs=pl.BlockSpec((rows, D_MODEL), lambda j: (0, j)),
        compiler_params=_params("parallel"),
        name="mod",
    )(cp, w_ada, b_ada.reshape(1, -1))
    return out[:B].reshape(B, N_MOD, 1, D_MODEL)


def _fold_pitch(dil):
    return dil if dil % (2 * SUBLANES) else dil + SUBLANES


def _inproj_kernel(x_ref, sc_ref, sh_ref, g_ref, w_ref, cos_ref, sa_ref, sb_ref,
                   z_ref, q0_ref, q1_ref, q2_ref, fold_ref, *, tm):
    h = _rms_mod(x_ref[0], g_ref[...], sc_ref[0], sh_ref[0]).astype(BF16)

    def proj(j):
        return jnp.dot(h, w_ref[:, j * COL_CHUNK:(j + 1) * COL_CHUNK], preferred_element_type=F32)

    def rope(v):
        parts = []
        for s in range(COL_CHUNK // LANES):
            p = v[:, s * LANES:(s + 1) * LANES]
            up = pltpu.roll(p, LANES - ROT_DIM // 2, axis=1)
            dn = pltpu.roll(p, ROT_DIM // 2, axis=1)
            parts.append(p * cos_ref[...] + up * sa_ref[...] + dn * sb_ref[...])
        return jnp.concatenate(parts, axis=1)

    qkv_refs = (q0_ref, q1_ref, q2_ref)

    def emit_folded(val, g, off):
        dil = ATT_GROUPS[g][1]
        ref = qkv_refs[g]
        if dil == 1:
            ref[0, 0, :, off:off + COL_CHUNK] = val.astype(BF16)
            return
        pitch = _fold_pitch(dil)
        for s in range(COL_CHUNK // LANES):
            slab = val[:, s * LANES:(s + 1) * LANES]
            if pitch == dil:
                fold_ref[s, 0:tm, :] = slab
            else:
                for m in range(tm // dil):
                    fold_ref[s, pitch * m:pitch * m + dil, :] = slab[dil * m:dil * (m + 1)]
        for r in range(dil):
            for s in range(COL_CHUNK // LANES):
                lo = off + s * LANES
                ref[0, r, :, lo:lo + LANES] = fold_ref[s, pl.ds(r, tm // dil, stride=pitch), :].astype(BF16)

    def emit_z(j_w, j_z, fn):
        z_ref[0, :, j_z * COL_CHUNK:(j_z + 1) * COL_CHUNK] = fn(proj(j_w)).astype(BF16)

    n_x = D_RNN // COL_CHUNK
    n_rnn = 2 * n_x
    n_gate0 = n_rnn + 3 * N_GROUPS
    for j in range(2 * D_MODEL // COL_CHUNK):
        emit_z(n_gate0 + j, n_rnn + j, lambda v: v)
        if j < n_x:
            emit_z(n_x + j, n_x + j, _gelu_tanh)
    for g in range(N_GROUPS):
        emit_folded(rope(proj(n_rnn + g)) * (HEAD_DIM ** -0.5 * LOG2E), g, 0)
        emit_folded(rope(proj(n_rnn + N_GROUPS + g)), g, GROUP_W)
        emit_folded(proj(n_rnn + 2 * N_GROUPS + g), g, 2 * GROUP_W)
    for j in range(n_x):
        emit_z(j, j, lambda v: v)


def _rope_tables(S):
    half = ROT_DIM // 2
    inv = ROPE_THETA ** (-(jnp.arange(0, ROT_DIM, 2, dtype=F32) / ROT_DIM))
    ang = jnp.arange(S, dtype=F32)[:, None] * inv[None, :]
    cos, sin = jnp.cos(ang), jnp.sin(ang)
    zeros = jnp.zeros((S, HEAD_DIM - ROT_DIM), F32)
    z8 = jnp.zeros((S, half), F32)
    c = jnp.concatenate([cos, cos, zeros + 1.0], axis=1)
    sa = jnp.concatenate([-sin, z8, zeros], axis=1)
    sb = jnp.concatenate([z8, sin, zeros], axis=1)
    rep = LANES // HEAD_DIM
    return tuple(jnp.tile(t, (1, rep)) for t in (c, sa, sb))


def _inproj(x, sc, sh, g, w_in, tm):
    B, S, D = x.shape
    cos, sa, sb = _rope_tables(S)
    dils = [d for _, d in ATT_GROUPS]
    row = lambda b, i: (b, 0, 0)
    out_shape = [jax.ShapeDtypeStruct((B, S, 2 * D_RNN + 2 * D_MODEL), BF16)]
    out_specs = [pl.BlockSpec((1, tm, 2 * D_RNN + 2 * D_MODEL), lambda b, i: (b, i, 0))]
    for d in dils:
        out_shape.append(jax.ShapeDtypeStruct((B, d, S // d, 3 * GROUP_W), BF16))
        out_specs.append(pl.BlockSpec((1, d, tm // d, 3 * GROUP_W), lambda b, i: (b, 0, i, 0)))
    tab = pl.BlockSpec((tm, LANES), lambda b, i: (i, 0))
    return pl.pallas_call(
        functools.partial(_inproj_kernel, tm=tm),
        out_shape=out_shape,
        grid=(B, S // tm),
        in_specs=[
            pl.BlockSpec((1, tm, D), lambda b, i: (b, i, 0)),
            pl.BlockSpec((1, 1, D), row),
            pl.BlockSpec((1, 1, D), row),
            _resident((1, D)),
            _resident((D, IN_COLS)),
            tab, tab, tab,
        ],
        out_specs=out_specs,
        scratch_shapes=[pltpu.VMEM((COL_CHUNK // LANES, max(_fold_pitch(d) * (tm // d) for d in dils), LANES), F32)],
        compiler_params=_params("parallel", "parallel"),
        name="inproj",
    )(x, sc, sh, g, w_in, cos, sa, sb)


def _rnn_seg_len(S):
    assert S % (8 * SUBLANES) == 0
    return S // SUBLANES + RNN_SEG_EXTRA


def _rglru_kernel(x_ref, gate_ref, cw_ref, cb_ref, wg_ref, bg_ref, lam_ref, o_ref,
                  xnat_ref, hnat_ref, hloc_ref, acum_ref, *, S):
    C = RNN_CG
    T = RNN_T
    TB = RNN_TB
    PAD = SUBLANES
    seg = _rnn_seg_len(S)
    n_main = (seg - RNN_SEG_EXTRA) // TB
    rows = xnat_ref.shape[0]

    xnat_ref[0:PAD, :] = jnp.zeros((PAD, C), F32)
    xnat_ref[PAD + S:rows, :] = jnp.zeros((rows - PAD - S, C), F32)

    def fill(c, carry):
        r0 = pl.multiple_of(c * T, T)
        xnat_ref[pl.ds(r0 + PAD, T), :] = x_ref[0, pl.ds(r0, T), :].astype(F32)
        return carry

    lax.fori_loop(0, S // T, fill, 0)

    neg_lam = -lam_ref[0]
    softplus = jnp.maximum(neg_lam, 0.0) + jnp.log1p(jnp.exp(-jnp.abs(neg_lam)))
    coef2 = (-RG_C * LOG2E) * softplus
    cwb = [jnp.broadcast_to(cw_ref[k:k + 1, :], (SUBLANES, C)) for k in range(CONV_W)]
    cbb = jnp.broadcast_to(cb_ref[...], (SUBLANES, C))
    sub = lax.broadcasted_iota(jnp.int32, (SUBLANES, C), 0)
    steps_left = S - sub * seg

    def step_rows(t):
        return (pl.ds(PAD + t, SUBLANES, stride=seg), slice(None))

    def block_gates(t0, n, direction):
        taps = [xnat_ref[step_rows(t0 + m - CONV_LEFT)] for m in range(n + CONV_W - 1)]
        xcs = []
        for j in range(n):
            acc = cbb + taps[j] * cwb[0]
            for k in range(1, CONV_W):
                acc = acc + taps[j + k] * cwb[k]
            xcs.append(acc)
        xc = jnp.concatenate(xcs, axis=0)
        lo = direction * 2 * C
        gz = jnp.dot(xc.astype(BF16), wg_ref[0, :, lo:lo + 2 * C], preferred_element_type=F32)
        gz = gz + bg_ref[0, :, lo:lo + 2 * C]
        r = _sigmoid(gz[:, :C])
        i = _sigmoid(gz[:, C:])
        a = jnp.exp2(coef2[direction:direction + 1] * r)
        y = 1.0 - a * a
        root = jnp.where(y > 0.0, y * lax.rsqrt(y), 0.0)
        u = root * (i * xc)
        return a, u

    def scan_block(t0, n, direction, carry):
        h, acc = carry
        a, u = block_gates(t0, n, direction)
        order = range(n) if direction == 0 else range(n - 1, -1, -1)
        for j in order:
            aj = a[j * SUBLANES:(j + 1) * SUBLANES]
            uj = u[j * SUBLANES:(j + 1) * SUBLANES]
            if direction == 1:
                uj = jnp.where(t0 + j < steps_left, uj, 0.0)
            h = aj * h + uj
            acc = aj * acc
            row = pl.multiple_of((t0 + j) * SUBLANES, SUBLANES)
            hloc_ref[direction, pl.ds(row, SUBLANES), :] = h
            acum_ref[direction, pl.ds(row, SUBLANES), :] = acc
        return h, acc

    def entry_states(h_tot, a_tot, direction):
        c = jnp.zeros((1, C), F32)
        out = jnp.zeros((SUBLANES, C), F32)
        order = range(SUBLANES) if direction == 0 else range(SUBLANES - 1, -1, -1)
        for s in order:
            out = jnp.where(sub == s, c, out)
            c = a_tot[s:s + 1] * c + h_tot[s:s + 1]
        return out

    def fix_block(t0, n, entries):
        for j in range(n):
            row = pl.multiple_of((t0 + j) * SUBLANES, SUBLANES)
            h = None
            for direction in (0, 1):
                part = (hloc_ref[direction, pl.ds(row, SUBLANES), :]
                        + acum_ref[direction, pl.ds(row, SUBLANES), :] * entries[direction])
                h = part if h is None else h + part
            hnat_ref[step_rows(t0 + j)] = h

    tail0 = n_main * TB
    init = (jnp.zeros((SUBLANES, C), F32), jnp.ones((SUBLANES, C), F32))

    def main(b, carry):
        fwd = scan_block(pl.multiple_of(b * TB, TB), TB, 0, carry[0])
        bwd = scan_block(pl.multiple_of((n_main - 1 - b) * TB, TB), TB, 1, carry[1])
        return fwd, bwd

    bwd_tail = scan_block(tail0, RNN_SEG_EXTRA, 1, init)
    fwd_tot, bwd_tot = lax.fori_loop(0, n_main, main, (init, bwd_tail))
    fwd_tot = scan_block(tail0, RNN_SEG_EXTRA, 0, fwd_tot)
    entries = (entry_states(*fwd_tot, 0), entry_states(*bwd_tot, 1))

    def fix(b, carry):
        fix_block(pl.multiple_of(b * TB, TB), TB, entries)
        return carry

    lax.fori_loop(0, n_main, fix, 0)
    fix_block(tail0, RNN_SEG_EXTRA, entries)

    def finish(c, carry):
        r0 = pl.multiple_of(c * T, T)
        gate = gate_ref[0, pl.ds(r0, T), :].astype(F32)
        o_ref[0, pl.ds(r0, T), :] = (hnat_ref[pl.ds(r0 + PAD, T), :] * gate).astype(BF16)
        return carry

    lax.fori_loop(0, S // T, finish, 0)


def _block_diag_pairs(w):
    per = RNN_CG // RG_BW
    w = w.reshape(RG_BLOCKS // per, per, RG_BW, RG_BW)
    rows = []
    for p in range(per):
        cols = [w[:, p] if q == p else jnp.zeros_like(w[:, p]) for q in range(per)]
        rows.append(jnp.concatenate(cols, axis=-1))
    return jnp.concatenate(rows, axis=1)


def _rglru(z, conv_w, conv_b, rg_wa, rg_ba, rg_wx, rg_bx, rg_lambda):
    B, S, _ = z.shape
    C = RNN_CG
    n_grp = D_RNN // C
    wg = jnp.concatenate([_block_diag_pairs(rg_wa[0]), _block_diag_pairs(rg_wx[0]),
                          _block_diag_pairs(rg_wa[1]), _block_diag_pairs(rg_wx[1])], axis=-1).astype(BF16)
    bg = jnp.concatenate([rg_ba[0].reshape(n_grp, 1, C), rg_bx[0].reshape(n_grp, 1, C),
                          rg_ba[1].reshape(n_grp, 1, C), rg_bx[1].reshape(n_grp, 1, C)], axis=-1)
    lam = rg_lambda.reshape(2, n_grp, C).transpose(1, 0, 2)
    step_rows = SUBLANES * _rnn_seg_len(S)
    nat_rows = step_rows + 2 * SUBLANES
    return pl.pallas_call(
        functools.partial(_rglru_kernel, S=S),
        out_shape=jax.ShapeDtypeStruct((B, S, D_RNN), BF16),
        grid=(B, n_grp),
        in_specs=[
            pl.BlockSpec((1, S, C), lambda b, c: (b, 0, c)),
            pl.BlockSpec((1, S, C), lambda b, c: (b, 0, n_grp + c)),
            pl.BlockSpec((CONV_W, C), lambda b, c: (0, c)),
            pl.BlockSpec((1, C), lambda b, c: (0, c)),
            pl.BlockSpec((1, C, 4 * C), lambda b, c: (c, 0, 0)),
            pl.BlockSpec((1, 1, 4 * C), lambda b, c: (c, 0, 0)),
            pl.BlockSpec((1, 2, C), lambda b, c: (c, 0, 0)),
        ],
        out_specs=pl.BlockSpec((1, S, C), lambda b, c: (b, 0, c)),
        scratch_shapes=[pltpu.VMEM((nat_rows, C), F32), pltpu.VMEM((nat_rows, C), F32),
                        pltpu.VMEM((2, step_rows, C), F32), pltpu.VMEM((2, step_rows, C), F32)],
        compiler_params=_params("parallel", "parallel"),
        name="rglru",
    )(z, z, conv_w, conv_b.reshape(1, -1), wg, bg, lam)


def _attn_kernel(q_ref, k_ref, v_ref, o_ref, l_ref, cap_ref, *, L, n_res):
    BQ, KW = ATT_BQ, ATT_KW
    lane = lax.broadcasted_iota(jnp.int32, (BQ, LANES), 1)
    first = lane < HEAD_DIM
    first_kw = lax.broadcasted_iota(jnp.int32, (KW, LANES), 1) < HEAD_DIM
    rel = lax.broadcasted_iota(jnp.int32, (BQ, KW), 0) - lax.broadcasted_iota(jnp.int32, (BQ, KW), 1)
    n_blocks = L // BQ
    total = n_res * n_blocks
    assert n_blocks & (n_blocks - 1) == 0 and total % ATT_UNROLL == 0

    for j in range(3):
        cap = jnp.where(jnp.abs(rel + j * RADIUS) <= RADIUS, jnp.inf, NEG_INF).astype(F32)
        cap_ref[j, 0:BQ, :] = cap
        cap_ref[j, BQ:2 * BQ, :] = cap

    def scores(idx):
        r = lax.shift_right_logical(idx, n_blocks.bit_length() - 1)
        q0 = pl.multiple_of(jnp.bitwise_and(idx, n_blocks - 1) * BQ, BQ)
        k0 = pl.multiple_of(jnp.clip(q0 - RADIUS, 0, L - KW), RADIUS)
        q = q_ref[0, r, pl.ds(q0, BQ), :]
        k = k_ref[0, r, pl.ds(k0, KW), :]
        zero = jnp.zeros_like(q)
        q2 = jnp.concatenate([jnp.where(first, q, zero), jnp.where(first, zero, q)], axis=0)
        s = lax.dot_general(q2, k, (((1,), (1,)), ((), ())), preferred_element_type=F32)
        return r, q0, k0, s

    def finish(r, q0, k0, s):
        cap = cap_ref[lax.shift_right_logical(q0 - k0, RADIUS.bit_length() - 1)]
        v = v_ref[0, r, pl.ds(k0, KW), :]
        s = jnp.minimum(s, cap)
        m = jnp.max(s, axis=-1, keepdims=True)
        p = jnp.exp2(s - m).astype(BF16)
        one = jnp.ones_like(v)
        pv_a = jnp.dot(p[:BQ], jnp.where(first_kw, v, one), preferred_element_type=F32)
        pv_b = jnp.dot(p[BQ:], jnp.where(first_kw, one, v), preferred_element_type=F32)
        num = jnp.where(first, pv_a, pv_b)
        den = pltpu.roll(jnp.where(first, pv_b, pv_a), HEAD_DIM, axis=1)
        top = jnp.where(first, m[:BQ], m[BQ:])
        o_ref[0, r, pl.ds(q0, BQ), :] = (num / den).astype(BF16)
        l_ref[0, r, pl.ds(q0, BQ), :] = top * LN2 + jnp.log(den)

    def body(it, carry):
        pending = [scores(it * ATT_UNROLL + j) for j in range(ATT_AHEAD)]
        for j in range(ATT_UNROLL):
            if j + ATT_AHEAD < ATT_UNROLL:
                pending.append(scores(it * ATT_UNROLL + j + ATT_AHEAD))
            finish(*pending.pop(0))
        return carry

    lax.fori_loop(0, total // ATT_UNROLL, body, 0)


def _attention(qkv):
    B, dil, L, _ = qkv.shape
    n_pair = GROUP_W // LANES
    spec = lambda off: pl.BlockSpec((1, dil, L, LANES), lambda b, p: (b, 0, 0, off + p))
    return pl.pallas_call(
        functools.partial(_attn_kernel, L=L, n_res=dil),
        out_shape=[jax.ShapeDtypeStruct((B, dil, L, GROUP_W), BF16),
                   jax.ShapeDtypeStruct((B, dil, L, GROUP_W), F32)],
        grid=(B, n_pair),
        in_specs=[spec(0), spec(n_pair), spec(2 * n_pair)],
        out_specs=[spec(0), spec(0)],
        scratch_shapes=[pltpu.VMEM((3, 2 * ATT_BQ, ATT_KW), F32)],
        compiler_params=_params("parallel", "parallel"),
        name="attn",
    )(qkv, qkv, qkv)


def _mixffn_kernel(x_ref, gt1_ref, sc_ref, sh_ref, gt2_ref, g2_ref, gf_ref, mg_ref, rnn_ref,
                   o0_ref, l0_ref, o1_ref, l1_ref, o2_ref, l2_ref,
                   wr_ref, wa_ref, wo_ref, wi_ref, wf_ref, y_ref,
                   fo1_ref, fl1_ref, fo2_ref, fl2_ref, *, tm):
    n_tiles = GROUP_W // LANES

    def unfold(src_ref, dst_ref, dil):
        pitch = _fold_pitch(dil)
        for r in range(dil):
            for s in range(n_tiles):
                dst_ref[s, pl.ds(r, tm // dil, stride=pitch), :] = src_ref[0, r, :, s * LANES:(s + 1) * LANES].astype(F32)
        if pitch == dil:
            return jnp.concatenate([dst_ref[s, 0:tm, :] for s in range(n_tiles)], axis=1)
        return jnp.concatenate(
            [jnp.concatenate([dst_ref[s, pitch * m:pitch * m + dil, :] for m in range(tm // dil)], axis=0)
             for s in range(n_tiles)], axis=1)

    o1 = unfold(o1_ref, fo1_ref, ATT_GROUPS[1][1])
    l1 = unfold(l1_ref, fl1_ref, ATT_GROUPS[1][1])
    o2 = unfold(o2_ref, fo2_ref, ATT_GROUPS[2][1])
    l2 = unfold(l2_ref, fl2_ref, ATT_GROUPS[2][1])
    lses = (l0_ref[0, 0], l1, l2)
    outs = (o0_ref[0, 0].astype(F32), o1, o2)

    top = jnp.maximum(jnp.maximum(lses[0], lses[1]), lses[2])
    es = [jnp.exp(l - top) for l in lses]
    att = (es[0] * outs[0] + es[1] * outs[1] + es[2] * outs[2]) / (es[0] + es[1] + es[2])

    gate = _sigmoid(mg_ref[0].astype(F32))
    br_r = jnp.dot(rnn_ref[0], wr_ref[...], preferred_element_type=F32)
    br_a = jnp.dot(att.astype(BF16), wa_ref[...], preferred_element_type=F32)
    merged = gate[:, :D_MODEL] * br_r + gate[:, D_MODEL:] * br_a
    mix = jnp.dot(merged.astype(BF16), wo_ref[...], preferred_element_type=F32)
    x1 = x_ref[0] + gt1_ref[0] * mix

    h = _rms_mod(x1, g2_ref[...], sc_ref[0], sh_ref[0]).astype(BF16)
    ff = None
    for c in range(D_FF // FF_CHUNK):
        lo = c * FF_CHUNK
        fg = jnp.dot(h, wi_ref[:, lo:lo + FF_CHUNK], preferred_element_type=F32)
        fu = jnp.dot(h, wi_ref[:, D_FF + lo:D_FF + lo + FF_CHUNK], preferred_element_type=F32)
        act = ((fg * _sigmoid(fg)) * fu).astype(BF16)
        part = jnp.dot(act, wf_ref[lo:lo + FF_CHUNK, :], preferred_element_type=F32)
        ff = part if ff is None else ff + part
    x2 = x1 + gt2_ref[0] * ff
    y = x2 * lax.rsqrt(jnp.mean(x2 * x2, axis=-1, keepdims=True) + EPS)
    y_ref[0] = y * gf_ref[...]


def _mixffn(x, mods, g2, gf, z, rnn, attn_outs, weights, tm):
    B, S, D = x.shape
    mod_spec = pl.BlockSpec((1, 1, D), lambda b, i: (b, 0, 0))
    in_specs = [pl.BlockSpec((1, tm, D), lambda b, i: (b, i, 0))] + [mod_spec] * len(mods)
    in_specs += [
        _resident((1, D)),
        _resident((1, D)),
        pl.BlockSpec((1, tm, 2 * D_MODEL), lambda b, i: (b, i, 1)),
        pl.BlockSpec((1, tm, D_RNN), lambda b, i: (b, i, 0)),
    ]
    args = [x, *mods, g2, gf, z, rnn]
    for (o, l), (_, d) in zip(attn_outs, ATT_GROUPS):
        blk = pl.BlockSpec((1, d, tm // d, GROUP_W), lambda b, i: (b, 0, i, 0))
        in_specs += [blk, blk]
        args += [o, l]
    in_specs += [_resident(w.shape) for w in weights]
    args += list(weights)
    return pl.pallas_call(
        functools.partial(_mixffn_kernel, tm=tm),
        out_shape=jax.ShapeDtypeStruct((B, S, D), F32),
        grid=(B, S // tm),
        in_specs=in_specs,
        out_specs=pl.BlockSpec((1, tm, D), lambda b, i: (b, i, 0)),
        scratch_shapes=[pltpu.VMEM((GROUP_W // LANES, _fold_pitch(d) * (tm // d), LANES), F32)
                        for _, d in ATT_GROUPS[1:] for _ in range(2)],
        compiler_params=_params("parallel", "parallel"),
        name="mixffn",
    )(*args)


def _encode(x, mod, w, tm_in, tm_mix):
    sh1, sc1, gt1, sh2, sc2, gt2 = (mod[:, k] for k in range(N_MOD))
    outs = _inproj(x, sc1, sh1, w["norm1_g"], w["w_in"], tm_in)
    z, qkvs = outs[0], outs[1:]
    rnn = _rglru(z, w["conv_w"], w["conv_b"], w["rg_wa"], w["rg_ba"], w["rg_wx"], w["rg_bx"], w["rg_lambda"])
    attn_outs = [_attention(qkv) for qkv in qkvs]
    weights = (w["w_br_rnn"], w["w_br_attn"], w["w_out"], w["w_ffn_in"], w["w_ffn_out"])
    return _mixffn(x, (gt1, sc2, sh2, gt2), w["norm2_g"], w["final_g"], z, rnn, attn_outs, weights, tm_mix)


def kernel(x_prompt, x_sample, c_prompt, c_sample, w_ada, b_ada, norm1_g, w_in, conv_w, conv_b, rg_wa, rg_ba, rg_wx, rg_bx, rg_lambda, w_br_rnn, w_br_attn, w_out, norm2_g, w_ffn_in, w_ffn_out, final_g):
    assert w_ada.shape[0] == 1, "single layer"
    w = dict(
        norm1_g=norm1_g[0].reshape(1, -1),
        w_in=w_in[0].astype(BF16), conv_w=conv_w[0], conv_b=conv_b[0],
        rg_wa=rg_wa[0], rg_ba=rg_ba[0], rg_wx=rg_wx[0], rg_bx=rg_bx[0], rg_lambda=rg_lambda[0],
        w_br_rnn=w_br_rnn[0].astype(BF16), w_br_attn=w_br_attn[0].astype(BF16), w_out=w_out[0].astype(BF16),
        norm2_g=norm2_g[0].reshape(1, -1), w_ffn_in=w_ffn_in[0].astype(BF16),
        w_ffn_out=w_ffn_out[0].astype(BF16), final_g=final_g.reshape(1, -1),
    )
    tiles = dict(tm_in=512, tm_mix=512)
    n_prompt = c_prompt.shape[0]
    mod = _modulation(jnp.concatenate([c_prompt, c_sample], axis=0), w_ada[0], b_ada[0])
    return (_encode(x_prompt, mod[:n_prompt], w, **tiles), _encode(x_sample, mod[n_prompt:], w, **tiles))
```

```python
import functools

import jax
import jax.numpy as jnp
from jax import lax
from jax.experimental import pallas as pl
from jax.experimental.pallas import tpu as pltpu

F32 = jnp.float32
BF16 = jnp.bfloat16

D_MODEL = 1024
D_RNN = 1024
RG_BLOCKS = 16
RG_BW = D_RNN // RG_BLOCKS
RG_C = 8.0
CONV_W = 4
CONV_LEFT = 2
HEAD_DIM = 64
HEADS_PER_GROUP = 8
ATT_GROUPS = ((128, 1), (512, 4), (2048, 16))
N_GROUPS = len(ATT_GROUPS)
GROUP_W = HEADS_PER_GROUP * HEAD_DIM
ATT_W = N_GROUPS * GROUP_W
ROT_DIM = HEAD_DIM // 4
ROPE_THETA = 500000.0
D_FF = 2816
N_MOD = 6
EPS = 1e-6
NEG_INF = -1e30
LOG2E = 1.4426950408889634
LN2 = 0.6931471805599453
IN_COLS = 2 * D_RNN + 3 * ATT_W + 2 * D_MODEL
RADIUS = 64

LANES = 128
SUBLANES = 8
VMEM_LIMIT_BYTES = 56 * 1024 * 1024

COL_CHUNK = 512
RNN_CG = 128
RNN_T = 256
RNN_TB = 128
RNN_SEG_EXTRA = 4
ATT_BQ = 128
ATT_KW = ATT_BQ + 2 * RADIUS
ATT_UNROLL = 8
ATT_AHEAD = 2
FF_CHUNK = 256


def _resident(shape):
    nd = len(shape)
    return pl.BlockSpec(shape, lambda *_: (0,) * nd, pipeline_mode=pl.Buffered(1))


def _params(*sem):
    return pltpu.CompilerParams(dimension_semantics=sem, vmem_limit_bytes=VMEM_LIMIT_BYTES)


def _rms_mod(x, g, sc, sh):
    y = x * lax.rsqrt(jnp.mean(x * x, axis=-1, keepdims=True) + EPS)
    return y * (g * (1.0 + sc)) + sh


def _sigmoid(x):
    return 1.0 / (1.0 + jnp.exp2(x * -LOG2E))


def _gelu_tanh(x):
    return 0.5 * x * (1.0 + jnp.tanh(0.7978845608028654 * (x + 0.044715 * (x * x * x))))


def _mod_kernel(c_ref, w_ref, b_ref, o_ref):
    c = c_ref[...]
    s = (c * _sigmoid(c)).astype(BF16)
    o_ref[...] = jnp.dot(s, w_ref[...].astype(BF16), preferred_element_type=F32) + b_ref[...]


def _modulation(c, w_ada, b_ada):
    B = c.shape[0]
    rows = -(-B // 16) * 16
    cp = jnp.pad(c, ((0, rows - B), (0, 0)))
    out = pl.pallas_call(
        _mod_kernel,
        out_shape=jax.ShapeDtypeStruct((rows, N_MOD * D_MODEL), F32),
        grid=(N_MOD,),
        in_specs=[
            pl.BlockSpec((rows, D_MODEL), lambda j: (0, 0)),
            pl.BlockSpec((D_MODEL, D_MODEL), lambda j: (0, j)),
            pl.BlockSpec((1, D_MODEL), lambda j: (0, j)),
        ],
        out_specs=pl.BlockSpec((rows, D_MODEL), lambda j: (0, j)),
        compiler_params=_params("parallel"),
        name="mod",
    )(cp, w_ada, b_ada.reshape(1, -1))
    return out[:B].reshape(B, N_MOD, 1, D_MODEL)


def _fold_pitch(dil):
    return dil if dil % (2 * SUBLANES) else dil + SUBLANES


def _inproj_kernel(x_ref, sc_ref, sh_ref, g_ref, w_ref, cos_ref, sa_ref, sb_ref,
                   z_ref, q0_ref, q1_ref, q2_ref, fold_ref, *, tm):
    h = _rms_mod(x_ref[0], g_ref[...], sc_ref[0], sh_ref[0]).astype(BF16)

    def proj(j):
        return jnp.dot(h, w_ref[:, j * COL_CHUNK:(j + 1) * COL_CHUNK], preferred_element_type=F32)

    def rope(v):
        parts = []
        for s in range(COL_CHUNK // LANES):
            p = v[:, s * LANES:(s + 1) * LANES]
            up = pltpu.roll(p, LANES - ROT_DIM // 2, axis=1)
            dn = pltpu.roll(p, ROT_DIM // 2, axis=1)
            parts.append(p * cos_ref[...] + up * sa_ref[...] + dn * sb_ref[...])
        return jnp.concatenate(parts, axis=1)

    qkv_refs = (q0_ref, q1_ref, q2_ref)

    def emit_folded(val, g, off):
        dil = ATT_GROUPS[g][1]
        ref = qkv_refs[g]
        if dil == 1:
            ref[0, 0, :, off:off + COL_CHUNK] = val.astype(BF16)
            return
        pitch = _fold_pitch(dil)
        for s in range(COL_CHUNK // LANES):
            slab = val[:, s * LANES:(s + 1) * LANES]
            if pitch == dil:
                fold_ref[s, 0:tm, :] = slab
            else:
                for m in range(tm // dil):
                    fold_ref[s, pitch * m:pitch * m + dil, :] = slab[dil * m:dil * (m + 1)]
        for r in range(dil):
            for s in range(COL_CHUNK // LANES):
                lo = off + s * LANES
                ref[0, r, :, lo:lo + LANES] = fold_ref[s, pl.ds(r, tm // dil, stride=pitch), :].astype(BF16)

    def emit_z(j_w, j_z, fn):
        z_ref[0, :, j_z * COL_CHUNK:(j_z + 1) * COL_CHUNK] = fn(proj(j_w)).astype(BF16)

    n_x = D_RNN // COL_CHUNK
    n_rnn = 2 * n_x
    n_gate0 = n_rnn + 3 * N_GROUPS
    for j in range(2 * D_MODEL // COL_CHUNK):
        emit_z(n_gate0 + j, n_rnn + j, lambda v: v)
        if j < n_x:
            emit_z(n_x + j, n_x + j, _gelu_tanh)
    for g in range(N_GROUPS):
        emit_folded(rope(proj(n_rnn + g)) * (HEAD_DIM ** -0.5 * LOG2E), g, 0)
        emit_folded(rope(proj(n_rnn + N_GROUPS + g)), g, GROUP_W)
        emit_folded(proj(n_rnn + 2 * N_GROUPS + g), g, 2 * GROUP_W)
    for j in range(n_x):
        emit_z(j, j, lambda v: v)


def _rope_tables(S):
    half = ROT_DIM // 2
    inv = ROPE_THETA ** (-(jnp.arange(0, ROT_DIM, 2, dtype=F32) / ROT_DIM))
    ang = jnp.arange(S, dtype=F32)[:, None] * inv[None, :]
    cos, sin = jnp.cos(ang), jnp.sin(ang)
    zeros = jnp.zeros((S, HEAD_DIM - ROT_DIM), F32)
    z8 = jnp.zeros((S, half), F32)
    c = jnp.concatenate([cos, cos, zeros + 1.0], axis=1)
    sa = jnp.concatenate([-sin, z8, zeros], axis=1)
    sb = jnp.concatenate([z8, sin, zeros], axis=1)
    rep = LANES // HEAD_DIM
    return tuple(jnp.tile(t, (1, rep)) for t in (c, sa, sb))


def _inproj(x, sc, sh, g, w_in, tm):
    B, S, D = x.shape
    cos, sa, sb = _rope_tables(S)
    dils = [d for _, d in ATT_GROUPS]
    row = lambda b, i: (b, 0, 0)
    out_shape = [jax.ShapeDtypeStruct((B, S, 2 * D_RNN + 2 * D_MODEL), BF16)]
    out_specs = [pl.BlockSpec((1, tm, 2 * D_RNN + 2 * D_MODEL), lambda b, i: (b, i, 0))]
    for d in dils:
        out_shape.append(jax.ShapeDtypeStruct((B, d, S // d, 3 * GROUP_W), BF16))
        out_specs.append(pl.BlockSpec((1, d, tm // d, 3 * GROUP_W), lambda b, i: (b, 0, i, 0)))
    tab = pl.BlockSpec((tm, LANES), lambda b, i: (i, 0))
    return pl.pallas_call(
        functools.partial(_inproj_kernel, tm=tm),
        out_shape=out_shape,
        grid=(B, S // tm),
        in_specs=[
            pl.BlockSpec((1, tm, D), lambda b, i: (b, i, 0)),
            pl.BlockSpec((1, 1, D), row),
            pl.BlockSpec((1, 1, D), row),
            _resident((1, D)),
            _resident((D, IN_COLS)),
            tab, tab, tab,
        ],
        out_specs=out_specs,
        scratch_shapes=[pltpu.VMEM((COL_CHUNK // LANES, max(_fold_pitch(d) * (tm // d) for d in dils), LANES), F32)],
        compiler_params=_params("parallel", "parallel"),
        name="inproj",
    )(x, sc, sh, g, w_in, cos, sa, sb)


def _rnn_seg_len(S):
    assert S % (8 * SUBLANES) == 0
    return S // SUBLANES + RNN_SEG_EXTRA


def _rglru_kernel(x_ref, gate_ref, cw_ref, cb_ref, wg_ref, bg_ref, lam_ref, o_ref,
                  xnat_ref, hnat_ref, hloc_ref, acum_ref, *, S):
    C = RNN_CG
    T = RNN_T
    TB = RNN_TB
    PAD = SUBLANES
    seg = _rnn_seg_len(S)
    n_main = (seg - RNN_SEG_EXTRA) // TB
    rows = xnat_ref.shape[0]

    xnat_ref[0:PAD, :] = jnp.zeros((PAD, C), F32)
    xnat_ref[PAD + S:rows, :] = jnp.zeros((rows - PAD - S, C), F32)

    def fill(c, carry):
        r0 = pl.multiple_of(c * T, T)
        xnat_ref[pl.ds(r0 + PAD, T), :] = x_ref[0, pl.ds(r0, T), :].astype(F32)
        return carry

    lax.fori_loop(0, S // T, fill, 0)

    neg_lam = -lam_ref[0]
    softplus = jnp.maximum(neg_lam, 0.0) + jnp.log1p(jnp.exp(-jnp.abs(neg_lam)))
    half_coef2 = (-0.5 * RG_C * LOG2E) * softplus
    half_bias = 0.5 * bg_ref[0]
    cwb = [jnp.broadcast_to(cw_ref[k:k + 1, :], (SUBLANES, C)) for k in range(CONV_W)]
    cbb = jnp.broadcast_to(cb_ref[...], (SUBLANES, C))
    sub = lax.broadcasted_iota(jnp.int32, (SUBLANES, C), 0)
    steps_left = S - sub * seg

    def step_rows(t):
        return (pl.ds(PAD + t, SUBLANES, stride=seg), slice(None))

    def block_gates(t0, n, direction):
        taps = [xnat_ref[step_rows(t0 + m - CONV_LEFT)] for m in range(n + CONV_W - 1)]
        xcs = []
        for j in range(n):
            acc = cbb + taps[j] * cwb[0]
            for k in range(1, CONV_W):
                acc = acc + taps[j + k] * cwb[k]
            xcs.append(acc)
        xc = jnp.concatenate(xcs, axis=0)
        lo = direction * 2 * C
        gz = jnp.dot(xc.astype(BF16), wg_ref[0, :, lo:lo + 2 * C], preferred_element_type=F32)
        t = jnp.tanh(0.5 * gz + half_bias[:, lo:lo + 2 * C])
        half_coef = half_coef2[direction:direction + 1]
        a = jnp.exp2(half_coef * t[:, :C] + half_coef)
        y = 1.0 - a * a
        root = jnp.where(y > 0.0, y * lax.rsqrt(y), 0.0)
        half_xc = 0.5 * xc
        u = root * (half_xc * t[:, C:] + half_xc)
        return a, u

    def scan_block(t0, n, direction, carry, past_end=False):
        h, acc = carry
        a, u = block_gates(t0, n, direction)
        order = range(n) if direction == 0 else range(n - 1, -1, -1)
        for j in order:
            aj = a[j * SUBLANES:(j + 1) * SUBLANES]
            uj = u[j * SUBLANES:(j + 1) * SUBLANES]
            if direction == 1 and past_end:
                uj = jnp.where(t0 + j < steps_left, uj, 0.0)
            h = aj * h + uj
            acc = aj * acc
            row = pl.multiple_of((t0 + j) * SUBLANES, SUBLANES)
            hloc_ref[direction, pl.ds(row, SUBLANES), :] = h
            acum_ref[direction, pl.ds(row, SUBLANES), :] = acc
        return h, acc

    def entry_states(h_tot, a_tot, direction):
        c = jnp.zeros((1, C), F32)
        out = jnp.zeros((SUBLANES, C), F32)
        order = range(SUBLANES) if direction == 0 else range(SUBLANES - 1, -1, -1)
        for s in order:
            out = jnp.where(sub == s, c, out)
            c = a_tot[s:s + 1] * c + h_tot[s:s + 1]
        return out

    def fix_block(t0, n, entries):
        for j in range(n):
            row = pl.multiple_of((t0 + j) * SUBLANES, SUBLANES)
            h = None
            for direction in (0, 1):
                part = (hloc_ref[direction, pl.ds(row, SUBLANES), :]
                        + acum_ref[direction, pl.ds(row, SUBLANES), :] * entries[direction])
                h = part if h is None else h + part
            hnat_ref[step_rows(t0 + j)] = h

    tail0 = n_main * TB
    init = (jnp.zeros((SUBLANES, C), F32), jnp.ones((SUBLANES, C), F32))

    def block_start(b):
        return b * TB if isinstance(b, int) else pl.multiple_of(b * TB, TB)

    def main(b, carry, past_end=False):
        fwd = scan_block(block_start(b), TB, 0, carry[0])
        bwd = scan_block(block_start(n_main - 1 - b), TB, 1, carry[1], past_end)
        return fwd, bwd

    assert TB >= (SUBLANES - 1) * RNN_SEG_EXTRA and n_main >= 2
    bwd_tail = scan_block(tail0, RNN_SEG_EXTRA, 1, init, past_end=True)
    carry = main(0, (init, bwd_tail), past_end=True)
    fwd_tot, bwd_tot = lax.fori_loop(1, n_main, main, carry)
    fwd_tot = scan_block(tail0, RNN_SEG_EXTRA, 0, fwd_tot)
    entries = (entry_states(*fwd_tot, 0), entry_states(*bwd_tot, 1))

    def fix(b, carry):
        fix_block(pl.multiple_of(b * TB, TB), TB, entries)
        return carry

    lax.fori_loop(0, n_main, fix, 0)
    fix_block(tail0, RNN_SEG_EXTRA, entries)

    def finish(c, carry):
        r0 = pl.multiple_of(c * T, T)
        gate = gate_ref[0, pl.ds(r0, T), :].astype(F32)
        o_ref[0, pl.ds(r0, T), :] = (hnat_ref[pl.ds(r0 + PAD, T), :] * gate).astype(BF16)
        return carry

    lax.fori_loop(0, S // T, finish, 0)


def _block_diag_pairs(w):
    per = RNN_CG // RG_BW
    w = w.reshape(RG_BLOCKS // per, per, RG_BW, RG_BW)
    rows = []
    for p in range(per):
        cols = [w[:, p] if q == p else jnp.zeros_like(w[:, p]) for q in range(per)]
        rows.append(jnp.concatenate(cols, axis=-1))
    return jnp.concatenate(rows, axis=1)


def _rglru(z, conv_w, conv_b, rg_wa, rg_ba, rg_wx, rg_bx, rg_lambda):
    B, S, _ = z.shape
    C = RNN_CG
    n_grp = D_RNN // C
    wg = jnp.concatenate([_block_diag_pairs(rg_wa[0]), _block_diag_pairs(rg_wx[0]),
                          _block_diag_pairs(rg_wa[1]), _block_diag_pairs(rg_wx[1])], axis=-1).astype(BF16)
    bg = jnp.concatenate([rg_ba[0].reshape(n_grp, 1, C), rg_bx[0].reshape(n_grp, 1, C),
                          rg_ba[1].reshape(n_grp, 1, C), rg_bx[1].reshape(n_grp, 1, C)], axis=-1)
    lam = rg_lambda.reshape(2, n_grp, C).transpose(1, 0, 2)
    step_rows = SUBLANES * _rnn_seg_len(S)
    nat_rows = step_rows + 2 * SUBLANES
    return pl.pallas_call(
        functools.partial(_rglru_kernel, S=S),
        out_shape=jax.ShapeDtypeStruct((B, S, D_RNN), BF16),
        grid=(B, n_grp),
        in_specs=[
            pl.BlockSpec((1, S, C), lambda b, c: (b, 0, c)),
            pl.BlockSpec((1, S, C), lambda b, c: (b, 0, n_grp + c)),
            pl.BlockSpec((CONV_W, C), lambda b, c: (0, c)),
            pl.BlockSpec((1, C), lambda b, c: (0, c)),
            pl.BlockSpec((1, C, 4 * C), lambda b, c: (c, 0, 0)),
            pl.BlockSpec((1, 1, 4 * C), lambda b, c: (c, 0, 0)),
            pl.BlockSpec((1, 2, C), lambda b, c: (c, 0, 0)),
        ],
        out_specs=pl.BlockSpec((1, S, C), lambda b, c: (b, 0, c)),
        scratch_shapes=[pltpu.VMEM((nat_rows, C), F32), pltpu.VMEM((nat_rows, C), F32),
                        pltpu.VMEM((2, step_rows, C), F32), pltpu.VMEM((2, step_rows, C), F32)],
        compiler_params=_params("parallel", "parallel"),
        name="rglru",
    )(z, z, conv_w, conv_b.reshape(1, -1), wg, bg, lam)


def _attn_kernel(q_ref, k_ref, v_ref, o_ref, l_ref, cap_ref, *, L, n_res):
    BQ, KW = ATT_BQ, ATT_KW
    lane = lax.broadcasted_iota(jnp.int32, (BQ, LANES), 1)
    first = lane < HEAD_DIM
    first_kw = lax.broadcasted_iota(jnp.int32, (KW, LANES), 1) < HEAD_DIM
    rel = lax.broadcasted_iota(jnp.int32, (BQ, KW), 0) - lax.broadcasted_iota(jnp.int32, (BQ, KW), 1)
    n_blocks = L // BQ
    total = n_res * n_blocks
    assert n_blocks & (n_blocks - 1) == 0 and total % ATT_UNROLL == 0

    for j in range(3):
        cap = jnp.where(jnp.abs(rel + j * RADIUS) <= RADIUS, jnp.inf, NEG_INF).astype(F32)
        cap_ref[j, 0:BQ, :] = cap
        cap_ref[j, BQ:2 * BQ, :] = cap

    def scores(idx):
        r = lax.shift_right_logical(idx, n_blocks.bit_length() - 1)
        q0 = pl.multiple_of(jnp.bitwise_and(idx, n_blocks - 1) * BQ, BQ)
        k0 = pl.multiple_of(jnp.clip(q0 - RADIUS, 0, L - KW), RADIUS)
        q = q_ref[0, r, pl.ds(q0, BQ), :]
        k = k_ref[0, r, pl.ds(k0, KW), :]
        zero = jnp.zeros_like(q)
        q2 = jnp.concatenate([jnp.where(first, q, zero), jnp.where(first, zero, q)], axis=0)
        s = lax.dot_general(q2, k, (((1,), (1,)), ((), ())), preferred_element_type=F32)
        return r, q0, k0, s

    def finish(r, q0, k0, s):
        cap = cap_ref[lax.shift_right_logical(q0 - k0, RADIUS.bit_length() - 1)]
        v = v_ref[0, r, pl.ds(k0, KW), :]
        s = jnp.minimum(s, cap)
        m = jnp.max(s, axis=-1, keepdims=True)
        p = jnp.exp2(s - m).astype(BF16)
        one = jnp.ones_like(v)
        pv_a = jnp.dot(p[:BQ], jnp.where(first_kw, v, one), preferred_element_type=F32)
        pv_b = jnp.dot(p[BQ:], jnp.where(first_kw, one, v), preferred_element_type=F32)
        num = jnp.where(first, pv_a, pv_b)
        den = pltpu.roll(jnp.where(first, pv_b, pv_a), HEAD_DIM, axis=1)
        top = jnp.where(first, m[:BQ], m[BQ:])
        o_ref[0, r, pl.ds(q0, BQ), :] = (num / den).astype(BF16)
        l_ref[0, r, pl.ds(q0, BQ), :] = top * LN2 + jnp.log(den)

    def body(it, carry):
        pending = [scores(it * ATT_UNROLL + j) for j in range(ATT_AHEAD)]
        for j in range(ATT_UNROLL):
            if j + ATT_AHEAD < ATT_UNROLL:
                pending.append(scores(it * ATT_UNROLL + j + ATT_AHEAD))
            finish(*pending.pop(0))
        return carry

    lax.fori_loop(0, total // ATT_UNROLL, body, 0)


def _attention(qkv):
    B, dil, L, _ = qkv.shape
    n_pair = GROUP_W // LANES
    spec = lambda off: pl.BlockSpec((1, dil, L, LANES), lambda b, p: (b, 0, 0, off + p))
    return pl.pallas_call(
        functools.partial(_attn_kernel, L=L, n_res=dil),
        out_shape=[jax.ShapeDtypeStruct((B, dil, L, GROUP_W), BF16),
                   jax.ShapeDtypeStruct((B, dil, L, GROUP_W), F32)],
        grid=(B, n_pair),
        in_specs=[spec(0), spec(n_pair), spec(2 * n_pair)],
        out_specs=[spec(0), spec(0)],
        scratch_shapes=[pltpu.VMEM((3, 2 * ATT_BQ, ATT_KW), F32)],
        compiler_params=_params("parallel", "parallel"),
        name="attn",
    )(qkv, qkv, qkv)


def _mixffn_kernel(x_ref, gt1_ref, sc_ref, sh_ref, gt2_ref, g2_ref, gf_ref, mg_ref, rnn_ref,
                   o0_ref, l0_ref, o1_ref, l1_ref, o2_ref, l2_ref,
                   wr_ref, wa_ref, wo_ref, wi_ref, wf_ref, y_ref,
                   fo1_ref, fl1_ref, fo2_ref, fl2_ref, *, tm):
    n_tiles = GROUP_W // LANES

    def unfold(src_ref, dst_ref, dil):
        pitch = _fold_pitch(dil)
        for r in range(dil):
            for s in range(n_tiles):
                dst_ref[s, pl.ds(r, tm // dil, stride=pitch), :] = src_ref[0, r, :, s * LANES:(s + 1) * LANES].astype(F32)
        if pitch == dil:
            return jnp.concatenate([dst_ref[s, 0:tm, :] for s in range(n_tiles)], axis=1)
        return jnp.concatenate(
            [jnp.concatenate([dst_ref[s, pitch * m:pitch * m + dil, :] for m in range(tm // dil)], axis=0)
             for s in range(n_tiles)], axis=1)

    o1 = unfold(o1_ref, fo1_ref, ATT_GROUPS[1][1])
    l1 = unfold(l1_ref, fl1_ref, ATT_GROUPS[1][1])
    o2 = unfold(o2_ref, fo2_ref, ATT_GROUPS[2][1])
    l2 = unfold(l2_ref, fl2_ref, ATT_GROUPS[2][1])
    lses = (l0_ref[0, 0], l1, l2)
    outs = (o0_ref[0, 0].astype(F32), o1, o2)

    top = jnp.maximum(jnp.maximum(lses[0], lses[1]), lses[2])
    es = [jnp.exp(l - top) for l in lses]
    att = (es[0] * outs[0] + es[1] * outs[1] + es[2] * outs[2]) / (es[0] + es[1] + es[2])

    gate = _sigmoid(mg_ref[0].astype(F32))
    br_r = jnp.dot(rnn_ref[0], wr_ref[...], preferred_element_type=F32)
    br_a = jnp.dot(att.astype(BF16), wa_ref[...], preferred_element_type=F32)
    merged = gate[:, :D_MODEL] * br_r + gate[:, D_MODEL:] * br_a
    mix = jnp.dot(merged.astype(BF16), wo_ref[...], preferred_element_type=F32)
    x1 = x_ref[0] + gt1_ref[0] * mix

    h = _rms_mod(x1, g2_ref[...], sc_ref[0], sh_ref[0]).astype(BF16)
    ff = None
    for c in range(D_FF // FF_CHUNK):
        lo = c * FF_CHUNK
        fg = jnp.dot(h, wi_ref[:, lo:lo + FF_CHUNK], preferred_element_type=F32)
        fu = jnp.dot(h, wi_ref[:, D_FF + lo:D_FF + lo + FF_CHUNK], preferred_element_type=F32)
        act = ((fg * _sigmoid(fg)) * fu).astype(BF16)
        part = jnp.dot(act, wf_ref[lo:lo + FF_CHUNK, :], preferred_element_type=F32)
        ff = part if ff is None else ff + part
    x2 = x1 + gt2_ref[0] * ff
    y = x2 * lax.rsqrt(jnp.mean(x2 * x2, axis=-1, keepdims=True) + EPS)
    y_ref[0] = y * gf_ref[...]


def _mixffn(x, mods, g2, gf, z, rnn, attn_outs, weights, tm):
    B, S, D = x.shape
    mod_spec = pl.BlockSpec((1, 1, D), lambda b, i: (b, 0, 0))
    in_specs = [pl.BlockSpec((1, tm, D), lambda b, i: (b, i, 0))] + [mod_spec] * len(mods)
    in_specs += [
        _resident((1, D)),
        _resident((1, D)),
        pl.BlockSpec((1, tm, 2 * D_MODEL), lambda b, i: (b, i, 1)),
        pl.BlockSpec((1, tm, D_RNN), lambda b, i: (b, i, 0)),
    ]
    args = [x, *mods, g2, gf, z, rnn]
    for (o, l), (_, d) in zip(attn_outs, ATT_GROUPS):
        blk = pl.BlockSpec((1, d, tm // d, GROUP_W), lambda b, i: (b, 0, i, 0))
        in_specs += [blk, blk]
        args += [o, l]
    in_specs += [_resident(w.shape) for w in weights]
    args += list(weights)
    return pl.pallas_call(
        functools.partial(_mixffn_kernel, tm=tm),
        out_shape=jax.ShapeDtypeStruct((B, S, D), F32),
        grid=(B, S // tm),
        in_specs=in_specs,
        out_specs=pl.BlockSpec((1, tm, D), lambda b, i: (b, i, 0)),
        scratch_shapes=[pltpu.VMEM((GROUP_W // LANES, _fold_pitch(d) * (tm // d), LANES), F32)
                        for _, d in ATT_GROUPS[1:] for _ in range(2)],
        compiler_params=_params("parallel", "parallel"),
        name="mixffn",
    )(*args)


def _encode(x, mod, w, tm_in, tm_mix):
    sh1, sc1, gt1, sh2, sc2, gt2 = (mod[:, k] for k in range(N_MOD))
    outs = _inproj(x, sc1, sh1, w["norm1_g"], w["w_in"], tm_in)
    z, qkvs = outs[0], outs[1:]
    rnn = _rglru(z, w["conv_w"], w["conv_b"], w["rg_wa"], w["rg_ba"], w["rg_wx"], w["rg_bx"], w["rg_lambda"])
    attn_outs = [_attention(qkv) for qkv in qkvs]
    weights = (w["w_br_rnn"], w["w_br_attn"], w["w_out"], w["w_ffn_in"], w["w_ffn_out"])
    return _mixffn(x, (gt1, sc2, sh2, gt2), w["norm2_g"], w["final_g"], z, rnn, attn_outs, weights, tm_mix)


def kernel(x_prompt, x_sample, c_prompt, c_sample, w_ada, b_ada, norm1_g, w_in, conv_w, conv_b, rg_wa, rg_ba, rg_wx, rg_bx, rg_lambda, w_br_rnn, w_br_attn, w_out, norm2_g, w_ffn_in, w_ffn_out, final_g):
    assert w_ada.shape[0] == 1, "single layer"
    w = dict(
        norm1_g=norm1_g[0].reshape(1, -1),
        w_in=w_in[0].astype(BF16), conv_w=conv_w[0], conv_b=conv_b[0],
        rg_wa=rg_wa[0], rg_ba=rg_ba[0], rg_wx=rg_wx[0], rg_bx=rg_bx[0], rg_lambda=rg_lambda[0],
        w_br_rnn=w_br_rnn[0].astype(BF16), w_br_attn=w_br_attn[0].astype(BF16), w_out=w_out[0].astype(BF16),
        norm2_g=norm2_g[0].reshape(1, -1), w_ffn_in=w_ffn_in[0].astype(BF16),
        w_ffn_out=w_ffn_out[0].astype(BF16), final_g=final_g.reshape(1, -1),
    )
    tiles = dict(tm_in=512, tm_mix=512)
    n_prompt = c_prompt.shape[0]
    mod = _modulation(jnp.concatenate([c_prompt, c_sample], axis=0), w_ada[0], b_ada[0])
    return (_encode(x_prompt, mod[:n_prompt], w, **tiles), _encode(x_sample, mod[n_prompt:], w, **tiles))
```

```python
import functools

import jax
import jax.numpy as jnp
from jax import lax
from jax.experimental import pallas as pl
from jax.experimental.pallas import tpu as pltpu

F32 = jnp.float32
BF16 = jnp.bfloat16

D_MODEL = 1024
D_RNN = 1024
RG_BLOCKS = 16
RG_BW = D_RNN // RG_BLOCKS
RG_C = 8.0
CONV_W = 4
CONV_LEFT = 2
HEAD_DIM = 64
HEADS_PER_GROUP = 8
ATT_GROUPS = ((128, 1), (512, 4), (2048, 16))
N_GROUPS = len(ATT_GROUPS)
GROUP_W = HEADS_PER_GROUP * HEAD_DIM
ATT_W = N_GROUPS * GROUP_W
ROT_DIM = HEAD_DIM // 4
ROPE_THETA = 500000.0
D_FF = 2816
N_MOD = 6
EPS = 1e-6
NEG_INF = -1e30
LOG2E = 1.4426950408889634
LN2 = 0.6931471805599453
IN_COLS = 2 * D_RNN + 3 * ATT_W + 2 * D_MODEL
RADIUS = 64

LANES = 128
SUBLANES = 8
VMEM_LIMIT_BYTES = 56 * 1024 * 1024

COL_CHUNK = 512
RNN_CG = 128
RNN_T = 256
RNN_TB = 128
RNN_SEG_EXTRA = 4
ATT_BQ = 128
ATT_KW = ATT_BQ + 2 * RADIUS
ATT_UNROLL = 8
ATT_AHEAD = 2
FF_CHUNK = 256


def _resident(shape):
    nd = len(shape)
    return pl.BlockSpec(shape, lambda *_: (0,) * nd, pipeline_mode=pl.Buffered(1))


def _params(*sem):
    return pltpu.CompilerParams(dimension_semantics=sem, vmem_limit_bytes=VMEM_LIMIT_BYTES)


def _rms_mod(x, g, sc, sh):
    y = x * lax.rsqrt(jnp.mean(x * x, axis=-1, keepdims=True) + EPS)
    return y * (g * (1.0 + sc)) + sh


def _sigmoid(x):
    return 1.0 / (1.0 + jnp.exp2(x * -LOG2E))


def _gelu_tanh(x):
    return 0.5 * x * (1.0 + jnp.tanh(0.7978845608028654 * (x + 0.044715 * (x * x * x))))


def _mod_kernel(c_ref, w_ref, b_ref, o_ref):
    c = c_ref[...]
    s = (c * _sigmoid(c)).astype(BF16)
    o_ref[...] = jnp.dot(s, w_ref[...].astype(BF16), preferred_element_type=F32) + b_ref[...]


def _modulation(c, w_ada, b_ada):
    B = c.shape[0]
    rows = -(-B // 16) * 16
    cp = jnp.pad(c, ((0, rows - B), (0, 0)))
    out = pl.pallas_call(
        _mod_kernel,
        out_shape=jax.ShapeDtypeStruct((rows, N_MOD * D_MODEL), F32),
        grid=(N_MOD,),
        in_specs=[
            pl.BlockSpec((rows, D_MODEL), lambda j: (0, 0)),
            pl.BlockSpec((D_MODEL, D_MODEL), lambda j: (0, j)),
            pl.BlockSpec((1, D_MODEL), lambda j: (0, j)),
        ],
        out_specs=pl.BlockSpec((rows, D_MODEL), lambda j: (0, j)),
        compiler_params=_params("parallel"),
        name="mod",
    )(cp, w_ada, b_ada.reshape(1, -1))
    return out[:B].reshape(B, N_MOD, 1, D_MODEL)


def _fold_pitch(dil):
    return dil if dil % (2 * SUBLANES) else dil + SUBLANES


def _inproj_kernel(x_ref, sc_ref, sh_ref, g_ref, w_ref, cos_ref, sa_ref, sb_ref,
                   z_ref, q0_ref, q1_ref, q2_ref, fold_ref, *, tm):
    h = _rms_mod(x_ref[0], g_ref[...], sc_ref[0], sh_ref[0]).astype(BF16)

    def proj(j):
        return jnp.dot(h, w_ref[:, j * COL_CHUNK:(j + 1) * COL_CHUNK], preferred_element_type=F32)

    def rope(v):
        parts = []
        for s in range(COL_CHUNK // LANES):
            p = v[:, s * LANES:(s + 1) * LANES]
            up = pltpu.roll(p, LANES - ROT_DIM // 2, axis=1)
            dn = pltpu.roll(p, ROT_DIM // 2, axis=1)
            parts.append(p * cos_ref[...] + up * sa_ref[...] + dn * sb_ref[...])
        return jnp.concatenate(parts, axis=1)

    qkv_refs = (q0_ref, q1_ref, q2_ref)

    def emit_folded(val, g, off):
        dil = ATT_GROUPS[g][1]
        ref = qkv_refs[g]
        if dil == 1:
            ref[0, 0, :, off:off + COL_CHUNK] = val.astype(BF16)
            return
        pitch = _fold_pitch(dil)
        for s in range(COL_CHUNK // LANES):
            slab = val[:, s * LANES:(s + 1) * LANES]
            if pitch == dil:
                fold_ref[s, 0:tm, :] = slab
            else:
                for m in range(tm // dil):
                    fold_ref[s, pitch * m:pitch * m + dil, :] = slab[dil * m:dil * (m + 1)]
        for r in range(dil):
            for s in range(COL_CHUNK // LANES):
                lo = off + s * LANES
                ref[0, r, :, lo:lo + LANES] = fold_ref[s, pl.ds(r, tm // dil, stride=pitch), :].astype(BF16)

    def emit_z(j_w, j_z, fn):
        z_ref[0, :, j_z * COL_CHUNK:(j_z + 1) * COL_CHUNK] = fn(proj(j_w)).astype(BF16)

    n_x = D_RNN // COL_CHUNK
    n_rnn = 2 * n_x
    n_gate0 = n_rnn + 3 * N_GROUPS
    for j in range(2 * D_MODEL // COL_CHUNK):
        emit_z(n_gate0 + j, n_rnn + j, lambda v: v)
        if j < n_x:
            emit_z(n_x + j, n_x + j, _gelu_tanh)
    for g in range(N_GROUPS):
        emit_folded(rope(proj(n_rnn + g)) * (HEAD_DIM ** -0.5 * LOG2E), g, 0)
        emit_folded(rope(proj(n_rnn + N_GROUPS + g)), g, GROUP_W)
        emit_folded(proj(n_rnn + 2 * N_GROUPS + g), g, 2 * GROUP_W)
    for j in range(n_x):
        emit_z(j, j, lambda v: v)


def _rope_tables(S):
    half = ROT_DIM // 2
    inv = ROPE_THETA ** (-(jnp.arange(0, ROT_DIM, 2, dtype=F32) / ROT_DIM))
    ang = jnp.arange(S, dtype=F32)[:, None] * inv[None, :]
    cos, sin = jnp.cos(ang), jnp.sin(ang)
    zeros = jnp.zeros((S, HEAD_DIM - ROT_DIM), F32)
    z8 = jnp.zeros((S, half), F32)
    c = jnp.concatenate([cos, cos, zeros + 1.0], axis=1)
    sa = jnp.concatenate([-sin, z8, zeros], axis=1)
    sb = jnp.concatenate([z8, sin, zeros], axis=1)
    rep = LANES // HEAD_DIM
    return tuple(jnp.tile(t, (1, rep)) for t in (c, sa, sb))


def _inproj(x, sc, sh, g, w_in, tm):
    B, S, D = x.shape
    cos, sa, sb = _rope_tables(S)
    dils = [d for _, d in ATT_GROUPS]
    row = lambda b, i: (b, 0, 0)
    out_shape = [jax.ShapeDtypeStruct((B, S, 2 * D_RNN + 2 * D_MODEL), BF16)]
    out_specs = [pl.BlockSpec((1, tm, 2 * D_RNN + 2 * D_MODEL), lambda b, i: (b, i, 0))]
    for d in dils:
        out_shape.append(jax.ShapeDtypeStruct((B, d, S // d, 3 * GROUP_W), BF16))
        out_specs.append(pl.BlockSpec((1, d, tm // d, 3 * GROUP_W), lambda b, i: (b, 0, i, 0)))
    tab = pl.BlockSpec((tm, LANES), lambda b, i: (i, 0))
    return pl.pallas_call(
        functools.partial(_inproj_kernel, tm=tm),
        out_shape=out_shape,
        grid=(B, S // tm),
        in_specs=[
            pl.BlockSpec((1, tm, D), lambda b, i: (b, i, 0)),
            pl.BlockSpec((1, 1, D), row),
            pl.BlockSpec((1, 1, D), row),
            _resident((1, D)),
            _resident((D, IN_COLS)),
            tab, tab, tab,
        ],
        out_specs=out_specs,
        scratch_shapes=[pltpu.VMEM((COL_CHUNK // LANES, max(_fold_pitch(d) * (tm // d) for d in dils), LANES), F32)],
        compiler_params=_params("parallel", "parallel"),
        name="inproj",
    )(x, sc, sh, g, w_in, cos, sa, sb)


def _rnn_seg_len(S):
    assert S % (8 * SUBLANES) == 0
    return S // SUBLANES + RNN_SEG_EXTRA


def _rglru_kernel(x_ref, gate_ref, cw_ref, cb_ref, wg_ref, bg_ref, lam_ref, o_ref,
                  xnat_ref, hnat_ref, hloc_ref, acum_ref, xc_ref, *, S):
    C = RNN_CG
    T = RNN_T
    TB = RNN_TB
    PAD = SUBLANES
    seg = _rnn_seg_len(S)
    n_main = (seg - RNN_SEG_EXTRA) // TB
    rows = xnat_ref.shape[0]

    xnat_ref[0:PAD, :] = jnp.zeros((PAD, C), F32)
    xnat_ref[PAD + S:rows, :] = jnp.zeros((rows - PAD - S, C), F32)

    def fill(c, carry):
        r0 = pl.multiple_of(c * T, T)
        xnat_ref[pl.ds(r0 + PAD, T), :] = x_ref[0, pl.ds(r0, T), :].astype(F32)
        return carry

    lax.fori_loop(0, S // T, fill, 0)

    neg_lam = -lam_ref[0]
    softplus = jnp.maximum(neg_lam, 0.0) + jnp.log1p(jnp.exp(-jnp.abs(neg_lam)))
    half_coef2 = (-0.5 * RG_C * LOG2E) * softplus
    half_bias = 0.5 * bg_ref[0]
    cwb = [jnp.broadcast_to(cw_ref[k:k + 1, :], (SUBLANES, C)) for k in range(CONV_W)]
    cbb = jnp.broadcast_to(cb_ref[...], (SUBLANES, C))
    sub = lax.broadcasted_iota(jnp.int32, (SUBLANES, C), 0)
    steps_left = S - sub * seg

    def step_rows(t):
        return (pl.ds(PAD + t, SUBLANES, stride=seg), slice(None))

    def block_gates(t0, n, direction, conv):
        rows = pl.ds(pl.multiple_of(t0 * SUBLANES, SUBLANES), n * SUBLANES)
        if conv == "reuse":
            xc = xc_ref[rows, :]
        else:
            taps = [xnat_ref[step_rows(t0 + m - CONV_LEFT)] for m in range(n + CONV_W - 1)]
            xcs = []
            for j in range(n):
                acc = cbb + taps[j] * cwb[0]
                for k in range(1, CONV_W):
                    acc = acc + taps[j + k] * cwb[k]
                xcs.append(acc)
            xc = jnp.concatenate(xcs, axis=0)
            if conv == "keep":
                xc_ref[rows, :] = xc
        lo = direction * 2 * C
        gz = jnp.dot(xc.astype(BF16), wg_ref[0, :, lo:lo + 2 * C], preferred_element_type=F32)
        t = jnp.tanh(0.5 * gz + half_bias[:, lo:lo + 2 * C])
        half_coef = half_coef2[direction:direction + 1]
        a = jnp.exp2(half_coef * t[:, :C] + half_coef)
        y = 1.0 - a * a
        root = jnp.where(y > 0.0, y * lax.rsqrt(y), 0.0)
        half_xc = 0.5 * xc
        u = root * (half_xc * t[:, C:] + half_xc)
        return a, u

    def scan_block(t0, n, direction, carry, past_end=False, conv="own"):
        h, acc = carry
        a, u = block_gates(t0, n, direction, conv)
        order = range(n) if direction == 0 else range(n - 1, -1, -1)
        for j in order:
            aj = a[j * SUBLANES:(j + 1) * SUBLANES]
            uj = u[j * SUBLANES:(j + 1) * SUBLANES]
            if direction == 1 and past_end:
                uj = jnp.where(t0 + j < steps_left, uj, 0.0)
            h = aj * h + uj
            acc = aj * acc
            row = pl.multiple_of((t0 + j) * SUBLANES, SUBLANES)
            hloc_ref[direction, pl.ds(row, SUBLANES), :] = h
            acum_ref[direction, pl.ds(row, SUBLANES), :] = acc
        return h, acc

    def entry_states(h_tot, a_tot, direction):
        c = jnp.zeros((1, C), F32)
        out = jnp.zeros((SUBLANES, C), F32)
        order = range(SUBLANES) if direction == 0 else range(SUBLANES - 1, -1, -1)
        for s in order:
            out = jnp.where(sub == s, c, out)
            c = a_tot[s:s + 1] * c + h_tot[s:s + 1]
        return out

    def fix_block(t0, n, entries):
        for j in range(n):
            row = pl.multiple_of((t0 + j) * SUBLANES, SUBLANES)
            h = None
            for direction in (0, 1):
                part = (hloc_ref[direction, pl.ds(row, SUBLANES), :]
                        + acum_ref[direction, pl.ds(row, SUBLANES), :] * entries[direction])
                h = part if h is None else h + part
            hnat_ref[step_rows(t0 + j)] = h

    tail0 = n_main * TB
    init = (jnp.zeros((SUBLANES, C), F32), jnp.ones((SUBLANES, C), F32))

    def block_start(b):
        return b * TB if isinstance(b, int) else pl.multiple_of(b * TB, TB)

    def main(b, carry, past_end=False, conv="keep"):
        fwd = scan_block(block_start(b), TB, 0, carry[0], conv=conv)
        bwd = scan_block(block_start(n_main - 1 - b), TB, 1, carry[1], past_end, conv=conv)
        return fwd, bwd

    assert TB >= (SUBLANES - 1) * RNN_SEG_EXTRA and n_main >= 2 and n_main % 2 == 0
    bwd_tail = scan_block(tail0, RNN_SEG_EXTRA, 1, init, past_end=True)
    carry = main(0, (init, bwd_tail), past_end=True)
    carry = lax.fori_loop(1, n_main // 2, main, carry)
    fwd_tot, bwd_tot = lax.fori_loop(n_main // 2, n_main, functools.partial(main, conv="reuse"), carry)
    fwd_tot = scan_block(tail0, RNN_SEG_EXTRA, 0, fwd_tot)
    entries = (entry_states(*fwd_tot, 0), entry_states(*bwd_tot, 1))

    def fix(b, carry):
        fix_block(pl.multiple_of(b * TB, TB), TB, entries)
        return carry

    lax.fori_loop(0, n_main, fix, 0)
    fix_block(tail0, RNN_SEG_EXTRA, entries)

    def finish(c, carry):
        r0 = pl.multiple_of(c * T, T)
        gate = gate_ref[0, pl.ds(r0, T), :].astype(F32)
        o_ref[0, pl.ds(r0, T), :] = (hnat_ref[pl.ds(r0 + PAD, T), :] * gate).astype(BF16)
        return carry

    lax.fori_loop(0, S // T, finish, 0)


def _block_diag_pairs(w):
    per = RNN_CG // RG_BW
    w = w.reshape(RG_BLOCKS // per, per, RG_BW, RG_BW)
    rows = []
    for p in range(per):
        cols = [w[:, p] if q == p else jnp.zeros_like(w[:, p]) for q in range(per)]
        rows.append(jnp.concatenate(cols, axis=-1))
    return jnp.concatenate(rows, axis=1)


def _rglru(z, conv_w, conv_b, rg_wa, rg_ba, rg_wx, rg_bx, rg_lambda):
    B, S, _ = z.shape
    C = RNN_CG
    n_grp = D_RNN // C
    wg = jnp.concatenate([_block_diag_pairs(rg_wa[0]), _block_diag_pairs(rg_wx[0]),
                          _block_diag_pairs(rg_wa[1]), _block_diag_pairs(rg_wx[1])], axis=-1).astype(BF16)
    bg = jnp.concatenate([rg_ba[0].reshape(n_grp, 1, C), rg_bx[0].reshape(n_grp, 1, C),
                          rg_ba[1].reshape(n_grp, 1, C), rg_bx[1].reshape(n_grp, 1, C)], axis=-1)
    lam = rg_lambda.reshape(2, n_grp, C).transpose(1, 0, 2)
    step_rows = SUBLANES * _rnn_seg_len(S)
    nat_rows = step_rows + 2 * SUBLANES
    return pl.pallas_call(
        functools.partial(_rglru_kernel, S=S),
        out_shape=jax.ShapeDtypeStruct((B, S, D_RNN), BF16),
        grid=(B, n_grp),
        in_specs=[
            pl.BlockSpec((1, S, C), lambda b, c: (b, 0, c)),
            pl.BlockSpec((1, S, C), lambda b, c: (b, 0, n_grp + c)),
            pl.BlockSpec((CONV_W, C), lambda b, c: (0, c)),
            pl.BlockSpec((1, C), lambda b, c: (0, c)),
            pl.BlockSpec((1, C, 4 * C), lambda b, c: (c, 0, 0)),
            pl.BlockSpec((1, 1, 4 * C), lambda b, c: (c, 0, 0)),
            pl.BlockSpec((1, 2, C), lambda b, c: (c, 0, 0)),
        ],
        out_specs=pl.BlockSpec((1, S, C), lambda b, c: (b, 0, c)),
        scratch_shapes=[pltpu.VMEM((nat_rows, C), F32), pltpu.VMEM((nat_rows, C), F32),
                        pltpu.VMEM((2, step_rows, C), F32), pltpu.VMEM((2, step_rows, C), F32),
                        pltpu.VMEM((step_rows, C), F32)],
        compiler_params=_params("parallel", "parallel"),
        name="rglru",
    )(z, z, conv_w, conv_b.reshape(1, -1), wg, bg, lam)


def _attn_kernel(q_ref, k_ref, v_ref, o_ref, l_ref, cap_ref, *, L, n_res):
    BQ, KW = ATT_BQ, ATT_KW
    lane = lax.broadcasted_iota(jnp.int32, (BQ, LANES), 1)
    first = lane < HEAD_DIM
    first_kw = lax.broadcasted_iota(jnp.int32, (KW, LANES), 1) < HEAD_DIM
    rel = lax.broadcasted_iota(jnp.int32, (BQ, KW), 0) - lax.broadcasted_iota(jnp.int32, (BQ, KW), 1)
    n_blocks = L // BQ
    total = n_res * n_blocks
    assert n_blocks & (n_blocks - 1) == 0 and total % ATT_UNROLL == 0

    for j in range(3):
        cap = jnp.where(jnp.abs(rel + j * RADIUS) <= RADIUS, jnp.inf, NEG_INF).astype(F32)
        cap_ref[j, 0:BQ, :] = cap
        cap_ref[j, BQ:2 * BQ, :] = cap

    def scores(idx):
        r = lax.shift_right_logical(idx, n_blocks.bit_length() - 1)
        q0 = pl.multiple_of(jnp.bitwise_and(idx, n_blocks - 1) * BQ, BQ)
        k0 = pl.multiple_of(jnp.clip(q0 - RADIUS, 0, L - KW), RADIUS)
        q = q_ref[0, r, pl.ds(q0, BQ), :]
        k = k_ref[0, r, pl.ds(k0, KW), :]
        zero = jnp.zeros_like(q)
        q2 = jnp.concatenate([jnp.where(first, q, zero), jnp.where(first, zero, q)], axis=0)
        s = lax.dot_general(q2, k, (((1,), (1,)), ((), ())), preferred_element_type=F32)
        return r, q0, k0, s

    def finish(r, q0, k0, s):
        cap = cap_ref[lax.shift_right_logical(q0 - k0, RADIUS.bit_length() - 1)]
        v = v_ref[0, r, pl.ds(k0, KW), :]
        s = jnp.minimum(s, cap)
        m = jnp.max(s, axis=-1, keepdims=True)
        p = jnp.exp2(s - m).astype(BF16)
        one = jnp.ones_like(v)
        pv_a = jnp.dot(p[:BQ], jnp.where(first_kw, v, one), preferred_element_type=F32)
        pv_b = jnp.dot(p[BQ:], jnp.where(first_kw, one, v), preferred_element_type=F32)
        num = jnp.where(first, pv_a, pv_b)
        den = pltpu.roll(jnp.where(first, pv_b, pv_a), HEAD_DIM, axis=1)
        top = jnp.where(first, m[:BQ], m[BQ:])
        o_ref[0, r, pl.ds(q0, BQ), :] = (num / den).astype(BF16)
        l_ref[0, r, pl.ds(q0, BQ), :] = top * LN2 + jnp.log(den)

    def body(it, carry):
        pending = [scores(it * ATT_UNROLL + j) for j in range(ATT_AHEAD)]
        for j in range(ATT_UNROLL):
            if j + ATT_AHEAD < ATT_UNROLL:
                pending.append(scores(it * ATT_UNROLL + j + ATT_AHEAD))
            finish(*pending.pop(0))
        return carry

    lax.fori_loop(0, total // ATT_UNROLL, body, 0)


def _attention(qkv):
    B, dil, L, _ = qkv.shape
    n_pair = GROUP_W // LANES
    spec = lambda off: pl.BlockSpec((1, dil, L, LANES), lambda b, p: (b, 0, 0, off + p))
    return pl.pallas_call(
        functools.partial(_attn_kernel, L=L, n_res=dil),
        out_shape=[jax.ShapeDtypeStruct((B, dil, L, GROUP_W), BF16),
                   jax.ShapeDtypeStruct((B, dil, L, GROUP_W), F32)],
        grid=(B, n_pair),
        in_specs=[spec(0), spec(n_pair), spec(2 * n_pair)],
        out_specs=[spec(0), spec(0)],
        scratch_shapes=[pltpu.VMEM((3, 2 * ATT_BQ, ATT_KW), F32)],
        compiler_params=_params("parallel", "parallel"),
        name="attn",
    )(qkv, qkv, qkv)


def _mixffn_kernel(x_ref, gt1_ref, sc_ref, sh_ref, gt2_ref, g2_ref, gf_ref, mg_ref, rnn_ref,
                   o0_ref, l0_ref, o1_ref, l1_ref, o2_ref, l2_ref,
                   wr_ref, wa_ref, wo_ref, wi_ref, wf_ref, y_ref,
                   fo1_ref, fl1_ref, fo2_ref, fl2_ref, *, tm):
    n_tiles = GROUP_W // LANES

    def unfold(src_ref, dst_ref, dil):
        pitch = _fold_pitch(dil)
        for r in range(dil):
            for s in range(n_tiles):
                dst_ref[s, pl.ds(r, tm // dil, stride=pitch), :] = src_ref[0, r, :, s * LANES:(s + 1) * LANES].astype(F32)
        if pitch == dil:
            return jnp.concatenate([dst_ref[s, 0:tm, :] for s in range(n_tiles)], axis=1)
        return jnp.concatenate(
            [jnp.concatenate([dst_ref[s, pitch * m:pitch * m + dil, :] for m in range(tm // dil)], axis=0)
             for s in range(n_tiles)], axis=1)

    o1 = unfold(o1_ref, fo1_ref, ATT_GROUPS[1][1])
    l1 = unfold(l1_ref, fl1_ref, ATT_GROUPS[1][1])
    o2 = unfold(o2_ref, fo2_ref, ATT_GROUPS[2][1])
    l2 = unfold(l2_ref, fl2_ref, ATT_GROUPS[2][1])
    lses = (l0_ref[0, 0], l1, l2)
    outs = (o0_ref[0, 0].astype(F32), o1, o2)

    top = jnp.maximum(jnp.maximum(lses[0], lses[1]), lses[2])
    es = [jnp.exp(l - top) for l in lses]
    att = (es[0] * outs[0] + es[1] * outs[1] + es[2] * outs[2]) / (es[0] + es[1] + es[2])

    gate = _sigmoid(mg_ref[0].astype(F32))
    br_r = jnp.dot(rnn_ref[0], wr_ref[...], preferred_element_type=F32)
    br_a = jnp.dot(att.astype(BF16), wa_ref[...], preferred_element_type=F32)
    merged = gate[:, :D_MODEL] * br_r + gate[:, D_MODEL:] * br_a
    mix = jnp.dot(merged.astype(BF16), wo_ref[...], preferred_element_type=F32)
    x1 = x_ref[0] + gt1_ref[0] * mix

    h = _rms_mod(x1, g2_ref[...], sc_ref[0], sh_ref[0]).astype(BF16)
    ff = None
    for c in range(D_FF // FF_CHUNK):
        lo = c * FF_CHUNK
        fg = jnp.dot(h, wi_ref[:, lo:lo + FF_CHUNK], preferred_element_type=F32)
        fu = jnp.dot(h, wi_ref[:, D_FF + lo:D_FF + lo + FF_CHUNK], preferred_element_type=F32)
        act = ((fg * _sigmoid(fg)) * fu).astype(BF16)
        part = jnp.dot(act, wf_ref[lo:lo + FF_CHUNK, :], preferred_element_type=F32)
        ff = part if ff is None else ff + part
    x2 = x1 + gt2_ref[0] * ff
    y = x2 * lax.rsqrt(jnp.mean(x2 * x2, axis=-1, keepdims=True) + EPS)
    y_ref[0] = y * gf_ref[...]


def _mixffn(x, mods, g2, gf, z, rnn, attn_outs, weights, tm):
    B, S, D = x.shape
    mod_spec = pl.BlockSpec((1, 1, D), lambda b, i: (b, 0, 0))
    in_specs = [pl.BlockSpec((1, tm, D), lambda b, i: (b, i, 0))] + [mod_spec] * len(mods)
    in_specs += [
        _resident((1, D)),
        _resident((1, D)),
        pl.BlockSpec((1, tm, 2 * D_MODEL), lambda b, i: (b, i, 1)),
        pl.BlockSpec((1, tm, D_RNN), lambda b, i: (b, i, 0)),
    ]
    args = [x, *mods, g2, gf, z, rnn]
    for (o, l), (_, d) in zip(attn_outs, ATT_GROUPS):
        blk = pl.BlockSpec((1, d, tm // d, GROUP_W), lambda b, i: (b, 0, i, 0))
        in_specs += [blk, blk]
        args += [o, l]
    in_specs += [_resident(w.shape) for w in weights]
    args += list(weights)
    return pl.pallas_call(
        functools.partial(_mixffn_kernel, tm=tm),
        out_shape=jax.ShapeDtypeStruct((B, S, D), F32),
        grid=(B, S // tm),
        in_specs=in_specs,
        out_specs=pl.BlockSpec((1, tm, D), lambda b, i: (b, i, 0)),
        scratch_shapes=[pltpu.VMEM((GROUP_W // LANES, _fold_pitch(d) * (tm // d), LANES), F32)
                        for _, d in ATT_GROUPS[1:] for _ in range(2)],
        compiler_params=_params("parallel", "parallel"),
        name="mixffn",
    )(*args)


def _encode(x, mod, w, tm_in, tm_mix):
    sh1, sc1, gt1, sh2, sc2, gt2 = (mod[:, k] for k in range(N_MOD))
    outs = _inproj(x, sc1, sh1, w["norm1_g"], w["w_in"], tm_in)
    z, qkvs = outs[0], outs[1:]
    rnn = _rglru(z, w["conv_w"], w["conv_b"], w["rg_wa"], w["rg_ba"], w["rg_wx"], w["rg_bx"], w["rg_lambda"])
    attn_outs = [_attention(qkv) for qkv in qkvs]
    weights = (w["w_br_rnn"], w["w_br_attn"], w["w_out"], w["w_ffn_in"], w["w_ffn_out"])
    return _mixffn(x, (gt1, sc2, sh2, gt2), w["norm2_g"], w["final_g"], z, rnn, attn_outs, weights, tm_mix)


def kernel(x_prompt, x_sample, c_prompt, c_sample, w_ada, b_ada, norm1_g, w_in, conv_w, conv_b, rg_wa, rg_ba, rg_wx, rg_bx, rg_lambda, w_br_rnn, w_br_attn, w_out, norm2_g, w_ffn_in, w_ffn_out, final_g):
    assert w_ada.shape[0] == 1, "single layer"
    w = dict(
        norm1_g=norm1_g[0].reshape(1, -1),
        w_in=w_in[0].astype(BF16), conv_w=conv_w[0], conv_b=conv_b[0],
        rg_wa=rg_wa[0], rg_ba=rg_ba[0], rg_wx=rg_wx[0], rg_bx=rg_bx[0], rg_lambda=rg_lambda[0],
        w_br_rnn=w_br_rnn[0].astype(BF16), w_br_attn=w_br_attn[0].astype(BF16), w_out=w_out[0].astype(BF16),
        norm2_g=norm2_g[0].reshape(1, -1), w_ffn_in=w_ffn_in[0].astype(BF16),
        w_ffn_out=w_ffn_out[0].astype(BF16), final_g=final_g.reshape(1, -1),
    )
    tiles = dict(tm_in=512, tm_mix=512)
    n_prompt = c_prompt.shape[0]
    mod = _modulation(jnp.concatenate([c_prompt, c_sample], axis=0), w_ada[0], b_ada[0])
    return (_encode(x_prompt, mod[:n_prompt], w, **tiles), _encode(x_sample, mod[n_prompt:], w, **tiles))
```

```python
import functools

import jax
import jax.numpy as jnp
from jax import lax
from jax.experimental import pallas as pl
from jax.experimental.pallas import tpu as pltpu

F32 = jnp.float32
BF16 = jnp.bfloat16

D_MODEL = 1024
D_RNN = 1024
RG_BLOCKS = 16
RG_BW = D_RNN // RG_BLOCKS
RG_C = 8.0
CONV_W = 4
CONV_LEFT = 2
HEAD_DIM = 64
HEADS_PER_GROUP = 8
ATT_GROUPS = ((128, 1), (512, 4), (2048, 16))
N_GROUPS = len(ATT_GROUPS)
GROUP_W = HEADS_PER_GROUP * HEAD_DIM
ATT_W = N_GROUPS * GROUP_W
ROT_DIM = HEAD_DIM // 4
ROPE_THETA = 500000.0
D_FF = 2816
N_MOD = 6
EPS = 1e-6
NEG_INF = -1e30
LOG2E = 1.4426950408889634
LN2 = 0.6931471805599453
IN_COLS = 2 * D_RNN + 3 * ATT_W + 2 * D_MODEL
RADIUS = 64

LANES = 128
SUBLANES = 8
VMEM_LIMIT_BYTES = 56 * 1024 * 1024

COL_CHUNK = 512
RNN_CG = 128
RNN_T = 256
RNN_TB = 128
RNN_SEG_EXTRA = 4
ATT_BQ = 128
ATT_KW = ATT_BQ + 2 * RADIUS
ATT_UNROLL = 8
ATT_AHEAD = 2
FF_CHUNK = 256
HEAD_PIECES = 2


def _resident(shape):
    nd = len(shape)
    return pl.BlockSpec(shape, lambda *_: (0,) * nd, pipeline_mode=pl.Buffered(1))


def _params(*sem):
    return pltpu.CompilerParams(dimension_semantics=sem, vmem_limit_bytes=VMEM_LIMIT_BYTES)


def _rms_mod(x, g, sc, sh):
    y = x * lax.rsqrt(jnp.mean(x * x, axis=-1, keepdims=True) + EPS)
    return y * (g * (1.0 + sc)) + sh


def _sigmoid(x):
    return 1.0 / (1.0 + jnp.exp2(x * -LOG2E))


def _gelu_tanh(x):
    return 0.5 * x * (1.0 + jnp.tanh(0.7978845608028654 * (x + 0.044715 * (x * x * x))))


def _mod_kernel(c_ref, w_ref, b_ref, o_ref):
    c = c_ref[...]
    s = (c * _sigmoid(c)).astype(BF16)
    o_ref[...] = jnp.dot(s, w_ref[...].astype(BF16), preferred_element_type=F32) + b_ref[...]


def _modulation(c, w_ada, b_ada):
    B = c.shape[0]
    rows = -(-B // 16) * 16
    cp = jnp.pad(c, ((0, rows - B), (0, 0)))
    out = pl.pallas_call(
        _mod_kernel,
        out_shape=jax.ShapeDtypeStruct((rows, N_MOD * D_MODEL), F32),
        grid=(N_MOD,),
        in_specs=[
            pl.BlockSpec((rows, D_MODEL), lambda j: (0, 0)),
            pl.BlockSpec((D_MODEL, D_MODEL), lambda j: (0, j)),
            pl.BlockSpec((1, D_MODEL), lambda j: (0, j)),
        ],
        out_specs=pl.BlockSpec((rows, D_MODEL), lambda j: (0, j)),
        compiler_params=_params("parallel"),
        name="mod",
    )(cp, w_ada, b_ada.reshape(1, -1))
    return out[:B].reshape(B, N_MOD, 1, D_MODEL)


def _fold_pitch(dil):
    return dil if dil % (2 * SUBLANES) else dil + SUBLANES


def _inproj_kernel(x_ref, sc_ref, sh_ref, g_ref, w_ref, cos_ref, sa_ref, sb_ref,
                   z_ref, q0_ref, q1_ref, q2_ref, fold_ref, *, tm):
    rows = tm // HEAD_PIECES
    halves = [_rms_mod(x_ref[0, r0:r0 + rows, :], g_ref[...], sc_ref[0], sh_ref[0]).astype(BF16)
              for r0 in range(0, tm, rows)]
    h = jnp.concatenate(halves, axis=0)
    first_proj = []

    def proj(j):
        w = w_ref[:, j * COL_CHUNK:(j + 1) * COL_CHUNK]
        if not first_proj:
            first_proj.append(j)
            return jnp.concatenate([jnp.dot(part, w, preferred_element_type=F32) for part in halves], axis=0)
        return jnp.dot(h, w, preferred_element_type=F32)

    def rope(v):
        parts = []
        for s in range(COL_CHUNK // LANES):
            p = v[:, s * LANES:(s + 1) * LANES]
            up = pltpu.roll(p, LANES - ROT_DIM // 2, axis=1)
            dn = pltpu.roll(p, ROT_DIM // 2, axis=1)
            parts.append(p * cos_ref[...] + up * sa_ref[...] + dn * sb_ref[...])
        return jnp.concatenate(parts, axis=1)

    qkv_refs = (q0_ref, q1_ref, q2_ref)

    def emit_folded(val, g, off):
        dil = ATT_GROUPS[g][1]
        ref = qkv_refs[g]
        if dil == 1:
            ref[0, 0, :, off:off + COL_CHUNK] = val.astype(BF16)
            return
        pitch = _fold_pitch(dil)
        for s in range(COL_CHUNK // LANES):
            slab = val[:, s * LANES:(s + 1) * LANES]
            if pitch == dil:
                fold_ref[s, 0:tm, :] = slab
            else:
                for m in range(tm // dil):
                    fold_ref[s, pitch * m:pitch * m + dil, :] = slab[dil * m:dil * (m + 1)]
        for r in range(dil):
            for s in range(COL_CHUNK // LANES):
                lo = off + s * LANES
                ref[0, r, :, lo:lo + LANES] = fold_ref[s, pl.ds(r, tm // dil, stride=pitch), :].astype(BF16)

    def emit_z(j_w, j_z, fn):
        z_ref[0, :, j_z * COL_CHUNK:(j_z + 1) * COL_CHUNK] = fn(proj(j_w)).astype(BF16)

    n_x = D_RNN // COL_CHUNK
    n_rnn = 2 * n_x
    n_gate0 = n_rnn + 3 * N_GROUPS
    for j in range(2 * D_MODEL // COL_CHUNK):
        emit_z(n_gate0 + j, n_rnn + j, lambda v: v)
        if j < n_x:
            emit_z(n_x + j, n_x + j, _gelu_tanh)
    for g in range(N_GROUPS):
        emit_folded(rope(proj(n_rnn + g)) * (HEAD_DIM ** -0.5 * LOG2E), g, 0)
        emit_folded(rope(proj(n_rnn + N_GROUPS + g)), g, GROUP_W)
        emit_folded(proj(n_rnn + 2 * N_GROUPS + g), g, 2 * GROUP_W)
    for j in range(n_x):
        emit_z(j, j, lambda v: v)


def _rope_tables(S):
    half = ROT_DIM // 2
    inv = ROPE_THETA ** (-(jnp.arange(0, ROT_DIM, 2, dtype=F32) / ROT_DIM))
    ang = jnp.arange(S, dtype=F32)[:, None] * inv[None, :]
    cos, sin = jnp.cos(ang), jnp.sin(ang)
    zeros = jnp.zeros((S, HEAD_DIM - ROT_DIM), F32)
    z8 = jnp.zeros((S, half), F32)
    c = jnp.concatenate([cos, cos, zeros + 1.0], axis=1)
    sa = jnp.concatenate([-sin, z8, zeros], axis=1)
    sb = jnp.concatenate([z8, sin, zeros], axis=1)
    rep = LANES // HEAD_DIM
    return tuple(jnp.tile(t, (1, rep)) for t in (c, sa, sb))


def _inproj(x, sc, sh, g, w_in, tm):
    B, S, D = x.shape
    cos, sa, sb = _rope_tables(S)
    dils = [d for _, d in ATT_GROUPS]
    row = lambda b, i: (b, 0, 0)
    out_shape = [jax.ShapeDtypeStruct((B, S, 2 * D_RNN + 2 * D_MODEL), BF16)]
    out_specs = [pl.BlockSpec((1, tm, 2 * D_RNN + 2 * D_MODEL), lambda b, i: (b, i, 0))]
    for d in dils:
        out_shape.append(jax.ShapeDtypeStruct((B, d, S // d, 3 * GROUP_W), BF16))
        out_specs.append(pl.BlockSpec((1, d, tm // d, 3 * GROUP_W), lambda b, i: (b, 0, i, 0)))
    tab = pl.BlockSpec((tm, LANES), lambda b, i: (i, 0))
    return pl.pallas_call(
        functools.partial(_inproj_kernel, tm=tm),
        out_shape=out_shape,
        grid=(B, S // tm),
        in_specs=[
            pl.BlockSpec((1, tm, D), lambda b, i: (b, i, 0)),
            pl.BlockSpec((1, 1, D), row),
            pl.BlockSpec((1, 1, D), row),
            _resident((1, D)),
            _resident((D, IN_COLS)),
            tab, tab, tab,
        ],
        out_specs=out_specs,
        scratch_shapes=[pltpu.VMEM((COL_CHUNK // LANES, max(_fold_pitch(d) * (tm // d) for d in dils), LANES), F32)],
        compiler_params=_params("parallel", "parallel"),
        name="inproj",
    )(x, sc, sh, g, w_in, cos, sa, sb)


def _rnn_seg_len(S):
    assert S % (8 * SUBLANES) == 0
    return S // SUBLANES + RNN_SEG_EXTRA


def _rglru_kernel(x_ref, gate_ref, cw_ref, cb_ref, wg_ref, bg_ref, lam_ref, o_ref,
                  xnat_ref, hnat_ref, hloc_ref, acum_ref, xc_ref, *, S):
    C = RNN_CG
    T = RNN_T
    TB = RNN_TB
    PAD = SUBLANES
    seg = _rnn_seg_len(S)
    n_main = (seg - RNN_SEG_EXTRA) // TB
    rows = xnat_ref.shape[0]

    xnat_ref[0:PAD, :] = jnp.zeros((PAD, C), F32)
    xnat_ref[PAD + S:rows, :] = jnp.zeros((rows - PAD - S, C), F32)

    def fill(c, carry):
        r0 = pl.multiple_of(c * T, T)
        xnat_ref[pl.ds(r0 + PAD, T), :] = x_ref[0, pl.ds(r0, T), :].astype(F32)
        return carry

    lax.fori_loop(0, S // T, fill, 0)

    neg_lam = -lam_ref[0]
    softplus = jnp.maximum(neg_lam, 0.0) + jnp.log1p(jnp.exp(-jnp.abs(neg_lam)))
    half_coef2 = (-0.5 * RG_C * LOG2E) * softplus
    half_bias = 0.5 * bg_ref[0]
    cwb = [jnp.broadcast_to(cw_ref[k:k + 1, :], (SUBLANES, C)) for k in range(CONV_W)]
    cbb = jnp.broadcast_to(cb_ref[...], (SUBLANES, C))
    sub = lax.broadcasted_iota(jnp.int32, (SUBLANES, C), 0)
    steps_left = S - sub * seg

    def step_rows(t):
        return (pl.ds(PAD + t, SUBLANES, stride=seg), slice(None))

    def block_gates(t0, n, direction, conv):
        rows = pl.ds(pl.multiple_of(t0 * SUBLANES, SUBLANES), n * SUBLANES)
        if conv == "reuse":
            xc = xc_ref[rows, :]
        else:
            taps = [xnat_ref[step_rows(t0 + m - CONV_LEFT)] for m in range(n + CONV_W - 1)]
            xcs = []
            for j in range(n):
                acc = cbb + taps[j] * cwb[0]
                for k in range(1, CONV_W):
                    acc = acc + taps[j + k] * cwb[k]
                xcs.append(acc)
            xc = jnp.concatenate(xcs, axis=0)
            if conv == "keep":
                xc_ref[rows, :] = xc
        lo = direction * 2 * C
        gz = jnp.dot(xc.astype(BF16), wg_ref[0, :, lo:lo + 2 * C], preferred_element_type=F32)
        t = jnp.tanh(0.5 * gz + half_bias[:, lo:lo + 2 * C])
        half_coef = half_coef2[direction:direction + 1]
        a = jnp.exp2(half_coef * t[:, :C] + half_coef)
        y = 1.0 - a * a
        root = jnp.where(y > 0.0, y * lax.rsqrt(y), 0.0)
        half_xc = 0.5 * xc
        u = root * (half_xc * t[:, C:] + half_xc)
        return a, u

    def scan_block(t0, n, direction, carry, past_end=False, conv="own"):
        h, acc = carry
        a, u = block_gates(t0, n, direction, conv)
        order = range(n) if direction == 0 else range(n - 1, -1, -1)
        for j in order:
            aj = a[j * SUBLANES:(j + 1) * SUBLANES]
            uj = u[j * SUBLANES:(j + 1) * SUBLANES]
            if direction == 1 and past_end:
                uj = jnp.where(t0 + j < steps_left, uj, 0.0)
            h = aj * h + uj
            acc = aj * acc
            row = pl.multiple_of((t0 + j) * SUBLANES, SUBLANES)
            hloc_ref[direction, pl.ds(row, SUBLANES), :] = h
            acum_ref[direction, pl.ds(row, SUBLANES), :] = acc
        return h, acc

    def entry_states(h_tot, a_tot, direction):
        c = jnp.zeros((1, C), F32)
        out = jnp.zeros((SUBLANES, C), F32)
        order = range(SUBLANES) if direction == 0 else range(SUBLANES - 1, -1, -1)
        for s in order:
            out = jnp.where(sub == s, c, out)
            c = a_tot[s:s + 1] * c + h_tot[s:s + 1]
        return out

    def fix_block(t0, n, entries):
        for j in range(n):
            row = pl.multiple_of((t0 + j) * SUBLANES, SUBLANES)
            h = None
            for direction in (0, 1):
                part = (hloc_ref[direction, pl.ds(row, SUBLANES), :]
                        + acum_ref[direction, pl.ds(row, SUBLANES), :] * entries[direction])
                h = part if h is None else h + part
            hnat_ref[step_rows(t0 + j)] = h

    tail0 = n_main * TB
    init = (jnp.zeros((SUBLANES, C), F32), jnp.ones((SUBLANES, C), F32))

    def block_start(b):
        return b * TB if isinstance(b, int) else pl.multiple_of(b * TB, TB)

    def main(b, carry, past_end=False, conv="keep"):
        fwd = scan_block(block_start(b), TB, 0, carry[0], conv=conv)
        bwd = scan_block(block_start(n_main - 1 - b), TB, 1, carry[1], past_end, conv=conv)
        return fwd, bwd

    assert TB >= (SUBLANES - 1) * RNN_SEG_EXTRA and n_main >= 2 and n_main % 2 == 0
    bwd_tail = scan_block(tail0, RNN_SEG_EXTRA, 1, init, past_end=True)
    carry = main(0, (init, bwd_tail), past_end=True)
    carry = lax.fori_loop(1, n_main // 2, main, carry)
    fwd_tot, bwd_tot = lax.fori_loop(n_main // 2, n_main, functools.partial(main, conv="reuse"), carry)
    fwd_tot = scan_block(tail0, RNN_SEG_EXTRA, 0, fwd_tot)
    entries = (entry_states(*fwd_tot, 0), entry_states(*bwd_tot, 1))

    def fix(b, carry):
        fix_block(pl.multiple_of(b * TB, TB), TB, entries)
        return carry

    lax.fori_loop(0, n_main, fix, 0)
    fix_block(tail0, RNN_SEG_EXTRA, entries)

    def finish(c, carry):
        r0 = pl.multiple_of(c * T, T)
        gate = gate_ref[0, pl.ds(r0, T), :].astype(F32)
        o_ref[0, pl.ds(r0, T), :] = (hnat_ref[pl.ds(r0 + PAD, T), :] * gate).astype(BF16)
        return carry

    lax.fori_loop(0, S // T, finish, 0)


def _block_diag_pairs(w):
    per = RNN_CG // RG_BW
    w = w.reshape(RG_BLOCKS // per, per, RG_BW, RG_BW)
    rows = []
    for p in range(per):
        cols = [w[:, p] if q == p else jnp.zeros_like(w[:, p]) for q in range(per)]
        rows.append(jnp.concatenate(cols, axis=-1))
    return jnp.concatenate(rows, axis=1)


def _rglru(z, conv_w, conv_b, rg_wa, rg_ba, rg_wx, rg_bx, rg_lambda):
    B, S, _ = z.shape
    C = RNN_CG
    n_grp = D_RNN // C
    wg = jnp.concatenate([_block_diag_pairs(rg_wa[0]), _block_diag_pairs(rg_wx[0]),
                          _block_diag_pairs(rg_wa[1]), _block_diag_pairs(rg_wx[1])], axis=-1).astype(BF16)
    bg = jnp.concatenate([rg_ba[0].reshape(n_grp, 1, C), rg_bx[0].reshape(n_grp, 1, C),
                          rg_ba[1].reshape(n_grp, 1, C), rg_bx[1].reshape(n_grp, 1, C)], axis=-1)
    lam = rg_lambda.reshape(2, n_grp, C).transpose(1, 0, 2)
    step_rows = SUBLANES * _rnn_seg_len(S)
    nat_rows = step_rows + 2 * SUBLANES
    return pl.pallas_call(
        functools.partial(_rglru_kernel, S=S),
        out_shape=jax.ShapeDtypeStruct((B, S, D_RNN), BF16),
        grid=(B, n_grp),
        in_specs=[
            pl.BlockSpec((1, S, C), lambda b, c: (b, 0, c)),
            pl.BlockSpec((1, S, C), lambda b, c: (b, 0, n_grp + c)),
            pl.BlockSpec((CONV_W, C), lambda b, c: (0, c)),
            pl.BlockSpec((1, C), lambda b, c: (0, c)),
            pl.BlockSpec((1, C, 4 * C), lambda b, c: (c, 0, 0)),
            pl.BlockSpec((1, 1, 4 * C), lambda b, c: (c, 0, 0)),
            pl.BlockSpec((1, 2, C), lambda b, c: (c, 0, 0)),
        ],
        out_specs=pl.BlockSpec((1, S, C), lambda b, c: (b, 0, c)),
        scratch_shapes=[pltpu.VMEM((nat_rows, C), F32), pltpu.VMEM((nat_rows, C), F32),
                        pltpu.VMEM((2, step_rows, C), F32), pltpu.VMEM((2, step_rows, C), F32),
                        pltpu.VMEM((step_rows, C), F32)],
        compiler_params=_params("parallel", "parallel"),
        name="rglru",
    )(z, z, conv_w, conv_b.reshape(1, -1), wg, bg, lam)


def _attn_kernel(q_ref, k_ref, v_ref, o_ref, l_ref, cap_ref, *, L, n_res):
    BQ, KW = ATT_BQ, ATT_KW
    lane = lax.broadcasted_iota(jnp.int32, (BQ, LANES), 1)
    first = lane < HEAD_DIM
    first_kw = lax.broadcasted_iota(jnp.int32, (KW, LANES), 1) < HEAD_DIM
    rel = lax.broadcasted_iota(jnp.int32, (BQ, KW), 0) - lax.broadcasted_iota(jnp.int32, (BQ, KW), 1)
    n_blocks = L // BQ
    total = n_res * n_blocks
    assert n_blocks & (n_blocks - 1) == 0 and total % ATT_UNROLL == 0

    for j in range(3):
        cap = jnp.where(jnp.abs(rel + j * RADIUS) <= RADIUS, jnp.inf, NEG_INF).astype(F32)
        cap_ref[j, 0:BQ, :] = cap
        cap_ref[j, BQ:2 * BQ, :] = cap

    def scores(idx):
        r = lax.shift_right_logical(idx, n_blocks.bit_length() - 1)
        q0 = pl.multiple_of(jnp.bitwise_and(idx, n_blocks - 1) * BQ, BQ)
        k0 = pl.multiple_of(jnp.clip(q0 - RADIUS, 0, L - KW), RADIUS)
        q = q_ref[0, r, pl.ds(q0, BQ), :]
        k = k_ref[0, r, pl.ds(k0, KW), :]
        zero = jnp.zeros_like(q)
        q2 = jnp.concatenate([jnp.where(first, q, zero), jnp.where(first, zero, q)], axis=0)
        s = lax.dot_general(q2, k, (((1,), (1,)), ((), ())), preferred_element_type=F32)
        return r, q0, k0, s

    def finish(r, q0, k0, s):
        cap = cap_ref[lax.shift_right_logical(q0 - k0, RADIUS.bit_length() - 1)]
        v = v_ref[0, r, pl.ds(k0, KW), :]
        s = jnp.minimum(s, cap)
        m = jnp.max(s, axis=-1, keepdims=True)
        p = jnp.exp2(s - m).astype(BF16)
        one = jnp.ones_like(v)
        pv_a = jnp.dot(p[:BQ], jnp.where(first_kw, v, one), preferred_element_type=F32)
        pv_b = jnp.dot(p[BQ:], jnp.where(first_kw, one, v), preferred_element_type=F32)
        num = jnp.where(first, pv_a, pv_b)
        den = pltpu.roll(jnp.where(first, pv_b, pv_a), HEAD_DIM, axis=1)
        top = jnp.where(first, m[:BQ], m[BQ:])
        o_ref[0, r, pl.ds(q0, BQ), :] = (num / den).astype(BF16)
        l_ref[0, r, pl.ds(q0, BQ), :] = top * LN2 + jnp.log(den)

    def body(it, carry):
        pending = [scores(it * ATT_UNROLL + j) for j in range(ATT_AHEAD)]
        for j in range(ATT_UNROLL):
            if j + ATT_AHEAD < ATT_UNROLL:
                pending.append(scores(it * ATT_UNROLL + j + ATT_AHEAD))
            finish(*pending.pop(0))
        return carry

    lax.fori_loop(0, total // ATT_UNROLL, body, 0)


def _attention(qkv):
    B, dil, L, _ = qkv.shape
    n_pair = GROUP_W // LANES
    spec = lambda off: pl.BlockSpec((1, dil, L, LANES), lambda b, p: (b, 0, 0, off + p))
    return pl.pallas_call(
        functools.partial(_attn_kernel, L=L, n_res=dil),
        out_shape=[jax.ShapeDtypeStruct((B, dil, L, GROUP_W), BF16),
                   jax.ShapeDtypeStruct((B, dil, L, GROUP_W), F32)],
        grid=(B, n_pair),
        in_specs=[spec(0), spec(n_pair), spec(2 * n_pair)],
        out_specs=[spec(0), spec(0)],
        scratch_shapes=[pltpu.VMEM((3, 2 * ATT_BQ, ATT_KW), F32)],
        compiler_params=_params("parallel", "parallel"),
        name="attn",
    )(qkv, qkv, qkv)


def _mixffn_kernel(x_ref, gt1_ref, sc_ref, sh_ref, gt2_ref, g2_ref, gf_ref, mg_ref, rnn_ref,
                   o0_ref, l0_ref, o1_ref, l1_ref, o2_ref, l2_ref,
                   wr_ref, wa_ref, wo_ref, wi_ref, wf_ref, y_ref,
                   fo1_ref, fl1_ref, fo2_ref, fl2_ref, *, tm):
    n_tiles = GROUP_W // LANES

    def unfold(src_ref, dst_ref, dil):
        pitch = _fold_pitch(dil)
        for r in range(dil):
            for s in range(n_tiles):
                dst_ref[s, pl.ds(r, tm // dil, stride=pitch), :] = src_ref[0, r, :, s * LANES:(s + 1) * LANES].astype(F32)
        if pitch == dil:
            return jnp.concatenate([dst_ref[s, 0:tm, :] for s in range(n_tiles)], axis=1)
        return jnp.concatenate(
            [jnp.concatenate([dst_ref[s, pitch * m:pitch * m + dil, :] for m in range(tm // dil)], axis=0)
             for s in range(n_tiles)], axis=1)

    o1 = unfold(o1_ref, fo1_ref, ATT_GROUPS[1][1])
    l1 = unfold(l1_ref, fl1_ref, ATT_GROUPS[1][1])
    o2 = unfold(o2_ref, fo2_ref, ATT_GROUPS[2][1])
    l2 = unfold(l2_ref, fl2_ref, ATT_GROUPS[2][1])
    lses = (l0_ref[0, 0], l1, l2)
    outs = (o0_ref[0, 0].astype(F32), o1, o2)

    top = jnp.maximum(jnp.maximum(lses[0], lses[1]), lses[2])
    es = [jnp.exp(l - top) for l in lses]
    att = (es[0] * outs[0] + es[1] * outs[1] + es[2] * outs[2]) / (es[0] + es[1] + es[2])

    br_a = jnp.dot(att.astype(BF16), wa_ref[...], preferred_element_type=F32)
    br_r = jnp.dot(rnn_ref[0], wr_ref[...], preferred_element_type=F32)
    gate = _sigmoid(mg_ref[0].astype(F32))
    merged = gate[:, :D_MODEL] * br_r + gate[:, D_MODEL:] * br_a
    rows = tm // HEAD_PIECES
    pieces = [slice(r0, r0 + rows) for r0 in range(0, tm, rows)]
    merged = merged.astype(BF16)
    x1s, hs = [], []
    for p in pieces:
        mix = jnp.dot(merged[p], wo_ref[...], preferred_element_type=F32)
        x1s.append(x_ref[0, p, :] + gt1_ref[0] * mix)
        hs.append(_rms_mod(x1s[-1], g2_ref[...], sc_ref[0], sh_ref[0]).astype(BF16))
    h = jnp.concatenate(hs, axis=0)

    def up(lhs, c):
        lo = c * FF_CHUNK
        fg = jnp.dot(lhs, wi_ref[:, lo:lo + FF_CHUNK], preferred_element_type=F32)
        fu = jnp.dot(lhs, wi_ref[:, D_FF + lo:D_FF + lo + FF_CHUNK], preferred_element_type=F32)
        return ((fg * _sigmoid(fg)) * fu).astype(BF16)

    n_chunks = D_FF // FF_CHUNK
    ff = None
    for c in range(n_chunks - 1):
        act = jnp.concatenate([up(part, c) for part in hs], axis=0) if c == 0 else up(h, c)
        part = jnp.dot(act, wf_ref[c * FF_CHUNK:(c + 1) * FF_CHUNK, :], preferred_element_type=F32)
        ff = part if ff is None else ff + part
    act = up(h, n_chunks - 1)
    for p, x1 in zip(pieces, x1s):
        ffp = ff[p] + jnp.dot(act[p], wf_ref[(n_chunks - 1) * FF_CHUNK:, :], preferred_element_type=F32)
        x2 = x1 + gt2_ref[0] * ffp
        y = x2 * lax.rsqrt(jnp.mean(x2 * x2, axis=-1, keepdims=True) + EPS)
        y_ref[0, p, :] = y * gf_ref[...]


def _mixffn(x, mods, g2, gf, z, rnn, attn_outs, weights, tm):
    B, S, D = x.shape
    mod_spec = pl.BlockSpec((1, 1, D), lambda b, i: (b, 0, 0))
    in_specs = [pl.BlockSpec((1, tm, D), lambda b, i: (b, i, 0))] + [mod_spec] * len(mods)
    in_specs += [
        _resident((1, D)),
        _resident((1, D)),
        pl.BlockSpec((1, tm, 2 * D_MODEL), lambda b, i: (b, i, 1)),
        pl.BlockSpec((1, tm, D_RNN), lambda b, i: (b, i, 0)),
    ]
    args = [x, *mods, g2, gf, z, rnn]
    for (o, l), (_, d) in zip(attn_outs, ATT_GROUPS):
        blk = pl.BlockSpec((1, d, tm // d, GROUP_W), lambda b, i: (b, 0, i, 0))
        in_specs += [blk, blk]
        args += [o, l]
    in_specs += [_resident(w.shape) for w in weights]
    args += list(weights)
    return pl.pallas_call(
        functools.partial(_mixffn_kernel, tm=tm),
        out_shape=jax.ShapeDtypeStruct((B, S, D), F32),
        grid=(B, S // tm),
        in_specs=in_specs,
        out_specs=pl.BlockSpec((1, tm, D), lambda b, i: (b, i, 0)),
        scratch_shapes=[pltpu.VMEM((GROUP_W // LANES, _fold_pitch(d) * (tm // d), LANES), F32)
                        for _, d in ATT_GROUPS[1:] for _ in range(2)],
        compiler_params=_params("parallel", "parallel"),
        name="mixffn",
    )(*args)


def _encode(x, mod, w, tm_in, tm_mix):
    sh1, sc1, gt1, sh2, sc2, gt2 = (mod[:, k] for k in range(N_MOD))
    outs = _inproj(x, sc1, sh1, w["norm1_g"], w["w_in"], tm_in)
    z, qkvs = outs[0], outs[1:]
    rnn = _rglru(z, w["conv_w"], w["conv_b"], w["rg_wa"], w["rg_ba"], w["rg_wx"], w["rg_bx"], w["rg_lambda"])
    attn_outs = [_attention(qkv) for qkv in qkvs]
    weights = (w["w_br_rnn"], w["w_br_attn"], w["w_out"], w["w_ffn_in"], w["w_ffn_out"])
    return _mixffn(x, (gt1, sc2, sh2, gt2), w["norm2_g"], w["final_g"], z, rnn, attn_outs, weights, tm_mix)


def kernel(x_prompt, x_sample, c_prompt, c_sample, w_ada, b_ada, norm1_g, w_in, conv_w, conv_b, rg_wa, rg_ba, rg_wx, rg_bx, rg_lambda, w_br_rnn, w_br_attn, w_out, norm2_g, w_ffn_in, w_ffn_out, final_g):
    assert w_ada.shape[0] == 1, "single layer"
    w = dict(
        norm1_g=norm1_g[0].reshape(1, -1),
        w_in=w_in[0].astype(BF16), conv_w=conv_w[0], conv_b=conv_b[0],
        rg_wa=rg_wa[0], rg_ba=rg_ba[0], rg_wx=rg_wx[0], rg_bx=rg_bx[0], rg_lambda=rg_lambda[0],
        w_br_rnn=w_br_rnn[0].astype(BF16), w_br_attn=w_br_attn[0].astype(BF16), w_out=w_out[0].astype(BF16),
        norm2_g=norm2_g[0].reshape(1, -1), w_ffn_in=w_ffn_in[0].astype(BF16),
        w_ffn_out=w_ffn_out[0].astype(BF16), final_g=final_g.reshape(1, -1),
    )
    tiles = dict(tm_in=512, tm_mix=512)
    n_prompt = c_prompt.shape[0]
    mod = _modulation(jnp.concatenate([c_prompt, c_sample], axis=0), w_ada[0], b_ada[0])
    return (_encode(x_prompt, mod[:n_prompt], w, **tiles), _encode(x_sample, mod[n_prompt:], w, **tiles))
```

```python
import functools

import jax
import jax.numpy as jnp
from jax import lax
from jax.experimental import pallas as pl
from jax.experimental.pallas import tpu as pltpu

F32 = jnp.float32
BF16 = jnp.bfloat16

D_MODEL = 1024
D_RNN = 1024
RG_BLOCKS = 16
RG_BW = D_RNN // RG_BLOCKS
RG_C = 8.0
CONV_W = 4
CONV_LEFT = 2
HEAD_DIM = 64
HEADS_PER_GROUP = 8
ATT_GROUPS = ((128, 1), (512, 4), (2048, 16))
N_GROUPS = len(ATT_GROUPS)
GROUP_W = HEADS_PER_GROUP * HEAD_DIM
ATT_W = N_GROUPS * GROUP_W
ROT_DIM = HEAD_DIM // 4
ROPE_THETA = 500000.0
D_FF = 2816
N_MOD = 6
EPS = 1e-6
NEG_INF = -1e30
LOG2E = 1.4426950408889634
LN2 = 0.6931471805599453
IN_COLS = 2 * D_RNN + 3 * ATT_W + 2 * D_MODEL
RADIUS = 64

LANES = 128
SUBLANES = 8
BF16_ROWS = 2 * SUBLANES
VMEM_LIMIT_BYTES = 56 * 1024 * 1024

COL_CHUNK = 512
RNN_CG = 128
RNN_T = 256
RNN_TB = 128
RNN_SEG_EXTRA = 4
ATT_BQ = 128
ATT_KW = ATT_BQ + 2 * RADIUS
ATT_UNROLL = 8
ATT_AHEAD = 2
FF_CHUNK = 256
HEAD_PIECES = 2


def _resident(shape):
    nd = len(shape)
    return pl.BlockSpec(shape, lambda *_: (0,) * nd, pipeline_mode=pl.Buffered(1))


def _params(*sem):
    return pltpu.CompilerParams(dimension_semantics=sem, vmem_limit_bytes=VMEM_LIMIT_BYTES)


def _rms_mod(x, g, sc, sh):
    y = x * lax.rsqrt(jnp.mean(x * x, axis=-1, keepdims=True) + EPS)
    return y * (g * (1.0 + sc)) + sh


def _sigmoid(x):
    return 1.0 / (1.0 + jnp.exp2(x * -LOG2E))


def _gelu_tanh(x):
    return 0.5 * x * (1.0 + jnp.tanh(0.7978845608028654 * (x + 0.044715 * (x * x * x))))


def _mod_kernel(c_ref, w_ref, b_ref, o_ref):
    c = c_ref[...]
    s = (c * _sigmoid(c)).astype(BF16)
    o_ref[...] = jnp.dot(s, w_ref[...].astype(BF16), preferred_element_type=F32) + b_ref[...]


def _modulation(c, w_ada, b_ada):
    B = c.shape[0]
    rows = -(-B // BF16_ROWS) * BF16_ROWS
    cp = jnp.pad(c, ((0, rows - B), (0, 0)))
    out = pl.pallas_call(
        _mod_kernel,
        out_shape=jax.ShapeDtypeStruct((rows, N_MOD * D_MODEL), F32),
        grid=(N_MOD,),
        in_specs=[
            pl.BlockSpec((rows, D_MODEL), lambda j: (0, 0)),
            pl.BlockSpec((D_MODEL, D_MODEL), lambda j: (0, j)),
            pl.BlockSpec((1, D_MODEL), lambda j: (0, j)),
        ],
        out_specs=pl.BlockSpec((rows, D_MODEL), lambda j: (0, j)),
        compiler_params=_params("parallel"),
        name="mod",
    )(cp, w_ada, b_ada.reshape(1, -1))
    return out[:B].reshape(B, N_MOD, 1, D_MODEL)


def _fold_pitch(dil):
    return dil if dil % (2 * SUBLANES) else dil + SUBLANES


def _inproj_kernel(x_ref, sc_ref, sh_ref, g_ref, w_ref, cos_ref, sa_ref, sb_ref,
                   z_ref, q0_ref, q1_ref, q2_ref, fold_ref, *, tm):
    rows = tm // HEAD_PIECES
    pieces = [_rms_mod(x_ref[0, r0:r0 + rows, :], g_ref[...], sc_ref[0], sh_ref[0]).astype(BF16)
              for r0 in range(0, tm, rows)]
    h = jnp.concatenate(pieces, axis=0)
    issued = []

    def proj(j):
        w = w_ref[:, j * COL_CHUNK:(j + 1) * COL_CHUNK]
        issued.append(j)
        if len(issued) == 1:
            return jnp.concatenate([jnp.dot(part, w, preferred_element_type=F32) for part in pieces], axis=0)
        return jnp.dot(h, w, preferred_element_type=F32)

    def rope(v):
        parts = []
        for s in range(COL_CHUNK // LANES):
            p = v[:, s * LANES:(s + 1) * LANES]
            up = pltpu.roll(p, LANES - ROT_DIM // 2, axis=1)
            dn = pltpu.roll(p, ROT_DIM // 2, axis=1)
            parts.append(p * cos_ref[...] + up * sa_ref[...] + dn * sb_ref[...])
        return jnp.concatenate(parts, axis=1)

    qkv_refs = (q0_ref, q1_ref, q2_ref)

    def emit_folded(val, g, off):
        dil = ATT_GROUPS[g][1]
        ref = qkv_refs[g]
        if dil == 1:
            ref[0, 0, :, off:off + COL_CHUNK] = val.astype(BF16)
            return
        pitch = _fold_pitch(dil)
        for s in range(COL_CHUNK // LANES):
            slab = val[:, s * LANES:(s + 1) * LANES]
            if pitch == dil:
                fold_ref[s, 0:tm, :] = slab
            else:
                for m in range(tm // dil):
                    fold_ref[s, pitch * m:pitch * m + dil, :] = slab[dil * m:dil * (m + 1)]
        for r in range(dil):
            for s in range(COL_CHUNK // LANES):
                lo = off + s * LANES
                ref[0, r, :, lo:lo + LANES] = fold_ref[s, pl.ds(r, tm // dil, stride=pitch), :].astype(BF16)

    def emit_z(j_w, j_z, fn):
        z_ref[0, :, j_z * COL_CHUNK:(j_z + 1) * COL_CHUNK] = fn(proj(j_w)).astype(BF16)

    n_x = D_RNN // COL_CHUNK
    n_rnn = 2 * n_x
    n_gate0 = n_rnn + 3 * N_GROUPS
    for j in range(2 * D_MODEL // COL_CHUNK):
        emit_z(n_gate0 + j, n_rnn + j, lambda v: v)
        if j < n_x:
            emit_z(n_x + j, n_x + j, _gelu_tanh)
    for g in range(N_GROUPS):
        emit_folded(rope(proj(n_rnn + g)) * (HEAD_DIM ** -0.5 * LOG2E), g, 0)
        emit_folded(rope(proj(n_rnn + N_GROUPS + g)), g, GROUP_W)
        emit_folded(proj(n_rnn + 2 * N_GROUPS + g), g, 2 * GROUP_W)
    for j in range(n_x):
        emit_z(j, j, lambda v: v)


def _rope_tables(S):
    half = ROT_DIM // 2
    inv = ROPE_THETA ** (-(jnp.arange(0, ROT_DIM, 2, dtype=F32) / ROT_DIM))
    ang = jnp.arange(S, dtype=F32)[:, None] * inv[None, :]
    cos, sin = jnp.cos(ang), jnp.sin(ang)
    zeros = jnp.zeros((S, HEAD_DIM - ROT_DIM), F32)
    z8 = jnp.zeros((S, half), F32)
    c = jnp.concatenate([cos, cos, zeros + 1.0], axis=1)
    sa = jnp.concatenate([-sin, z8, zeros], axis=1)
    sb = jnp.concatenate([z8, sin, zeros], axis=1)
    rep = LANES // HEAD_DIM
    return tuple(jnp.tile(t, (1, rep)) for t in (c, sa, sb))


def _inproj(x, sc, sh, g, w_in, tm):
    B, S, D = x.shape
    cos, sa, sb = _rope_tables(S)
    dils = [d for _, d in ATT_GROUPS]
    assert S % tm == 0 and all(tm % (d * BF16_ROWS) == 0 for d in dils) and tm % (HEAD_PIECES * BF16_ROWS) == 0
    row = lambda b, i: (b, 0, 0)
    out_shape = [jax.ShapeDtypeStruct((B, S, 2 * D_RNN + 2 * D_MODEL), BF16)]
    out_specs = [pl.BlockSpec((1, tm, 2 * D_RNN + 2 * D_MODEL), lambda b, i: (b, i, 0))]
    for d in dils:
        out_shape.append(jax.ShapeDtypeStruct((B, d, S // d, 3 * GROUP_W), BF16))
        out_specs.append(pl.BlockSpec((1, d, tm // d, 3 * GROUP_W), lambda b, i: (b, 0, i, 0)))
    tab = pl.BlockSpec((tm, LANES), lambda b, i: (i, 0))
    return pl.pallas_call(
        functools.partial(_inproj_kernel, tm=tm),
        out_shape=out_shape,
        grid=(B, S // tm),
        in_specs=[
            pl.BlockSpec((1, tm, D), lambda b, i: (b, i, 0)),
            pl.BlockSpec((1, 1, D), row),
            pl.BlockSpec((1, 1, D), row),
            _resident((1, D)),
            _resident((D, IN_COLS)),
            tab, tab, tab,
        ],
        out_specs=out_specs,
        scratch_shapes=[pltpu.VMEM((COL_CHUNK // LANES, max(_fold_pitch(d) * (tm // d) for d in dils), LANES), F32)],
        compiler_params=_params("parallel", "parallel"),
        name="inproj",
    )(x, sc, sh, g, w_in, cos, sa, sb)


def _rnn_seg_len(S):
    assert S % (8 * SUBLANES) == 0
    return S // SUBLANES + RNN_SEG_EXTRA


def _rglru_kernel(x_ref, gate_ref, cw_ref, cb_ref, wg_ref, bg_ref, lam_ref, o_ref,
                  xnat_ref, hnat_ref, hloc_ref, acum_ref, xc_ref, *, S):
    C = RNN_CG
    T = RNN_T
    TB = RNN_TB
    PAD = SUBLANES
    seg = _rnn_seg_len(S)
    n_main = (seg - RNN_SEG_EXTRA) // TB
    rows = xnat_ref.shape[0]

    xnat_ref[0:PAD, :] = jnp.zeros((PAD, C), F32)
    xnat_ref[PAD + S:rows, :] = jnp.zeros((rows - PAD - S, C), F32)

    def fill(c, carry):
        r0 = pl.multiple_of(c * T, T)
        xnat_ref[pl.ds(r0 + PAD, T), :] = x_ref[0, pl.ds(r0, T), :].astype(F32)
        return carry

    lax.fori_loop(0, S // T, fill, 0)

    neg_lam = -lam_ref[0]
    softplus = jnp.maximum(neg_lam, 0.0) + jnp.log1p(jnp.exp(-jnp.abs(neg_lam)))
    half_coef2 = (-0.5 * RG_C * LOG2E) * softplus
    half_bias = 0.5 * bg_ref[0]
    cwb = [jnp.broadcast_to(cw_ref[k:k + 1, :], (SUBLANES, C)) for k in range(CONV_W)]
    cbb = jnp.broadcast_to(cb_ref[...], (SUBLANES, C))
    sub = lax.broadcasted_iota(jnp.int32, (SUBLANES, C), 0)
    steps_left = S - sub * seg

    def step_rows(t):
        return (pl.ds(PAD + t, SUBLANES, stride=seg), slice(None))

    def block_gates(t0, n, direction, conv):
        rows = pl.ds(pl.multiple_of(t0 * SUBLANES, SUBLANES), n * SUBLANES)
        if conv == "reuse":
            xc = xc_ref[rows, :]
        else:
            taps = [xnat_ref[step_rows(t0 + m - CONV_LEFT)] for m in range(n + CONV_W - 1)]
            xcs = []
            for j in range(n):
                acc = cbb + taps[j] * cwb[0]
                for k in range(1, CONV_W):
                    acc = acc + taps[j + k] * cwb[k]
                xcs.append(acc)
            xc = jnp.concatenate(xcs, axis=0)
            if conv == "keep":
                xc_ref[rows, :] = xc
        lo = direction * 2 * C
        gz = jnp.dot(xc.astype(BF16), wg_ref[0, :, lo:lo + 2 * C], preferred_element_type=F32)
        t = jnp.tanh(0.5 * gz + half_bias[:, lo:lo + 2 * C])
        half_coef = half_coef2[direction:direction + 1]
        a = jnp.exp2(half_coef * t[:, :C] + half_coef)
        y = 1.0 - a * a
        root = jnp.where(y > 0.0, y * lax.rsqrt(y), 0.0)
        half_xc = 0.5 * xc
        u = root * (half_xc * t[:, C:] + half_xc)
        return a, u

    def scan_block(t0, n, direction, carry, past_end=False, conv="own"):
        h, acc = carry
        a, u = block_gates(t0, n, direction, conv)
        order = range(n) if direction == 0 else range(n - 1, -1, -1)
        for j in order:
            aj = a[j * SUBLANES:(j + 1) * SUBLANES]
            uj = u[j * SUBLANES:(j + 1) * SUBLANES]
            if direction == 1 and past_end:
                uj = jnp.where(t0 + j < steps_left, uj, 0.0)
            h = aj * h + uj
            acc = aj * acc
            row = pl.multiple_of((t0 + j) * SUBLANES, SUBLANES)
            hloc_ref[direction, pl.ds(row, SUBLANES), :] = h
            acum_ref[direction, pl.ds(row, SUBLANES), :] = acc
        return h, acc

    def entry_states(h_tot, a_tot, direction):
        c = jnp.zeros((1, C), F32)
        out = jnp.zeros((SUBLANES, C), F32)
        order = range(SUBLANES) if direction == 0 else range(SUBLANES - 1, -1, -1)
        for s in order:
            out = jnp.where(sub == s, c, out)
            c = a_tot[s:s + 1] * c + h_tot[s:s + 1]
        return out

    def fix_block(t0, n, entries):
        for j in range(n):
            row = pl.multiple_of((t0 + j) * SUBLANES, SUBLANES)
            h = None
            for direction in (0, 1):
                part = (hloc_ref[direction, pl.ds(row, SUBLANES), :]
                        + acum_ref[direction, pl.ds(row, SUBLANES), :] * entries[direction])
                h = part if h is None else h + part
            hnat_ref[step_rows(t0 + j)] = h

    tail0 = n_main * TB
    init = (jnp.zeros((SUBLANES, C), F32), jnp.ones((SUBLANES, C), F32))

    def block_start(b):
        return b * TB if isinstance(b, int) else pl.multiple_of(b * TB, TB)

    def main(b, carry, past_end=False, conv="keep"):
        fwd = scan_block(block_start(b), TB, 0, carry[0], conv=conv)
        bwd = scan_block(block_start(n_main - 1 - b), TB, 1, carry[1], past_end, conv=conv)
        return fwd, bwd

    assert TB >= (SUBLANES - 1) * RNN_SEG_EXTRA and n_main >= 2 and n_main % 2 == 0
    bwd_tail = scan_block(tail0, RNN_SEG_EXTRA, 1, init, past_end=True)
    carry = main(0, (init, bwd_tail), past_end=True)
    carry = lax.fori_loop(1, n_main // 2, main, carry)
    fwd_tot, bwd_tot = lax.fori_loop(n_main // 2, n_main, functools.partial(main, conv="reuse"), carry)
    fwd_tot = scan_block(tail0, RNN_SEG_EXTRA, 0, fwd_tot)
    entries = (entry_states(*fwd_tot, 0), entry_states(*bwd_tot, 1))

    def fix(b, carry):
        fix_block(pl.multiple_of(b * TB, TB), TB, entries)
        return carry

    lax.fori_loop(0, n_main, fix, 0)
    fix_block(tail0, RNN_SEG_EXTRA, entries)

    def finish(c, carry):
        r0 = pl.multiple_of(c * T, T)
        gate = gate_ref[0, pl.ds(r0, T), :].astype(F32)
        o_ref[0, pl.ds(r0, T), :] = (hnat_ref[pl.ds(r0 + PAD, T), :] * gate).astype(BF16)
        return carry

    lax.fori_loop(0, S // T, finish, 0)


def _block_diag_pairs(w):
    per = RNN_CG // RG_BW
    w = w.reshape(RG_BLOCKS // per, per, RG_BW, RG_BW)
    rows = []
    for p in range(per):
        cols = [w[:, p] if q == p else jnp.zeros_like(w[:, p]) for q in range(per)]
        rows.append(jnp.concatenate(cols, axis=-1))
    return jnp.concatenate(rows, axis=1)


def _rglru(z, conv_w, conv_b, rg_wa, rg_ba, rg_wx, rg_bx, rg_lambda):
    B, S, _ = z.shape
    C = RNN_CG
    n_grp = D_RNN // C
    wg = jnp.concatenate([_block_diag_pairs(rg_wa[0]), _block_diag_pairs(rg_wx[0]),
                          _block_diag_pairs(rg_wa[1]), _block_diag_pairs(rg_wx[1])], axis=-1).astype(BF16)
    bg = jnp.concatenate([rg_ba[0].reshape(n_grp, 1, C), rg_bx[0].reshape(n_grp, 1, C),
                          rg_ba[1].reshape(n_grp, 1, C), rg_bx[1].reshape(n_grp, 1, C)], axis=-1)
    lam = rg_lambda.reshape(2, n_grp, C).transpose(1, 0, 2)
    step_rows = SUBLANES * _rnn_seg_len(S)
    nat_rows = step_rows + 2 * SUBLANES
    return pl.pallas_call(
        functools.partial(_rglru_kernel, S=S),
        out_shape=jax.ShapeDtypeStruct((B, S, D_RNN), BF16),
        grid=(B, n_grp),
        in_specs=[
            pl.BlockSpec((1, S, C), lambda b, c: (b, 0, c)),
            pl.BlockSpec((1, S, C), lambda b, c: (b, 0, n_grp + c)),
            pl.BlockSpec((CONV_W, C), lambda b, c: (0, c)),
            pl.BlockSpec((1, C), lambda b, c: (0, c)),
            pl.BlockSpec((1, C, 4 * C), lambda b, c: (c, 0, 0)),
            pl.BlockSpec((1, 1, 4 * C), lambda b, c: (c, 0, 0)),
            pl.BlockSpec((1, 2, C), lambda b, c: (c, 0, 0)),
        ],
        out_specs=pl.BlockSpec((1, S, C), lambda b, c: (b, 0, c)),
        scratch_shapes=[pltpu.VMEM((nat_rows, C), F32), pltpu.VMEM((nat_rows, C), F32),
                        pltpu.VMEM((2, step_rows, C), F32), pltpu.VMEM((2, step_rows, C), F32),
                        pltpu.VMEM((step_rows, C), F32)],
        compiler_params=_params("parallel", "parallel"),
        name="rglru",
    )(z, z, conv_w, conv_b.reshape(1, -1), wg, bg, lam)


def _attn_kernel(q_ref, k_ref, v_ref, o_ref, l_ref, cap_ref, *, L, n_res):
    BQ, KW = ATT_BQ, ATT_KW
    lane = lax.broadcasted_iota(jnp.int32, (BQ, LANES), 1)
    first = lane < HEAD_DIM
    first_kw = lax.broadcasted_iota(jnp.int32, (KW, LANES), 1) < HEAD_DIM
    rel = lax.broadcasted_iota(jnp.int32, (BQ, KW), 0) - lax.broadcasted_iota(jnp.int32, (BQ, KW), 1)
    n_blocks = L // BQ
    total = n_res * n_blocks
    assert n_blocks & (n_blocks - 1) == 0 and total % ATT_UNROLL == 0

    for j in range(3):
        cap = jnp.where(jnp.abs(rel + j * RADIUS) <= RADIUS, jnp.inf, NEG_INF).astype(F32)
        cap_ref[j, 0:BQ, :] = cap
        cap_ref[j, BQ:2 * BQ, :] = cap

    def scores(idx):
        r = lax.shift_right_logical(idx, n_blocks.bit_length() - 1)
        q0 = pl.multiple_of(jnp.bitwise_and(idx, n_blocks - 1) * BQ, BQ)
        k0 = pl.multiple_of(jnp.clip(q0 - RADIUS, 0, L - KW), RADIUS)
        q = q_ref[0, r, pl.ds(q0, BQ), :]
        k = k_ref[0, r, pl.ds(k0, KW), :]
        zero = jnp.zeros_like(q)
        q2 = jnp.concatenate([jnp.where(first, q, zero), jnp.where(first, zero, q)], axis=0)
        s = lax.dot_general(q2, k, (((1,), (1,)), ((), ())), preferred_element_type=F32)
        return r, q0, k0, s

    def finish(r, q0, k0, s):
        cap = cap_ref[lax.shift_right_logical(q0 - k0, RADIUS.bit_length() - 1)]
        v = v_ref[0, r, pl.ds(k0, KW), :]
        s = jnp.minimum(s, cap)
        m = jnp.max(s, axis=-1, keepdims=True)
        p = jnp.exp2(s - m).astype(BF16)
        one = jnp.ones_like(v)
        pv_a = jnp.dot(p[:BQ], jnp.where(first_kw, v, one), preferred_element_type=F32)
        pv_b = jnp.dot(p[BQ:], jnp.where(first_kw, one, v), preferred_element_type=F32)
        num = jnp.where(first, pv_a, pv_b)
        den = pltpu.roll(jnp.where(first, pv_b, pv_a), HEAD_DIM, axis=1)
        top = jnp.where(first, m[:BQ], m[BQ:])
        o_ref[0, r, pl.ds(q0, BQ), :] = (num / den).astype(BF16)
        l_ref[0, r, pl.ds(q0, BQ), :] = top * LN2 + jnp.log(den)

    def body(it, carry):
        pending = [scores(it * ATT_UNROLL + j) for j in range(ATT_AHEAD)]
        for j in range(ATT_UNROLL):
            if j + ATT_AHEAD < ATT_UNROLL:
                pending.append(scores(it * ATT_UNROLL + j + ATT_AHEAD))
            finish(*pending.pop(0))
        return carry

    lax.fori_loop(0, total // ATT_UNROLL, body, 0)


def _attention(qkv):
    B, dil, L, _ = qkv.shape
    n_pair = GROUP_W // LANES
    spec = lambda off: pl.BlockSpec((1, dil, L, LANES), lambda b, p: (b, 0, 0, off + p))
    return pl.pallas_call(
        functools.partial(_attn_kernel, L=L, n_res=dil),
        out_shape=[jax.ShapeDtypeStruct((B, dil, L, GROUP_W), BF16),
                   jax.ShapeDtypeStruct((B, dil, L, GROUP_W), F32)],
        grid=(B, n_pair),
        in_specs=[spec(0), spec(n_pair), spec(2 * n_pair)],
        out_specs=[spec(0), spec(0)],
        scratch_shapes=[pltpu.VMEM((3, 2 * ATT_BQ, ATT_KW), F32)],
        compiler_params=_params("parallel", "parallel"),
        name="attn",
    )(qkv, qkv, qkv)


def _mixffn_kernel(x_ref, gt1_ref, sc_ref, sh_ref, gt2_ref, g2_ref, gf_ref, mg_ref, rnn_ref,
                   o0_ref, l0_ref, o1_ref, l1_ref, o2_ref, l2_ref,
                   wr_ref, wa_ref, wo_ref, wi_ref, wf_ref, y_ref,
                   fo1_ref, fl1_ref, fo2_ref, fl2_ref, *, tm):
    n_tiles = GROUP_W // LANES

    def unfold(src_ref, dst_ref, dil):
        pitch = _fold_pitch(dil)
        for r in range(dil):
            for s in range(n_tiles):
                dst_ref[s, pl.ds(r, tm // dil, stride=pitch), :] = src_ref[0, r, :, s * LANES:(s + 1) * LANES].astype(F32)
        if pitch == dil:
            return jnp.concatenate([dst_ref[s, 0:tm, :] for s in range(n_tiles)], axis=1)
        return jnp.concatenate(
            [jnp.concatenate([dst_ref[s, pitch * m:pitch * m + dil, :] for m in range(tm // dil)], axis=0)
             for s in range(n_tiles)], axis=1)

    o1 = unfold(o1_ref, fo1_ref, ATT_GROUPS[1][1])
    l1 = unfold(l1_ref, fl1_ref, ATT_GROUPS[1][1])
    o2 = unfold(o2_ref, fo2_ref, ATT_GROUPS[2][1])
    l2 = unfold(l2_ref, fl2_ref, ATT_GROUPS[2][1])
    lses = (l0_ref[0, 0], l1, l2)
    outs = (o0_ref[0, 0].astype(F32), o1, o2)

    top = jnp.maximum(jnp.maximum(lses[0], lses[1]), lses[2])
    es = [jnp.exp(l - top) for l in lses]
    att = (es[0] * outs[0] + es[1] * outs[1] + es[2] * outs[2]) / (es[0] + es[1] + es[2])

    br_a = jnp.dot(att.astype(BF16), wa_ref[...], preferred_element_type=F32)
    br_r = jnp.dot(rnn_ref[0], wr_ref[...], preferred_element_type=F32)
    gate = _sigmoid(mg_ref[0].astype(F32))
    merged = gate[:, :D_MODEL] * br_r + gate[:, D_MODEL:] * br_a
    rows = tm // HEAD_PIECES
    pieces = [slice(r0, r0 + rows) for r0 in range(0, tm, rows)]
    merged = merged.astype(BF16)
    x1s, hs = [], []
    for p in pieces:
        mix = jnp.dot(merged[p], wo_ref[...], preferred_element_type=F32)
        x1s.append(x_ref[0, p, :] + gt1_ref[0] * mix)
        hs.append(_rms_mod(x1s[-1], g2_ref[...], sc_ref[0], sh_ref[0]).astype(BF16))
    h = jnp.concatenate(hs, axis=0)

    def up(lhs, c):
        lo = c * FF_CHUNK
        fg = jnp.dot(lhs, wi_ref[:, lo:lo + FF_CHUNK], preferred_element_type=F32)
        fu = jnp.dot(lhs, wi_ref[:, D_FF + lo:D_FF + lo + FF_CHUNK], preferred_element_type=F32)
        return ((fg * _sigmoid(fg)) * fu).astype(BF16)

    n_chunks = D_FF // FF_CHUNK
    ff = None
    for c in range(n_chunks - 1):
        act = jnp.concatenate([up(part, c) for part in hs], axis=0) if c == 0 else up(h, c)
        part = jnp.dot(act, wf_ref[c * FF_CHUNK:(c + 1) * FF_CHUNK, :], preferred_element_type=F32)
        ff = part if ff is None else ff + part
    act = up(h, n_chunks - 1)
    for p, x1 in zip(pieces, x1s):
        ffp = ff[p] + jnp.dot(act[p], wf_ref[(n_chunks - 1) * FF_CHUNK:, :], preferred_element_type=F32)
        x2 = x1 + gt2_ref[0] * ffp
        y = x2 * lax.rsqrt(jnp.mean(x2 * x2, axis=-1, keepdims=True) + EPS)
        y_ref[0, p, :] = y * gf_ref[...]


def _mixffn(x, mods, g2, gf, z, rnn, attn_outs, weights, tm):
    B, S, D = x.shape
    mod_spec = pl.BlockSpec((1, 1, D), lambda b, i: (b, 0, 0))
    in_specs = [pl.BlockSpec((1, tm, D), lambda b, i: (b, i, 0))] + [mod_spec] * len(mods)
    in_specs += [
        _resident((1, D)),
        _resident((1, D)),
        pl.BlockSpec((1, tm, 2 * D_MODEL), lambda b, i: (b, i, 1)),
        pl.BlockSpec((1, tm, D_RNN), lambda b, i: (b, i, 0)),
    ]
    args = [x, *mods, g2, gf, z, rnn]
    for (o, l), (_, d) in zip(attn_outs, ATT_GROUPS):
        blk = pl.BlockSpec((1, d, tm // d, GROUP_W), lambda b, i: (b, 0, i, 0))
        in_specs += [blk, blk]
        args += [o, l]
    in_specs += [_resident(w.shape) for w in weights]
    args += list(weights)
    return pl.pallas_call(
        functools.partial(_mixffn_kernel, tm=tm),
        out_shape=jax.ShapeDtypeStruct((B, S, D), F32),
        grid=(B, S // tm),
        in_specs=in_specs,
        out_specs=pl.BlockSpec((1, tm, D), lambda b, i: (b, i, 0)),
        scratch_shapes=[pltpu.VMEM((GROUP_W // LANES, _fold_pitch(d) * (tm // d), LANES), F32)
                        for _, d in ATT_GROUPS[1:] for _ in range(2)],
        compiler_params=_params("parallel", "parallel"),
        name="mixffn",
    )(*args)


def _encode(x, mod, w, tm_in, tm_mix):
    sh1, sc1, gt1, sh2, sc2, gt2 = (mod[:, k] for k in range(N_MOD))
    outs = _inproj(x, sc1, sh1, w["norm1_g"], w["w_in"], tm_in)
    z, qkvs = outs[0], outs[1:]
    rnn = _rglru(z, w["conv_w"], w["conv_b"], w["rg_wa"], w["rg_ba"], w["rg_wx"], w["rg_bx"], w["rg_lambda"])
    attn_outs = [_attention(qkv) for qkv in qkvs]
    weights = (w["w_br_rnn"], w["w_br_attn"], w["w_out"], w["w_ffn_in"], w["w_ffn_out"])
    return _mixffn(x, (gt1, sc2, sh2, gt2), w["norm2_g"], w["final_g"], z, rnn, attn_outs, weights, tm_mix)


def kernel(x_prompt, x_sample, c_prompt, c_sample, w_ada, b_ada, norm1_g, w_in, conv_w, conv_b, rg_wa, rg_ba, rg_wx, rg_bx, rg_lambda, w_br_rnn, w_br_attn, w_out, norm2_g, w_ffn_in, w_ffn_out, final_g):
    assert w_ada.shape[0] == 1, "single layer"
    w = dict(
        norm1_g=norm1_g[0].reshape(1, -1),
        w_in=w_in[0].astype(BF16), conv_w=conv_w[0], conv_b=conv_b[0],
        rg_wa=rg_wa[0], rg_ba=rg_ba[0], rg_wx=rg_wx[0], rg_bx=rg_bx[0], rg_lambda=rg_lambda[0],
        w_br_rnn=w_br_rnn[0].astype(BF16), w_br_attn=w_br_attn[0].astype(BF16), w_out=w_out[0].astype(BF16),
        norm2_g=norm2_g[0].reshape(1, -1), w_ffn_in=w_ffn_in[0].astype(BF16),
        w_ffn_out=w_ffn_out[0].astype(BF16), final_g=final_g.reshape(1, -1),
    )
    tiles = dict(tm_in=512, tm_mix=512)
    n_prompt = c_prompt.shape[0]
    mod = _modulation(jnp.concatenate([c_prompt, c_sample], axis=0), w_ada[0], b_ada[0])
    return (_encode(x_prompt, mod[:n_prompt], w, **tiles), _encode(x_sample, mod[n_prompt:], w, **tiles))
```

```python
import functools

import jax
import jax.numpy as jnp
from jax import lax
from jax.experimental import pallas as pl
from jax.experimental.pallas import tpu as pltpu

F32 = jnp.float32
BF16 = jnp.bfloat16

D_MODEL = 1024
D_RNN = 1024
RG_BLOCKS = 16
RG_BW = D_RNN // RG_BLOCKS
RG_C = 8.0
CONV_W = 4
CONV_LEFT = 2
HEAD_DIM = 64
HEADS_PER_GROUP = 8
ATT_GROUPS = ((128, 1), (512, 4), (2048, 16))
N_GROUPS = len(ATT_GROUPS)
GROUP_W = HEADS_PER_GROUP * HEAD_DIM
ATT_W = N_GROUPS * GROUP_W
ROT_DIM = HEAD_DIM // 4
ROPE_THETA = 500000.0
D_FF = 2816
N_MOD = 6
EPS = 1e-6
NEG_INF = -1e30
LOG2E = 1.4426950408889634
IN_COLS = 2 * D_RNN + 3 * ATT_W + 2 * D_MODEL
RADIUS = 64

LANES = 128
SUBLANES = 8
BF16_ROWS = 2 * SUBLANES
VMEM_LIMIT_BYTES = 56 * 1024 * 1024

COL_CHUNK = 512
RNN_CG = 128
RNN_T = 256
RNN_TB = 128
RNN_SEG_EXTRA = 4
ATT_BQ = 128
ATT_KW = ATT_BQ + 2 * RADIUS
ATT_UNROLL = 8
ATT_AHEAD = 2
FF_CHUNK = 256
HEAD_PIECES = 2


def _resident(shape):
    nd = len(shape)
    return pl.BlockSpec(shape, lambda *_: (0,) * nd, pipeline_mode=pl.Buffered(1))


def _params(*sem):
    return pltpu.CompilerParams(dimension_semantics=sem, vmem_limit_bytes=VMEM_LIMIT_BYTES)


def _rms_mod(x, g, sc, sh):
    y = x * lax.rsqrt(jnp.mean(x * x, axis=-1, keepdims=True) + EPS)
    return y * (g * (1.0 + sc)) + sh


def _sigmoid(x):
    return 1.0 / (1.0 + jnp.exp2(x * -LOG2E))


def _gelu_tanh(x):
    return 0.5 * x * (1.0 + jnp.tanh(0.7978845608028654 * (x + 0.044715 * (x * x * x))))


def _mod_kernel(c_ref, w_ref, b_ref, o_ref):
    c = c_ref[...]
    s = (c * _sigmoid(c)).astype(BF16)
    o_ref[...] = jnp.dot(s, w_ref[...].astype(BF16), preferred_element_type=F32) + b_ref[...]


def _modulation(c, w_ada, b_ada):
    B = c.shape[0]
    rows = -(-B // BF16_ROWS) * BF16_ROWS
    cp = jnp.pad(c, ((0, rows - B), (0, 0)))
    out = pl.pallas_call(
        _mod_kernel,
        out_shape=jax.ShapeDtypeStruct((rows, N_MOD * D_MODEL), F32),
        grid=(N_MOD,),
        in_specs=[
            pl.BlockSpec((rows, D_MODEL), lambda j: (0, 0)),
            pl.BlockSpec((D_MODEL, D_MODEL), lambda j: (0, j)),
            pl.BlockSpec((1, D_MODEL), lambda j: (0, j)),
        ],
        out_specs=pl.BlockSpec((rows, D_MODEL), lambda j: (0, j)),
        compiler_params=_params("parallel"),
        name="mod",
    )(cp, w_ada, b_ada.reshape(1, -1))
    return out[:B].reshape(B, N_MOD, 1, D_MODEL)


def _fold_pitch(dil):
    return dil if dil % (2 * SUBLANES) else dil + SUBLANES


def _inproj_kernel(x_ref, sc_ref, sh_ref, g_ref, w_ref, cos_ref, sa_ref, sb_ref,
                   z_ref, q0_ref, q1_ref, q2_ref, fold_ref, *, tm):
    rows = tm // HEAD_PIECES
    pieces = [_rms_mod(x_ref[0, r0:r0 + rows, :], g_ref[...], sc_ref[0], sh_ref[0]).astype(BF16)
              for r0 in range(0, tm, rows)]
    h = jnp.concatenate(pieces, axis=0)
    issued = []

    def proj(j):
        w = w_ref[:, j * COL_CHUNK:(j + 1) * COL_CHUNK]
        issued.append(j)
        if len(issued) == 1:
            return jnp.concatenate([jnp.dot(part, w, preferred_element_type=F32) for part in pieces], axis=0)
        return jnp.dot(h, w, preferred_element_type=F32)

    def rope(v):
        parts = []
        for s in range(COL_CHUNK // LANES):
            p = v[:, s * LANES:(s + 1) * LANES]
            up = pltpu.roll(p, LANES - ROT_DIM // 2, axis=1)
            dn = pltpu.roll(p, ROT_DIM // 2, axis=1)
            parts.append(p * cos_ref[...] + up * sa_ref[...] + dn * sb_ref[...])
        return jnp.concatenate(parts, axis=1)

    qkv_refs = (q0_ref, q1_ref, q2_ref)

    def emit_folded(val, g, off):
        dil = ATT_GROUPS[g][1]
        ref = qkv_refs[g]
        if dil == 1:
            ref[0, 0, :, off:off + COL_CHUNK] = val.astype(BF16)
            return
        pitch = _fold_pitch(dil)
        for s in range(COL_CHUNK // LANES):
            slab = val[:, s * LANES:(s + 1) * LANES]
            if pitch == dil:
                fold_ref[s, 0:tm, :] = slab
            else:
                for m in range(tm // dil):
                    fold_ref[s, pitch * m:pitch * m + dil, :] = slab[dil * m:dil * (m + 1)]
        for r in range(dil):
            for s in range(COL_CHUNK // LANES):
                lo = off + s * LANES
                ref[0, r, :, lo:lo + LANES] = fold_ref[s, pl.ds(r, tm // dil, stride=pitch), :].astype(BF16)

    def emit_z(j_w, j_z, fn):
        z_ref[0, :, j_z * COL_CHUNK:(j_z + 1) * COL_CHUNK] = fn(proj(j_w)).astype(BF16)

    n_x = D_RNN // COL_CHUNK
    n_rnn = 2 * n_x
    n_gate0 = n_rnn + 3 * N_GROUPS
    for j in range(2 * D_MODEL // COL_CHUNK):
        emit_z(n_gate0 + j, n_rnn + j, lambda v: v)
        if j < n_x:
            emit_z(n_x + j, n_x + j, _gelu_tanh)
    for g in range(N_GROUPS):
        emit_folded(rope(proj(n_rnn + g)) * (HEAD_DIM ** -0.5 * LOG2E), g, 0)
        emit_folded(rope(proj(n_rnn + N_GROUPS + g)), g, GROUP_W)
        emit_folded(proj(n_rnn + 2 * N_GROUPS + g), g, 2 * GROUP_W)
    for j in range(n_x):
        emit_z(j, j, lambda v: v)


def _rope_tables(S):
    half = ROT_DIM // 2
    inv = ROPE_THETA ** (-(jnp.arange(0, ROT_DIM, 2, dtype=F32) / ROT_DIM))
    ang = jnp.arange(S, dtype=F32)[:, None] * inv[None, :]
    cos, sin = jnp.cos(ang), jnp.sin(ang)
    zeros = jnp.zeros((S, HEAD_DIM - ROT_DIM), F32)
    z8 = jnp.zeros((S, half), F32)
    c = jnp.concatenate([cos, cos, zeros + 1.0], axis=1)
    sa = jnp.concatenate([-sin, z8, zeros], axis=1)
    sb = jnp.concatenate([z8, sin, zeros], axis=1)
    rep = LANES // HEAD_DIM
    return tuple(jnp.tile(t, (1, rep)) for t in (c, sa, sb))


def _inproj(x, sc, sh, g, w_in, tm):
    B, S, D = x.shape
    cos, sa, sb = _rope_tables(S)
    dils = [d for _, d in ATT_GROUPS]
    assert S % tm == 0 and all(tm % (d * BF16_ROWS) == 0 for d in dils) and tm % (HEAD_PIECES * BF16_ROWS) == 0
    row = lambda b, i: (b, 0, 0)
    out_shape = [jax.ShapeDtypeStruct((B, S, 2 * D_RNN + 2 * D_MODEL), BF16)]
    out_specs = [pl.BlockSpec((1, tm, 2 * D_RNN + 2 * D_MODEL), lambda b, i: (b, i, 0))]
    for d in dils:
        out_shape.append(jax.ShapeDtypeStruct((B, d, S // d, 3 * GROUP_W), BF16))
        out_specs.append(pl.BlockSpec((1, d, tm // d, 3 * GROUP_W), lambda b, i: (b, 0, i, 0)))
    tab = pl.BlockSpec((tm, LANES), lambda b, i: (i, 0))
    return pl.pallas_call(
        functools.partial(_inproj_kernel, tm=tm),
        out_shape=out_shape,
        grid=(B, S // tm),
        in_specs=[
            pl.BlockSpec((1, tm, D), lambda b, i: (b, i, 0)),
            pl.BlockSpec((1, 1, D), row),
            pl.BlockSpec((1, 1, D), row),
            _resident((1, D)),
            _resident((D, IN_COLS)),
            tab, tab, tab,
        ],
        out_specs=out_specs,
        scratch_shapes=[pltpu.VMEM((COL_CHUNK // LANES, max(_fold_pitch(d) * (tm // d) for d in dils), LANES), F32)],
        compiler_params=_params("parallel", "parallel"),
        name="inproj",
    )(x, sc, sh, g, w_in, cos, sa, sb)


def _rnn_seg_len(S):
    assert S % (8 * SUBLANES) == 0
    return S // SUBLANES + RNN_SEG_EXTRA


def _rglru_kernel(x_ref, gate_ref, cw_ref, cb_ref, wg_ref, bg_ref, lam_ref, o_ref,
                  xnat_ref, hnat_ref, hloc_ref, acum_ref, xc_ref, *, S):
    C = RNN_CG
    T = RNN_T
    TB = RNN_TB
    PAD = SUBLANES
    seg = _rnn_seg_len(S)
    n_main = (seg - RNN_SEG_EXTRA) // TB
    rows = xnat_ref.shape[0]

    xnat_ref[0:PAD, :] = jnp.zeros((PAD, C), F32)
    xnat_ref[PAD + S:rows, :] = jnp.zeros((rows - PAD - S, C), F32)

    def fill(c, carry):
        r0 = pl.multiple_of(c * T, T)
        xnat_ref[pl.ds(r0 + PAD, T), :] = x_ref[0, pl.ds(r0, T), :].astype(F32)
        return carry

    lax.fori_loop(0, S // T, fill, 0)

    neg_lam = -lam_ref[0]
    softplus = jnp.maximum(neg_lam, 0.0) + jnp.log1p(jnp.exp(-jnp.abs(neg_lam)))
    half_coef2 = (-0.5 * RG_C * LOG2E) * softplus
    half_bias = 0.5 * bg_ref[0]
    cwb = [jnp.broadcast_to(cw_ref[k:k + 1, :], (SUBLANES, C)) for k in range(CONV_W)]
    cbb = jnp.broadcast_to(cb_ref[...], (SUBLANES, C))
    sub = lax.broadcasted_iota(jnp.int32, (SUBLANES, C), 0)
    steps_left = S - sub * seg

    def step_rows(t):
        return (pl.ds(PAD + t, SUBLANES, stride=seg), slice(None))

    def block_gates(t0, n, direction, conv):
        rows = pl.ds(pl.multiple_of(t0 * SUBLANES, SUBLANES), n * SUBLANES)
        if conv == "reuse":
            xc = xc_ref[rows, :]
        else:
            taps = [xnat_ref[step_rows(t0 + m - CONV_LEFT)] for m in range(n + CONV_W - 1)]
            xcs = []
            for j in range(n):
                acc = cbb + taps[j] * cwb[0]
                for k in range(1, CONV_W):
                    acc = acc + taps[j + k] * cwb[k]
                xcs.append(acc)
            xc = jnp.concatenate(xcs, axis=0)
            if conv == "keep":
                xc_ref[rows, :] = xc
        lo = direction * 2 * C
        gz = jnp.dot(xc.astype(BF16), wg_ref[0, :, lo:lo + 2 * C], preferred_element_type=F32)
        t = jnp.tanh(0.5 * gz + half_bias[:, lo:lo + 2 * C])
        half_coef = half_coef2[direction:direction + 1]
        a = jnp.exp2(half_coef * t[:, :C] + half_coef)
        y = 1.0 - a * a
        root = jnp.where(y > 0.0, y * lax.rsqrt(y), 0.0)
        half_xc = 0.5 * xc
        u = root * (half_xc * t[:, C:] + half_xc)
        return a, u

    def scan_block(t0, n, direction, carry, past_end=False, conv="own"):
        h, acc = carry
        a, u = block_gates(t0, n, direction, conv)
        order = range(n) if direction == 0 else range(n - 1, -1, -1)
        for j in order:
            aj = a[j * SUBLANES:(j + 1) * SUBLANES]
            uj = u[j * SUBLANES:(j + 1) * SUBLANES]
            if direction == 1 and past_end:
                uj = jnp.where(t0 + j < steps_left, uj, 0.0)
            h = aj * h + uj
            acc = aj * acc
            row = pl.multiple_of((t0 + j) * SUBLANES, SUBLANES)
            hloc_ref[direction, pl.ds(row, SUBLANES), :] = h
            acum_ref[direction, pl.ds(row, SUBLANES), :] = acc
        return h, acc

    def entry_states(h_tot, a_tot, direction):
        c = jnp.zeros((1, C), F32)
        out = jnp.zeros((SUBLANES, C), F32)
        order = range(SUBLANES) if direction == 0 else range(SUBLANES - 1, -1, -1)
        for s in order:
            out = jnp.where(sub == s, c, out)
            c = a_tot[s:s + 1] * c + h_tot[s:s + 1]
        return out

    def fix_block(t0, n, entries):
        for j in range(n):
            row = pl.multiple_of((t0 + j) * SUBLANES, SUBLANES)
            h = None
            for direction in (0, 1):
                part = (hloc_ref[direction, pl.ds(row, SUBLANES), :]
                        + acum_ref[direction, pl.ds(row, SUBLANES), :] * entries[direction])
                h = part if h is None else h + part
            hnat_ref[step_rows(t0 + j)] = h

    tail0 = n_main * TB
    init = (jnp.zeros((SUBLANES, C), F32), jnp.ones((SUBLANES, C), F32))

    def block_start(b):
        return b * TB if isinstance(b, int) else pl.multiple_of(b * TB, TB)

    def main(b, carry, past_end=False, conv="keep"):
        fwd = scan_block(block_start(b), TB, 0, carry[0], conv=conv)
        bwd = scan_block(block_start(n_main - 1 - b), TB, 1, carry[1], past_end, conv=conv)
        return fwd, bwd

    assert TB >= (SUBLANES - 1) * RNN_SEG_EXTRA and n_main >= 2 and n_main % 2 == 0
    bwd_tail = scan_block(tail0, RNN_SEG_EXTRA, 1, init, past_end=True)
    carry = main(0, (init, bwd_tail), past_end=True)
    carry = lax.fori_loop(1, n_main // 2, main, carry)
    fwd_tot, bwd_tot = lax.fori_loop(n_main // 2, n_main, functools.partial(main, conv="reuse"), carry)
    fwd_tot = scan_block(tail0, RNN_SEG_EXTRA, 0, fwd_tot)
    entries = (entry_states(*fwd_tot, 0), entry_states(*bwd_tot, 1))

    def fix(b, carry):
        fix_block(pl.multiple_of(b * TB, TB), TB, entries)
        return carry

    lax.fori_loop(0, n_main, fix, 0)
    fix_block(tail0, RNN_SEG_EXTRA, entries)

    def finish(c, carry):
        r0 = pl.multiple_of(c * T, T)
        gate = gate_ref[0, pl.ds(r0, T), :].astype(F32)
        o_ref[0, pl.ds(r0, T), :] = (hnat_ref[pl.ds(r0 + PAD, T), :] * gate).astype(BF16)
        return carry

    lax.fori_loop(0, S // T, finish, 0)


def _block_diag_pairs(w):
    per = RNN_CG // RG_BW
    w = w.reshape(RG_BLOCKS // per, per, RG_BW, RG_BW)
    rows = []
    for p in range(per):
        cols = [w[:, p] if q == p else jnp.zeros_like(w[:, p]) for q in range(per)]
        rows.append(jnp.concatenate(cols, axis=-1))
    return jnp.concatenate(rows, axis=1)


def _rglru(z, conv_w, conv_b, rg_wa, rg_ba, rg_wx, rg_bx, rg_lambda):
    B, S, _ = z.shape
    C = RNN_CG
    n_grp = D_RNN // C
    wg = jnp.concatenate([_block_diag_pairs(rg_wa[0]), _block_diag_pairs(rg_wx[0]),
                          _block_diag_pairs(rg_wa[1]), _block_diag_pairs(rg_wx[1])], axis=-1).astype(BF16)
    bg = jnp.concatenate([rg_ba[0].reshape(n_grp, 1, C), rg_bx[0].reshape(n_grp, 1, C),
                          rg_ba[1].reshape(n_grp, 1, C), rg_bx[1].reshape(n_grp, 1, C)], axis=-1)
    lam = rg_lambda.reshape(2, n_grp, C).transpose(1, 0, 2)
    step_rows = SUBLANES * _rnn_seg_len(S)
    nat_rows = step_rows + 2 * SUBLANES
    return pl.pallas_call(
        functools.partial(_rglru_kernel, S=S),
        out_shape=jax.ShapeDtypeStruct((B, S, D_RNN), BF16),
        grid=(B, n_grp),
        in_specs=[
            pl.BlockSpec((1, S, C), lambda b, c: (b, 0, c)),
            pl.BlockSpec((1, S, C), lambda b, c: (b, 0, n_grp + c)),
            pl.BlockSpec((CONV_W, C), lambda b, c: (0, c)),
            pl.BlockSpec((1, C), lambda b, c: (0, c)),
            pl.BlockSpec((1, C, 4 * C), lambda b, c: (c, 0, 0)),
            pl.BlockSpec((1, 1, 4 * C), lambda b, c: (c, 0, 0)),
            pl.BlockSpec((1, 2, C), lambda b, c: (c, 0, 0)),
        ],
        out_specs=pl.BlockSpec((1, S, C), lambda b, c: (b, 0, c)),
        scratch_shapes=[pltpu.VMEM((nat_rows, C), F32), pltpu.VMEM((nat_rows, C), F32),
                        pltpu.VMEM((2, step_rows, C), F32), pltpu.VMEM((2, step_rows, C), F32),
                        pltpu.VMEM((step_rows, C), F32)],
        compiler_params=_params("parallel", "parallel"),
        name="rglru",
    )(z, z, conv_w, conv_b.reshape(1, -1), wg, bg, lam)


def _attn_kernel(q_ref, k_ref, v_ref, o_ref, l_ref, cap_ref, *, L, n_res):
    BQ, KW = ATT_BQ, ATT_KW
    lane = lax.broadcasted_iota(jnp.int32, (BQ, LANES), 1)
    first = lane < HEAD_DIM
    first_kw = lax.broadcasted_iota(jnp.int32, (KW, LANES), 1) < HEAD_DIM
    ones_kw = jnp.concatenate([first_kw, jnp.logical_not(first_kw)], axis=0).astype(BF16)
    rel = lax.broadcasted_iota(jnp.int32, (BQ, KW), 0) - lax.broadcasted_iota(jnp.int32, (BQ, KW), 1)
    n_blocks = L // BQ
    total = n_res * n_blocks
    assert n_blocks & (n_blocks - 1) == 0 and total % ATT_UNROLL == 0

    for j in range(3):
        cap = jnp.where(jnp.abs(rel + j * RADIUS) <= RADIUS, jnp.inf, NEG_INF).astype(F32)
        cap_ref[j, 0:BQ, :] = cap
        cap_ref[j, BQ:2 * BQ, :] = cap

    def scores(idx):
        r = lax.shift_right_logical(idx, n_blocks.bit_length() - 1)
        q0 = pl.multiple_of(jnp.bitwise_and(idx, n_blocks - 1) * BQ, BQ)
        k0 = pl.multiple_of(jnp.clip(q0 - RADIUS, 0, L - KW), RADIUS)
        q = q_ref[0, r, pl.ds(q0, BQ), :]
        k = k_ref[0, r, pl.ds(k0, KW), :]
        zero = jnp.zeros_like(q)
        q2 = jnp.concatenate([jnp.where(first, q, zero), jnp.where(first, zero, q)], axis=0)
        s = lax.dot_general(q2, k, (((1,), (1,)), ((), ())), preferred_element_type=F32)
        return r, q0, k0, s

    def finish(r, q0, k0, s):
        cap = cap_ref[lax.shift_right_logical(q0 - k0, RADIUS.bit_length() - 1)]
        s = jnp.minimum(s, cap)
        m = jnp.max(s, axis=-1, keepdims=True)
        p = jnp.exp2(s - m).astype(BF16)
        v = v_ref[0, r, pl.ds(k0, KW), :]
        zero = jnp.zeros_like(v)
        values = jnp.concatenate([jnp.where(first_kw, v, zero), jnp.where(first_kw, zero, v)], axis=0)
        pv = jnp.dot(jnp.concatenate([p[:BQ], p[BQ:]], axis=1), jnp.concatenate([values, ones_kw], axis=1),
                     preferred_element_type=F32)
        num, den = pv[:, :LANES], pv[:, LANES:]
        top = jnp.where(first, m[:BQ], m[BQ:])
        o_ref[0, r, pl.ds(q0, BQ), :] = (num / den).astype(BF16)
        l_ref[0, r, pl.ds(q0, BQ), :] = top + jnp.log2(den)

    def body(it, carry):
        pending = [scores(it * ATT_UNROLL + j) for j in range(ATT_AHEAD)]
        for j in range(ATT_UNROLL):
            if j + ATT_AHEAD < ATT_UNROLL:
                pending.append(scores(it * ATT_UNROLL + j + ATT_AHEAD))
            finish(*pending.pop(0))
        return carry

    lax.fori_loop(0, total // ATT_UNROLL, body, 0)


def _attention(qkv):
    B, dil, L, _ = qkv.shape
    n_pair = GROUP_W // LANES
    spec = lambda off: pl.BlockSpec((1, dil, L, LANES), lambda b, p: (b, 0, 0, off + p))
    return pl.pallas_call(
        functools.partial(_attn_kernel, L=L, n_res=dil),
        out_shape=[jax.ShapeDtypeStruct((B, dil, L, GROUP_W), BF16),
                   jax.ShapeDtypeStruct((B, dil, L, GROUP_W), F32)],
        grid=(B, n_pair),
        in_specs=[spec(0), spec(n_pair), spec(2 * n_pair)],
        out_specs=[spec(0), spec(0)],
        scratch_shapes=[pltpu.VMEM((3, 2 * ATT_BQ, ATT_KW), F32)],
        compiler_params=_params("parallel", "parallel"),
        name="attn",
    )(qkv, qkv, qkv)


def _mixffn_kernel(x_ref, gt1_ref, sc_ref, sh_ref, gt2_ref, g2_ref, gf_ref, mg_ref, rnn_ref,
                   o0_ref, l0_ref, o1_ref, l1_ref, o2_ref, l2_ref,
                   wr_ref, wa_ref, wo_ref, wi_ref, wf_ref, y_ref,
                   fo1_ref, fl1_ref, fo2_ref, fl2_ref, *, tm):
    n_tiles = GROUP_W // LANES

    def unfold(src_ref, dst_ref, dil):
        pitch = _fold_pitch(dil)
        for r in range(dil):
            for s in range(n_tiles):
                dst_ref[s, pl.ds(r, tm // dil, stride=pitch), :] = src_ref[0, r, :, s * LANES:(s + 1) * LANES].astype(F32)
        if pitch == dil:
            return jnp.concatenate([dst_ref[s, 0:tm, :] for s in range(n_tiles)], axis=1)
        return jnp.concatenate(
            [jnp.concatenate([dst_ref[s, pitch * m:pitch * m + dil, :] for m in range(tm // dil)], axis=0)
             for s in range(n_tiles)], axis=1)

    o1 = unfold(o1_ref, fo1_ref, ATT_GROUPS[1][1])
    l1 = unfold(l1_ref, fl1_ref, ATT_GROUPS[1][1])
    o2 = unfold(o2_ref, fo2_ref, ATT_GROUPS[2][1])
    l2 = unfold(l2_ref, fl2_ref, ATT_GROUPS[2][1])
    lses = (l0_ref[0, 0], l1, l2)
    outs = (o0_ref[0, 0].astype(F32), o1, o2)

    top = jnp.maximum(jnp.maximum(lses[0], lses[1]), lses[2])
    es = [jnp.exp2(l - top) for l in lses]
    att = (es[0] * outs[0] + es[1] * outs[1] + es[2] * outs[2]) / (es[0] + es[1] + es[2])

    br_a = jnp.dot(att.astype(BF16), wa_ref[...], preferred_element_type=F32)
    br_r = jnp.dot(rnn_ref[0], wr_ref[...], preferred_element_type=F32)
    gate = _sigmoid(mg_ref[0].astype(F32))
    merged = gate[:, :D_MODEL] * br_r + gate[:, D_MODEL:] * br_a
    rows = tm // HEAD_PIECES
    pieces = [slice(r0, r0 + rows) for r0 in range(0, tm, rows)]
    merged = merged.astype(BF16)
    x1s, hs = [], []
    for p in pieces:
        mix = jnp.dot(merged[p], wo_ref[...], preferred_element_type=F32)
        x1s.append(x_ref[0, p, :] + gt1_ref[0] * mix)
        hs.append(_rms_mod(x1s[-1], g2_ref[...], sc_ref[0], sh_ref[0]).astype(BF16))
    h = jnp.concatenate(hs, axis=0)

    def up(lhs, c):
        lo = c * FF_CHUNK
        fg = jnp.dot(lhs, wi_ref[:, lo:lo + FF_CHUNK], preferred_element_type=F32)
        fu = jnp.dot(lhs, wi_ref[:, D_FF + lo:D_FF + lo + FF_CHUNK], preferred_element_type=F32)
        return ((fg * _sigmoid(fg)) * fu).astype(BF16)

    n_chunks = D_FF // FF_CHUNK
    ff = None
    for c in range(n_chunks - 1):
        act = jnp.concatenate([up(part, c) for part in hs], axis=0) if c == 0 else up(h, c)
        part = jnp.dot(act, wf_ref[c * FF_CHUNK:(c + 1) * FF_CHUNK, :], preferred_element_type=F32)
        ff = part if ff is None else ff + part
    act = up(h, n_chunks - 1)
    for p, x1 in zip(pieces, x1s):
        ffp = ff[p] + jnp.dot(act[p], wf_ref[(n_chunks - 1) * FF_CHUNK:, :], preferred_element_type=F32)
        x2 = x1 + gt2_ref[0] * ffp
        y = x2 * lax.rsqrt(jnp.mean(x2 * x2, axis=-1, keepdims=True) + EPS)
        y_ref[0, p, :] = y * gf_ref[...]


def _mixffn(x, mods, g2, gf, z, rnn, attn_outs, weights, tm):
    B, S, D = x.shape
    mod_spec = pl.BlockSpec((1, 1, D), lambda b, i: (b, 0, 0))
    in_specs = [pl.BlockSpec((1, tm, D), lambda b, i: (b, i, 0))] + [mod_spec] * len(mods)
    in_specs += [
        _resident((1, D)),
        _resident((1, D)),
        pl.BlockSpec((1, tm, 2 * D_MODEL), lambda b, i: (b, i, 1)),
        pl.BlockSpec((1, tm, D_RNN), lambda b, i: (b, i, 0)),
    ]
    args = [x, *mods, g2, gf, z, rnn]
    for (o, l), (_, d) in zip(attn_outs, ATT_GROUPS):
        blk = pl.BlockSpec((1, d, tm // d, GROUP_W), lambda b, i: (b, 0, i, 0))
        in_specs += [blk, blk]
        args += [o, l]
    in_specs += [_resident(w.shape) for w in weights]
    args += list(weights)
    return pl.pallas_call(
        functools.partial(_mixffn_kernel, tm=tm),
        out_shape=jax.ShapeDtypeStruct((B, S, D), F32),
        grid=(B, S // tm),
        in_specs=in_specs,
        out_specs=pl.BlockSpec((1, tm, D), lambda b, i: (b, i, 0)),
        scratch_shapes=[pltpu.VMEM((GROUP_W // LANES, _fold_pitch(d) * (tm // d), LANES), F32)
                        for _, d in ATT_GROUPS[1:] for _ in range(2)],
        compiler_params=_params("parallel", "parallel"),
        name="mixffn",
    )(*args)


def _encode(x, mod, w, tm_in, tm_mix):
    sh1, sc1, gt1, sh2, sc2, gt2 = (mod[:, k] for k in range(N_MOD))
    outs = _inproj(x, sc1, sh1, w["norm1_g"], w["w_in"], tm_in)
    z, qkvs = outs[0], outs[1:]
    rnn = _rglru(z, w["conv_w"], w["conv_b"], w["rg_wa"], w["rg_ba"], w["rg_wx"], w["rg_bx"], w["rg_lambda"])
    attn_outs = [_attention(qkv) for qkv in qkvs]
    weights = (w["w_br_rnn"], w["w_br_attn"], w["w_out"], w["w_ffn_in"], w["w_ffn_out"])
    return _mixffn(x, (gt1, sc2, sh2, gt2), w["norm2_g"], w["final_g"], z, rnn, attn_outs, weights, tm_mix)


def kernel(x_prompt, x_sample, c_prompt, c_sample, w_ada, b_ada, norm1_g, w_in, conv_w, conv_b, rg_wa, rg_ba, rg_wx, rg_bx, rg_lambda, w_br_rnn, w_br_attn, w_out, norm2_g, w_ffn_in, w_ffn_out, final_g):
    assert w_ada.shape[0] == 1, "single layer"
    w = dict(
        norm1_g=norm1_g[0].reshape(1, -1),
        w_in=w_in[0].astype(BF16), conv_w=conv_w[0], conv_b=conv_b[0],
        rg_wa=rg_wa[0], rg_ba=rg_ba[0], rg_wx=rg_wx[0], rg_bx=rg_bx[0], rg_lambda=rg_lambda[0],
        w_br_rnn=w_br_rnn[0].astype(BF16), w_br_attn=w_br_attn[0].astype(BF16), w_out=w_out[0].astype(BF16),
        norm2_g=norm2_g[0].reshape(1, -1), w_ffn_in=w_ffn_in[0].astype(BF16),
        w_ffn_out=w_ffn_out[0].astype(BF16), final_g=final_g.reshape(1, -1),
    )
    tiles = dict(tm_in=512, tm_mix=512)
    n_prompt = c_prompt.shape[0]
    mod = _modulation(jnp.concatenate([c_prompt, c_sample], axis=0), w_ada[0], b_ada[0])
    return (_encode(x_prompt, mod[:n_prompt], w, **tiles), _encode(x_sample, mod[n_prompt:], w, **tiles))
```

```python
import functools

import jax
import jax.numpy as jnp
from jax import lax
from jax.experimental import pallas as pl
from jax.experimental.pallas import tpu as pltpu

F32 = jnp.float32
BF16 = jnp.bfloat16

D_MODEL = 1024
D_RNN = 1024
RG_BLOCKS = 16
RG_BW = D_RNN // RG_BLOCKS
RG_C = 8.0
CONV_W = 4
CONV_LEFT = 2
HEAD_DIM = 64
HEADS_PER_GROUP = 8
ATT_GROUPS = ((128, 1), (512, 4), (2048, 16))
N_GROUPS = len(ATT_GROUPS)
GROUP_W = HEADS_PER_GROUP * HEAD_DIM
ATT_W = N_GROUPS * GROUP_W
ROT_DIM = HEAD_DIM // 4
ROPE_THETA = 500000.0
D_FF = 2816
N_MOD = 6
EPS = 1e-6
NEG_INF = -1e30
LOG2E = 1.4426950408889634
IN_COLS = 2 * D_RNN + 3 * ATT_W + 2 * D_MODEL
RADIUS = 64

LANES = 128
SUBLANES = 8
BF16_ROWS = 2 * SUBLANES
VMEM_LIMIT_BYTES = 56 * 1024 * 1024

COL_CHUNK = 512
RNN_CG = 128
RNN_T = 256
RNN_TB = 128
RNN_SEG_EXTRA = 4
ATT_BQ = 128
ATT_KW = ATT_BQ + 2 * RADIUS
ATT_UNROLL = 8
ATT_AHEAD = 2
FF_CHUNK = 256
HEAD_PIECES = 2


def _resident(shape):
    nd = len(shape)
    return pl.BlockSpec(shape, lambda *_: (0,) * nd, pipeline_mode=pl.Buffered(1))


def _params(*sem):
    return pltpu.CompilerParams(dimension_semantics=sem, vmem_limit_bytes=VMEM_LIMIT_BYTES)


def _rms_mod(x, g, sc, sh):
    y = x * lax.rsqrt(jnp.mean(x * x, axis=-1, keepdims=True) + EPS)
    return y * (g * (1.0 + sc)) + sh


def _sigmoid(x):
    return 1.0 / (1.0 + jnp.exp2(x * -LOG2E))


def _gelu_tanh(x):
    return 0.5 * x * (1.0 + jnp.tanh(0.7978845608028654 * (x + 0.044715 * (x * x * x))))


def _mod_kernel(c_ref, w_ref, b_ref, o_ref):
    c = c_ref[...]
    s = (c * _sigmoid(c)).astype(BF16)
    o_ref[...] = jnp.dot(s, w_ref[...].astype(BF16), preferred_element_type=F32) + b_ref[...]


def _modulation(c, w_ada, b_ada):
    B = c.shape[0]
    rows = -(-B // BF16_ROWS) * BF16_ROWS
    cp = jnp.pad(c, ((0, rows - B), (0, 0)))
    out = pl.pallas_call(
        _mod_kernel,
        out_shape=jax.ShapeDtypeStruct((rows, N_MOD * D_MODEL), F32),
        grid=(N_MOD,),
        in_specs=[
            pl.BlockSpec((rows, D_MODEL), lambda j: (0, 0)),
            pl.BlockSpec((D_MODEL, D_MODEL), lambda j: (0, j)),
            pl.BlockSpec((1, D_MODEL), lambda j: (0, j)),
        ],
        out_specs=pl.BlockSpec((rows, D_MODEL), lambda j: (0, j)),
        compiler_params=_params("parallel"),
        name="mod",
    )(cp, w_ada, b_ada.reshape(1, -1))
    return out[:B].reshape(B, N_MOD, 1, D_MODEL)


def _fold_pitch(dil):
    return dil if dil % (2 * SUBLANES) else dil + SUBLANES


def _inproj_kernel(x_ref, sc_ref, sh_ref, g_ref, w_ref, cos_ref, sa_ref, sb_ref,
                   z_ref, q0_ref, q1_ref, q2_ref, fold_ref, *, tm):
    rows = tm // HEAD_PIECES
    pieces = [_rms_mod(x_ref[0, r0:r0 + rows, :], g_ref[...], sc_ref[0], sh_ref[0]).astype(BF16)
              for r0 in range(0, tm, rows)]
    h = jnp.concatenate(pieces, axis=0)
    issued = []

    def proj(j):
        w = w_ref[:, j * COL_CHUNK:(j + 1) * COL_CHUNK]
        issued.append(j)
        if len(issued) == 1:
            return jnp.concatenate([jnp.dot(part, w, preferred_element_type=F32) for part in pieces], axis=0)
        return jnp.dot(h, w, preferred_element_type=F32)

    def rope(v):
        parts = []
        for s in range(COL_CHUNK // LANES):
            p = v[:, s * LANES:(s + 1) * LANES]
            up = pltpu.roll(p, LANES - ROT_DIM // 2, axis=1)
            dn = pltpu.roll(p, ROT_DIM // 2, axis=1)
            parts.append(p * cos_ref[...] + up * sa_ref[...] + dn * sb_ref[...])
        return jnp.concatenate(parts, axis=1)

    qkv_refs = (q0_ref, q1_ref, q2_ref)

    def emit_folded(val, g, off):
        dil = ATT_GROUPS[g][1]
        ref = qkv_refs[g]
        if dil == 1:
            ref[0, 0, :, off:off + COL_CHUNK] = val.astype(BF16)
            return
        pitch = _fold_pitch(dil)
        for s in range(COL_CHUNK // LANES):
            slab = val[:, s * LANES:(s + 1) * LANES]
            if pitch == dil:
                fold_ref[s, 0:tm, :] = slab
            else:
                for m in range(tm // dil):
                    fold_ref[s, pitch * m:pitch * m + dil, :] = slab[dil * m:dil * (m + 1)]
        for r in range(dil):
            for s in range(COL_CHUNK // LANES):
                lo = off + s * LANES
                ref[0, r, :, lo:lo + LANES] = fold_ref[s, pl.ds(r, tm // dil, stride=pitch), :].astype(BF16)

    def emit_z(j_w, j_z, fn):
        z_ref[0, :, j_z * COL_CHUNK:(j_z + 1) * COL_CHUNK] = fn(proj(j_w)).astype(BF16)

    n_x = D_RNN // COL_CHUNK
    n_rnn = 2 * n_x
    n_gate0 = n_rnn + 3 * N_GROUPS
    def emit_q(g):
        emit_folded(rope(proj(n_rnn + g)) * (HEAD_DIM ** -0.5 * LOG2E), g, 0)

    def emit_k(g):
        emit_folded(rope(proj(n_rnn + N_GROUPS + g)), g, GROUP_W)

    def emit_v(g):
        emit_folded(proj(n_rnn + 2 * N_GROUPS + g), g, 2 * GROUP_W)

    plain = [functools.partial(emit_z, n_gate0 + j, n_rnn + j, lambda v: v) for j in range(2 * D_MODEL // COL_CHUNK)]
    plain += [functools.partial(emit_z, j, j, lambda v: v) for j in range(n_x)]
    heavy = [functools.partial(f, g) for g in range(N_GROUPS - 1, 0, -1) for f in (emit_q, emit_k, emit_v)]
    for step in range(max(len(plain), len(heavy))):
        if step < len(plain):
            plain[step]()
        if step < len(heavy):
            heavy[step]()
    for j in range(n_x):
        emit_z(n_x + j, n_x + j, _gelu_tanh)
    for f in (emit_q, emit_k, emit_v):
        f(0)


def _rope_tables(S):
    half = ROT_DIM // 2
    inv = ROPE_THETA ** (-(jnp.arange(0, ROT_DIM, 2, dtype=F32) / ROT_DIM))
    ang = jnp.arange(S, dtype=F32)[:, None] * inv[None, :]
    cos, sin = jnp.cos(ang), jnp.sin(ang)
    zeros = jnp.zeros((S, HEAD_DIM - ROT_DIM), F32)
    z8 = jnp.zeros((S, half), F32)
    c = jnp.concatenate([cos, cos, zeros + 1.0], axis=1)
    sa = jnp.concatenate([-sin, z8, zeros], axis=1)
    sb = jnp.concatenate([z8, sin, zeros], axis=1)
    rep = LANES // HEAD_DIM
    return tuple(jnp.tile(t, (1, rep)) for t in (c, sa, sb))


def _inproj(x, sc, sh, g, w_in, tm):
    B, S, D = x.shape
    cos, sa, sb = _rope_tables(S)
    dils = [d for _, d in ATT_GROUPS]
    assert S % tm == 0 and all(tm % (d * BF16_ROWS) == 0 for d in dils) and tm % (HEAD_PIECES * BF16_ROWS) == 0
    row = lambda b, i: (b, 0, 0)
    out_shape = [jax.ShapeDtypeStruct((B, S, 2 * D_RNN + 2 * D_MODEL), BF16)]
    out_specs = [pl.BlockSpec((1, tm, 2 * D_RNN + 2 * D_MODEL), lambda b, i: (b, i, 0))]
    for d in dils:
        out_shape.append(jax.ShapeDtypeStruct((B, d, S // d, 3 * GROUP_W), BF16))
        out_specs.append(pl.BlockSpec((1, d, tm // d, 3 * GROUP_W), lambda b, i: (b, 0, i, 0)))
    tab = pl.BlockSpec((tm, LANES), lambda b, i: (i, 0))
    return pl.pallas_call(
        functools.partial(_inproj_kernel, tm=tm),
        out_shape=out_shape,
        grid=(B, S // tm),
        in_specs=[
            pl.BlockSpec((1, tm, D), lambda b, i: (b, i, 0)),
            pl.BlockSpec((1, 1, D), row),
            pl.BlockSpec((1, 1, D), row),
            _resident((1, D)),
            _resident((D, IN_COLS)),
            tab, tab, tab,
        ],
        out_specs=out_specs,
        scratch_shapes=[pltpu.VMEM((COL_CHUNK // LANES, max(_fold_pitch(d) * (tm // d) for d in dils), LANES), F32)],
        compiler_params=_params("parallel", "parallel"),
        name="inproj",
    )(x, sc, sh, g, w_in, cos, sa, sb)


def _rnn_seg_len(S):
    assert S % (8 * SUBLANES) == 0
    return S // SUBLANES + RNN_SEG_EXTRA


def _rglru_kernel(x_ref, gate_ref, cw_ref, cb_ref, wg_ref, bg_ref, lam_ref, o_ref,
                  xnat_ref, hnat_ref, hloc_ref, acum_ref, xc_ref, *, S):
    C = RNN_CG
    T = RNN_T
    TB = RNN_TB
    PAD = SUBLANES
    seg = _rnn_seg_len(S)
    n_main = (seg - RNN_SEG_EXTRA) // TB
    rows = xnat_ref.shape[0]

    xnat_ref[0:PAD, :] = jnp.zeros((PAD, C), F32)
    xnat_ref[PAD + S:rows, :] = jnp.zeros((rows - PAD - S, C), F32)

    def fill(c, carry):
        r0 = pl.multiple_of(c * T, T)
        xnat_ref[pl.ds(r0 + PAD, T), :] = x_ref[0, pl.ds(r0, T), :].astype(F32)
        return carry

    lax.fori_loop(0, S // T, fill, 0)

    neg_lam = -lam_ref[0]
    softplus = jnp.maximum(neg_lam, 0.0) + jnp.log1p(jnp.exp(-jnp.abs(neg_lam)))
    half_coef2 = (-0.5 * RG_C * LOG2E) * softplus
    half_bias = 0.5 * bg_ref[0]
    cwb = [jnp.broadcast_to(cw_ref[k:k + 1, :], (SUBLANES, C)) for k in range(CONV_W)]
    cbb = jnp.broadcast_to(cb_ref[...], (SUBLANES, C))
    sub = lax.broadcasted_iota(jnp.int32, (SUBLANES, C), 0)
    steps_left = S - sub * seg

    def step_rows(t):
        return (pl.ds(PAD + t, SUBLANES, stride=seg), slice(None))

    def block_gates(t0, n, direction, conv):
        rows = pl.ds(pl.multiple_of(t0 * SUBLANES, SUBLANES), n * SUBLANES)
        if conv == "reuse":
            xc = xc_ref[rows, :]
        else:
            taps = [xnat_ref[step_rows(t0 + m - CONV_LEFT)] for m in range(n + CONV_W - 1)]
            xcs = []
            for j in range(n):
                acc = cbb + taps[j] * cwb[0]
                for k in range(1, CONV_W):
                    acc = acc + taps[j + k] * cwb[k]
                xcs.append(acc)
            xc = jnp.concatenate(xcs, axis=0)
            if conv == "keep":
                xc_ref[rows, :] = xc
        lo = direction * 2 * C
        gz = jnp.dot(xc.astype(BF16), wg_ref[0, :, lo:lo + 2 * C], preferred_element_type=F32)
        t = jnp.tanh(0.5 * gz + half_bias[:, lo:lo + 2 * C])
        half_coef = half_coef2[direction:direction + 1]
        a = jnp.exp2(half_coef * t[:, :C] + half_coef)
        y = 1.0 - a * a
        root = jnp.where(y > 0.0, y * lax.rsqrt(y), 0.0)
        half_xc = 0.5 * xc
        u = root * (half_xc * t[:, C:] + half_xc)
        return a, u

    def scan_block(t0, n, direction, carry, past_end=False, conv="own"):
        h, acc = carry
        a, u = block_gates(t0, n, direction, conv)
        order = range(n) if direction == 0 else range(n - 1, -1, -1)
        for j in order:
            aj = a[j * SUBLANES:(j + 1) * SUBLANES]
            uj = u[j * SUBLANES:(j + 1) * SUBLANES]
            if direction == 1 and past_end:
                uj = jnp.where(t0 + j < steps_left, uj, 0.0)
            h = aj * h + uj
            acc = aj * acc
            row = pl.multiple_of((t0 + j) * SUBLANES, SUBLANES)
            hloc_ref[direction, pl.ds(row, SUBLANES), :] = h
            acum_ref[direction, pl.ds(row, SUBLANES), :] = acc
        return h, acc

    def entry_states(h_tot, a_tot, direction):
        c = jnp.zeros((1, C), F32)
        out = jnp.zeros((SUBLANES, C), F32)
        order = range(SUBLANES) if direction == 0 else range(SUBLANES - 1, -1, -1)
        for s in order:
            out = jnp.where(sub == s, c, out)
            c = a_tot[s:s + 1] * c + h_tot[s:s + 1]
        return out

    def fix_block(t0, n, entries):
        for j in range(n):
            row = pl.multiple_of((t0 + j) * SUBLANES, SUBLANES)
            h = None
            for direction in (0, 1):
                part = (hloc_ref[direction, pl.ds(row, SUBLANES), :]
                        + acum_ref[direction, pl.ds(row, SUBLANES), :] * entries[direction])
                h = part if h is None else h + part
            hnat_ref[step_rows(t0 + j)] = h

    tail0 = n_main * TB
    init = (jnp.zeros((SUBLANES, C), F32), jnp.ones((SUBLANES, C), F32))

    def block_start(b):
        return b * TB if isinstance(b, int) else pl.multiple_of(b * TB, TB)

    def main(b, carry, past_end=False, conv="keep"):
        fwd = scan_block(block_start(b), TB, 0, carry[0], conv=conv)
        bwd = scan_block(block_start(n_main - 1 - b), TB, 1, carry[1], past_end, conv=conv)
        return fwd, bwd

    assert TB >= (SUBLANES - 1) * RNN_SEG_EXTRA and n_main >= 2 and n_main % 2 == 0
    bwd_tail = scan_block(tail0, RNN_SEG_EXTRA, 1, init, past_end=True)
    carry = main(0, (init, bwd_tail), past_end=True)
    carry = lax.fori_loop(1, n_main // 2, main, carry)
    fwd_tot, bwd_tot = lax.fori_loop(n_main // 2, n_main, functools.partial(main, conv="reuse"), carry)
    fwd_tot = scan_block(tail0, RNN_SEG_EXTRA, 0, fwd_tot)
    entries = (entry_states(*fwd_tot, 0), entry_states(*bwd_tot, 1))

    def fix(b, carry):
        fix_block(pl.multiple_of(b * TB, TB), TB, entries)
        return carry

    lax.fori_loop(0, n_main, fix, 0)
    fix_block(tail0, RNN_SEG_EXTRA, entries)

    def finish(c, carry):
        r0 = pl.multiple_of(c * T, T)
        gate = gate_ref[0, pl.ds(r0, T), :].astype(F32)
        o_ref[0, pl.ds(r0, T), :] = (hnat_ref[pl.ds(r0 + PAD, T), :] * gate).astype(BF16)
        return carry

    lax.fori_loop(0, S // T, finish, 0)


def _block_diag_pairs(w):
    per = RNN_CG // RG_BW
    w = w.reshape(RG_BLOCKS // per, per, RG_BW, RG_BW)
    rows = []
    for p in range(per):
        cols = [w[:, p] if q == p else jnp.zeros_like(w[:, p]) for q in range(per)]
        rows.append(jnp.concatenate(cols, axis=-1))
    return jnp.concatenate(rows, axis=1)


def _rglru(z, conv_w, conv_b, rg_wa, rg_ba, rg_wx, rg_bx, rg_lambda):
    B, S, _ = z.shape
    C = RNN_CG
    n_grp = D_RNN // C
    wg = jnp.concatenate([_block_diag_pairs(rg_wa[0]), _block_diag_pairs(rg_wx[0]),
                          _block_diag_pairs(rg_wa[1]), _block_diag_pairs(rg_wx[1])], axis=-1).astype(BF16)
    bg = jnp.concatenate([rg_ba[0].reshape(n_grp, 1, C), rg_bx[0].reshape(n_grp, 1, C),
                          rg_ba[1].reshape(n_grp, 1, C), rg_bx[1].reshape(n_grp, 1, C)], axis=-1)
    lam = rg_lambda.reshape(2, n_grp, C).transpose(1, 0, 2)
    step_rows = SUBLANES * _rnn_seg_len(S)
    nat_rows = step_rows + 2 * SUBLANES
    return pl.pallas_call(
        functools.partial(_rglru_kernel, S=S),
        out_shape=jax.ShapeDtypeStruct((B, S, D_RNN), BF16),
        grid=(B, n_grp),
        in_specs=[
            pl.BlockSpec((1, S, C), lambda b, c: (b, 0, c)),
            pl.BlockSpec((1, S, C), lambda b, c: (b, 0, n_grp + c)),
            pl.BlockSpec((CONV_W, C), lambda b, c: (0, c)),
            pl.BlockSpec((1, C), lambda b, c: (0, c)),
            pl.BlockSpec((1, C, 4 * C), lambda b, c: (c, 0, 0)),
            pl.BlockSpec((1, 1, 4 * C), lambda b, c: (c, 0, 0)),
            pl.BlockSpec((1, 2, C), lambda b, c: (c, 0, 0)),
        ],
        out_specs=pl.BlockSpec((1, S, C), lambda b, c: (b, 0, c)),
        scratch_shapes=[pltpu.VMEM((nat_rows, C), F32), pltpu.VMEM((nat_rows, C), F32),
                        pltpu.VMEM((2, step_rows, C), F32), pltpu.VMEM((2, step_rows, C), F32),
                        pltpu.VMEM((step_rows, C), F32)],
        compiler_params=_params("parallel", "parallel"),
        name="rglru",
    )(z, z, conv_w, conv_b.reshape(1, -1), wg, bg, lam)


def _attn_kernel(q_ref, k_ref, v_ref, o_ref, l_ref, cap_ref, *, L, n_res):
    BQ, KW = ATT_BQ, ATT_KW
    lane = lax.broadcasted_iota(jnp.int32, (BQ, LANES), 1)
    first = lane < HEAD_DIM
    first_kw = lax.broadcasted_iota(jnp.int32, (KW, LANES), 1) < HEAD_DIM
    ones_kw = jnp.concatenate([first_kw, jnp.logical_not(first_kw)], axis=0).astype(BF16)
    rel = lax.broadcasted_iota(jnp.int32, (BQ, KW), 0) - lax.broadcasted_iota(jnp.int32, (BQ, KW), 1)
    n_blocks = L // BQ
    total = n_res * n_blocks
    assert n_blocks & (n_blocks - 1) == 0 and total % ATT_UNROLL == 0

    for j in range(3):
        cap = jnp.where(jnp.abs(rel + j * RADIUS) <= RADIUS, jnp.inf, NEG_INF).astype(F32)
        cap_ref[j, 0:BQ, :] = cap
        cap_ref[j, BQ:2 * BQ, :] = cap

    def scores(idx):
        r = lax.shift_right_logical(idx, n_blocks.bit_length() - 1)
        q0 = pl.multiple_of(jnp.bitwise_and(idx, n_blocks - 1) * BQ, BQ)
        k0 = pl.multiple_of(jnp.clip(q0 - RADIUS, 0, L - KW), RADIUS)
        q = q_ref[0, r, pl.ds(q0, BQ), :]
        k = k_ref[0, r, pl.ds(k0, KW), :]
        zero = jnp.zeros_like(q)
        q2 = jnp.concatenate([jnp.where(first, q, zero), jnp.where(first, zero, q)], axis=0)
        s = lax.dot_general(q2, k, (((1,), (1,)), ((), ())), preferred_element_type=F32)
        return r, q0, k0, s

    def finish(r, q0, k0, s):
        cap = cap_ref[lax.shift_right_logical(q0 - k0, RADIUS.bit_length() - 1)]
        s = jnp.minimum(s, cap)
        m = jnp.max(s, axis=-1, keepdims=True)
        p = jnp.exp2(s - m).astype(BF16)
        v = v_ref[0, r, pl.ds(k0, KW), :]
        zero = jnp.zeros_like(v)
        values = jnp.concatenate([jnp.where(first_kw, v, zero), jnp.where(first_kw, zero, v)], axis=0)
        pv = jnp.dot(jnp.concatenate([p[:BQ], p[BQ:]], axis=1), jnp.concatenate([values, ones_kw], axis=1),
                     preferred_element_type=F32)
        num, den = pv[:, :LANES], pv[:, LANES:]
        top = jnp.where(first, m[:BQ], m[BQ:])
        o_ref[0, r, pl.ds(q0, BQ), :] = (num / den).astype(BF16)
        l_ref[0, r, pl.ds(q0, BQ), :] = top + jnp.log2(den)

    def body(it, carry):
        pending = [scores(it * ATT_UNROLL + j) for j in range(ATT_AHEAD)]
        for j in range(ATT_UNROLL):
            if j + ATT_AHEAD < ATT_UNROLL:
                pending.append(scores(it * ATT_UNROLL + j + ATT_AHEAD))
            finish(*pending.pop(0))
        return carry

    lax.fori_loop(0, total // ATT_UNROLL, body, 0)


def _attention(qkv):
    B, dil, L, _ = qkv.shape
    n_pair = GROUP_W // LANES
    spec = lambda off: pl.BlockSpec((1, dil, L, LANES), lambda b, p: (b, 0, 0, off + p))
    return pl.pallas_call(
        functools.partial(_attn_kernel, L=L, n_res=dil),
        out_shape=[jax.ShapeDtypeStruct((B, dil, L, GROUP_W), BF16),
                   jax.ShapeDtypeStruct((B, dil, L, GROUP_W), F32)],
        grid=(B, n_pair),
        in_specs=[spec(0), spec(n_pair), spec(2 * n_pair)],
        out_specs=[spec(0), spec(0)],
        scratch_shapes=[pltpu.VMEM((3, 2 * ATT_BQ, ATT_KW), F32)],
        compiler_params=_params("parallel", "parallel"),
        name="attn",
    )(qkv, qkv, qkv)


def _mixffn_kernel(x_ref, gt1_ref, sc_ref, sh_ref, gt2_ref, g2_ref, gf_ref, mg_ref, rnn_ref,
                   o0_ref, l0_ref, o1_ref, l1_ref, o2_ref, l2_ref,
                   wr_ref, wa_ref, wo_ref, wi_ref, wf_ref, y_ref,
                   fo1_ref, fl1_ref, fo2_ref, fl2_ref, *, tm):
    n_tiles = GROUP_W // LANES

    def unfold(src_ref, dst_ref, dil):
        pitch = _fold_pitch(dil)
        for r in range(dil):
            for s in range(n_tiles):
                dst_ref[s, pl.ds(r, tm // dil, stride=pitch), :] = src_ref[0, r, :, s * LANES:(s + 1) * LANES].astype(F32)
        if pitch == dil:
            return jnp.concatenate([dst_ref[s, 0:tm, :] for s in range(n_tiles)], axis=1)
        return jnp.concatenate(
            [jnp.concatenate([dst_ref[s, pitch * m:pitch * m + dil, :] for m in range(tm // dil)], axis=0)
             for s in range(n_tiles)], axis=1)

    o1 = unfold(o1_ref, fo1_ref, ATT_GROUPS[1][1])
    l1 = unfold(l1_ref, fl1_ref, ATT_GROUPS[1][1])
    o2 = unfold(o2_ref, fo2_ref, ATT_GROUPS[2][1])
    l2 = unfold(l2_ref, fl2_ref, ATT_GROUPS[2][1])
    lses = (l0_ref[0, 0], l1, l2)
    outs = (o0_ref[0, 0].astype(F32), o1, o2)

    top = jnp.maximum(jnp.maximum(lses[0], lses[1]), lses[2])
    es = [jnp.exp2(l - top) for l in lses]
    att = (es[0] * outs[0] + es[1] * outs[1] + es[2] * outs[2]) / (es[0] + es[1] + es[2])

    br_a = jnp.dot(att.astype(BF16), wa_ref[...], preferred_element_type=F32)
    br_r = jnp.dot(rnn_ref[0], wr_ref[...], preferred_element_type=F32)
    gate = _sigmoid(mg_ref[0].astype(F32))
    merged = gate[:, :D_MODEL] * br_r + gate[:, D_MODEL:] * br_a
    rows = tm // HEAD_PIECES
    pieces = [slice(r0, r0 + rows) for r0 in range(0, tm, rows)]
    merged = merged.astype(BF16)
    x1s, hs = [], []
    for p in pieces:
        mix = jnp.dot(merged[p], wo_ref[...], preferred_element_type=F32)
        x1s.append(x_ref[0, p, :] + gt1_ref[0] * mix)
        hs.append(_rms_mod(x1s[-1], g2_ref[...], sc_ref[0], sh_ref[0]).astype(BF16))
    h = jnp.concatenate(hs, axis=0)

    def up(lhs, c):
        lo = c * FF_CHUNK
        fg = jnp.dot(lhs, wi_ref[:, lo:lo + FF_CHUNK], preferred_element_type=F32)
        fu = jnp.dot(lhs, wi_ref[:, D_FF + lo:D_FF + lo + FF_CHUNK], preferred_element_type=F32)
        return ((fg * _sigmoid(fg)) * fu).astype(BF16)

    n_chunks = D_FF // FF_CHUNK
    ff = None
    for c in range(n_chunks - 1):
        act = jnp.concatenate([up(part, c) for part in hs], axis=0) if c == 0 else up(h, c)
        part = jnp.dot(act, wf_ref[c * FF_CHUNK:(c + 1) * FF_CHUNK, :], preferred_element_type=F32)
        ff = part if ff is None else ff + part
    act = up(h, n_chunks - 1)
    for p, x1 in zip(pieces, x1s):
        ffp = ff[p] + jnp.dot(act[p], wf_ref[(n_chunks - 1) * FF_CHUNK:, :], preferred_element_type=F32)
        x2 = x1 + gt2_ref[0] * ffp
        y = x2 * lax.rsqrt(jnp.mean(x2 * x2, axis=-1, keepdims=True) + EPS)
        y_ref[0, p, :] = y * gf_ref[...]


def _mixffn(x, mods, g2, gf, z, rnn, attn_outs, weights, tm):
    B, S, D = x.shape
    mod_spec = pl.BlockSpec((1, 1, D), lambda b, i: (b, 0, 0))
    in_specs = [pl.BlockSpec((1, tm, D), lambda b, i: (b, i, 0))] + [mod_spec] * len(mods)
    in_specs += [
        _resident((1, D)),
        _resident((1, D)),
        pl.BlockSpec((1, tm, 2 * D_MODEL), lambda b, i: (b, i, 1)),
        pl.BlockSpec((1, tm, D_RNN), lambda b, i: (b, i, 0)),
    ]
    args = [x, *mods, g2, gf, z, rnn]
    for (o, l), (_, d) in zip(attn_outs, ATT_GROUPS):
        blk = pl.BlockSpec((1, d, tm // d, GROUP_W), lambda b, i: (b, 0, i, 0))
        in_specs += [blk, blk]
        args += [o, l]
    in_specs += [_resident(w.shape) for w in weights]
    args += list(weights)
    return pl.pallas_call(
        functools.partial(_mixffn_kernel, tm=tm),
        out_shape=jax.ShapeDtypeStruct((B, S, D), F32),
        grid=(B, S // tm),
        in_specs=in_specs,
        out_specs=pl.BlockSpec((1, tm, D), lambda b, i: (b, i, 0)),
        scratch_shapes=[pltpu.VMEM((GROUP_W // LANES, _fold_pitch(d) * (tm // d), LANES), F32)
                        for _, d in ATT_GROUPS[1:] for _ in range(2)],
        compiler_params=_params("parallel", "parallel"),
        name="mixffn",
    )(*args)


def _encode(x, mod, w, tm_in, tm_mix):
    sh1, sc1, gt1, sh2, sc2, gt2 = (mod[:, k] for k in range(N_MOD))
    outs = _inproj(x, sc1, sh1, w["norm1_g"], w["w_in"], tm_in)
    z, qkvs = outs[0], outs[1:]
    rnn = _rglru(z, w["conv_w"], w["conv_b"], w["rg_wa"], w["rg_ba"], w["rg_wx"], w["rg_bx"], w["rg_lambda"])
    attn_outs = [_attention(qkv) for qkv in qkvs]
    weights = (w["w_br_rnn"], w["w_br_attn"], w["w_out"], w["w_ffn_in"], w["w_ffn_out"])
    return _mixffn(x, (gt1, sc2, sh2, gt2), w["norm2_g"], w["final_g"], z, rnn, attn_outs, weights, tm_mix)


def kernel(x_prompt, x_sample, c_prompt, c_sample, w_ada, b_ada, norm1_g, w_in, conv_w, conv_b, rg_wa, rg_ba, rg_wx, rg_bx, rg_lambda, w_br_rnn, w_br_attn, w_out, norm2_g, w_ffn_in, w_ffn_out, final_g):
    assert w_ada.shape[0] == 1, "single layer"
    w = dict(
        norm1_g=norm1_g[0].reshape(1, -1),
        w_in=w_in[0].astype(BF16), conv_w=conv_w[0], conv_b=conv_b[0],
        rg_wa=rg_wa[0], rg_ba=rg_ba[0], rg_wx=rg_wx[0], rg_bx=rg_bx[0], rg_lambda=rg_lambda[0],
        w_br_rnn=w_br_rnn[0].astype(BF16), w_br_attn=w_br_attn[0].astype(BF16), w_out=w_out[0].astype(BF16),
        norm2_g=norm2_g[0].reshape(1, -1), w_ffn_in=w_ffn_in[0].astype(BF16),
        w_ffn_out=w_ffn_out[0].astype(BF16), final_g=final_g.reshape(1, -1),
    )
    tiles = dict(tm_in=512, tm_mix=512)
    n_prompt = c_prompt.shape[0]
    mod = _modulation(jnp.concatenate([c_prompt, c_sample], axis=0), w_ada[0], b_ada[0])
    return (_encode(x_prompt, mod[:n_prompt], w, **tiles), _encode(x_sample, mod[n_prompt:], w, **tiles))
```

```python
import functools

import jax
import jax.numpy as jnp
from jax import lax
from jax.experimental import pallas as pl
from jax.experimental.pallas import tpu as pltpu

F32 = jnp.float32
BF16 = jnp.bfloat16

D_MODEL = 1024
D_RNN = 1024
RG_BLOCKS = 16
RG_BW = D_RNN // RG_BLOCKS
RG_C = 8.0
CONV_W = 4
CONV_LEFT = 2
HEAD_DIM = 64
HEADS_PER_GROUP = 8
ATT_GROUPS = ((128, 1), (512, 4), (2048, 16))
N_GROUPS = len(ATT_GROUPS)
GROUP_W = HEADS_PER_GROUP * HEAD_DIM
ATT_W = N_GROUPS * GROUP_W
ROT_DIM = HEAD_DIM // 4
ROPE_THETA = 500000.0
D_FF = 2816
N_MOD = 6
EPS = 1e-6
NEG_INF = -1e30
LOG2E = 1.4426950408889634
IN_COLS = 2 * D_RNN + 3 * ATT_W + 2 * D_MODEL
RADIUS = 64

LANES = 128
SUBLANES = 8
BF16_ROWS = 2 * SUBLANES
VMEM_LIMIT_BYTES = 56 * 1024 * 1024

COL_CHUNK = 512
FOLD_STRIDE = 4
RNN_CG = 128
RNN_T = 256
RNN_TB = 128
RNN_SEG_EXTRA = 4
ATT_BQ = 128
ATT_KW = ATT_BQ + 2 * RADIUS
ATT_UNROLL = 8
ATT_AHEAD = 2
FF_CHUNK = 256
HEAD_PIECES = 2


def _resident(shape):
    nd = len(shape)
    return pl.BlockSpec(shape, lambda *_: (0,) * nd, pipeline_mode=pl.Buffered(1))


def _params(*sem):
    return pltpu.CompilerParams(dimension_semantics=sem, vmem_limit_bytes=VMEM_LIMIT_BYTES)


def _rms_mod(x, g, sc, sh):
    y = x * lax.rsqrt(jnp.mean(x * x, axis=-1, keepdims=True) + EPS)
    return y * (g * (1.0 + sc)) + sh


def _sigmoid(x):
    return 1.0 / (1.0 + jnp.exp2(x * -LOG2E))


def _gelu_tanh(x):
    return 0.5 * x * (1.0 + jnp.tanh(0.7978845608028654 * (x + 0.044715 * (x * x * x))))


def _mod_kernel(c_ref, w_ref, b_ref, o_ref):
    c = c_ref[...]
    s = (c * _sigmoid(c)).astype(BF16)
    o_ref[...] = jnp.dot(s, w_ref[...].astype(BF16), preferred_element_type=F32) + b_ref[...]


def _modulation(c, w_ada, b_ada):
    B = c.shape[0]
    rows = -(-B // BF16_ROWS) * BF16_ROWS
    cp = jnp.pad(c, ((0, rows - B), (0, 0)))
    out = pl.pallas_call(
        _mod_kernel,
        out_shape=jax.ShapeDtypeStruct((rows, N_MOD * D_MODEL), F32),
        grid=(N_MOD,),
        in_specs=[
            pl.BlockSpec((rows, D_MODEL), lambda j: (0, 0)),
            pl.BlockSpec((D_MODEL, D_MODEL), lambda j: (0, j)),
            pl.BlockSpec((1, D_MODEL), lambda j: (0, j)),
        ],
        out_specs=pl.BlockSpec((rows, D_MODEL), lambda j: (0, j)),
        compiler_params=_params("parallel"),
        name="mod",
    )(cp, w_ada, b_ada.reshape(1, -1))
    return out[:B].reshape(B, N_MOD, 1, D_MODEL)


def _fold_pitch(dil):
    return dil if dil % (2 * SUBLANES) else dil + SUBLANES


def _inproj_kernel(x_ref, sc_ref, sh_ref, g_ref, w_ref, cos_ref, sa_ref, sb_ref,
                   z_ref, q0_ref, q1_ref, q2_ref, fold_ref, fold2_ref, *, tm):
    rows = tm // HEAD_PIECES
    pieces = [_rms_mod(x_ref[0, r0:r0 + rows, :], g_ref[...], sc_ref[0], sh_ref[0]).astype(BF16)
              for r0 in range(0, tm, rows)]
    h = jnp.concatenate(pieces, axis=0)
    issued = []

    def proj(j):
        w = w_ref[:, j * COL_CHUNK:(j + 1) * COL_CHUNK]
        issued.append(j)
        if len(issued) == 1:
            return jnp.concatenate([jnp.dot(part, w, preferred_element_type=F32) for part in pieces], axis=0)
        return jnp.dot(h, w, preferred_element_type=F32)

    def rope(v):
        parts = []
        for s in range(COL_CHUNK // LANES):
            p = v[:, s * LANES:(s + 1) * LANES]
            up = pltpu.roll(p, LANES - ROT_DIM // 2, axis=1)
            dn = pltpu.roll(p, ROT_DIM // 2, axis=1)
            parts.append(p * cos_ref[...] + up * sa_ref[...] + dn * sb_ref[...])
        return jnp.concatenate(parts, axis=1)

    qkv_refs = (q0_ref, q1_ref, q2_ref)

    def emit_folded(val, g, off):
        dil = ATT_GROUPS[g][1]
        ref = qkv_refs[g]
        if dil == 1:
            ref[0, 0, :, off:off + COL_CHUNK] = val.astype(BF16)
            return
        inner = dil // FOLD_STRIDE
        assert dil in (FOLD_STRIDE, FOLD_STRIDE ** 2)
        for s in range(COL_CHUNK // LANES):
            fold_ref[s, 0:tm, :] = val[:, s * LANES:(s + 1) * LANES]
        n = tm // FOLD_STRIDE
        for b in range(FOLD_STRIDE):
            for s in range(COL_CHUNK // LANES):
                lo = off + s * LANES
                part = fold_ref[s, pl.ds(b, n, stride=FOLD_STRIDE), :]
                if inner == 1:
                    ref[0, b, :, lo:lo + LANES] = part.astype(BF16)
                else:
                    fold2_ref[s, b * n:(b + 1) * n, :] = part
        if inner > 1:
            for b in range(FOLD_STRIDE):
                for e in range(inner):
                    for s in range(COL_CHUNK // LANES):
                        lo = off + s * LANES
                        part = fold2_ref[s, pl.ds(b * n + e, n // inner, stride=inner), :]
                        ref[0, inner * e + b, :, lo:lo + LANES] = part.astype(BF16)

    def emit_z(j_w, j_z, fn):
        z_ref[0, :, j_z * COL_CHUNK:(j_z + 1) * COL_CHUNK] = fn(proj(j_w)).astype(BF16)

    n_x = D_RNN // COL_CHUNK
    n_rnn = 2 * n_x
    n_gate0 = n_rnn + 3 * N_GROUPS
    for j in range(2 * D_MODEL // COL_CHUNK):
        emit_z(n_gate0 + j, n_rnn + j, lambda v: v)
        if j < n_x:
            emit_z(n_x + j, n_x + j, _gelu_tanh)
    for g in range(N_GROUPS):
        emit_folded(rope(proj(n_rnn + g)) * (HEAD_DIM ** -0.5 * LOG2E), g, 0)
        emit_folded(rope(proj(n_rnn + N_GROUPS + g)), g, GROUP_W)
        emit_folded(proj(n_rnn + 2 * N_GROUPS + g), g, 2 * GROUP_W)
    for j in range(n_x):
        emit_z(j, j, lambda v: v)


def _rope_tables(S):
    half = ROT_DIM // 2
    inv = ROPE_THETA ** (-(jnp.arange(0, ROT_DIM, 2, dtype=F32) / ROT_DIM))
    ang = jnp.arange(S, dtype=F32)[:, None] * inv[None, :]
    cos, sin = jnp.cos(ang), jnp.sin(ang)
    zeros = jnp.zeros((S, HEAD_DIM - ROT_DIM), F32)
    z8 = jnp.zeros((S, half), F32)
    c = jnp.concatenate([cos, cos, zeros + 1.0], axis=1)
    sa = jnp.concatenate([-sin, z8, zeros], axis=1)
    sb = jnp.concatenate([z8, sin, zeros], axis=1)
    rep = LANES // HEAD_DIM
    return tuple(jnp.tile(t, (1, rep)) for t in (c, sa, sb))


def _inproj(x, sc, sh, g, w_in, tm):
    B, S, D = x.shape
    cos, sa, sb = _rope_tables(S)
    dils = [d for _, d in ATT_GROUPS]
    assert S % tm == 0 and all(tm % (d * BF16_ROWS) == 0 for d in dils) and tm % (HEAD_PIECES * BF16_ROWS) == 0
    row = lambda b, i: (b, 0, 0)
    out_shape = [jax.ShapeDtypeStruct((B, S, 2 * D_RNN + 2 * D_MODEL), BF16)]
    out_specs = [pl.BlockSpec((1, tm, 2 * D_RNN + 2 * D_MODEL), lambda b, i: (b, i, 0))]
    for d in dils:
        out_shape.append(jax.ShapeDtypeStruct((B, d, S // d, 3 * GROUP_W), BF16))
        out_specs.append(pl.BlockSpec((1, d, tm // d, 3 * GROUP_W), lambda b, i: (b, 0, i, 0)))
    tab = pl.BlockSpec((tm, LANES), lambda b, i: (i, 0))
    return pl.pallas_call(
        functools.partial(_inproj_kernel, tm=tm),
        out_shape=out_shape,
        grid=(B, S // tm),
        in_specs=[
            pl.BlockSpec((1, tm, D), lambda b, i: (b, i, 0)),
            pl.BlockSpec((1, 1, D), row),
            pl.BlockSpec((1, 1, D), row),
            _resident((1, D)),
            _resident((D, IN_COLS)),
            tab, tab, tab,
        ],
        out_specs=out_specs,
        scratch_shapes=[pltpu.VMEM((COL_CHUNK // LANES, tm, LANES), F32)] * 2,
        compiler_params=_params("parallel", "parallel"),
        name="inproj",
    )(x, sc, sh, g, w_in, cos, sa, sb)


def _rnn_seg_len(S):
    assert S % (8 * SUBLANES) == 0
    return S // SUBLANES + RNN_SEG_EXTRA


def _rglru_kernel(x_ref, gate_ref, cw_ref, cb_ref, wg_ref, bg_ref, lam_ref, o_ref,
                  xnat_ref, hnat_ref, hloc_ref, acum_ref, xc_ref, *, S):
    C = RNN_CG
    T = RNN_T
    TB = RNN_TB
    PAD = SUBLANES
    seg = _rnn_seg_len(S)
    n_main = (seg - RNN_SEG_EXTRA) // TB
    rows = xnat_ref.shape[0]

    xnat_ref[0:PAD, :] = jnp.zeros((PAD, C), F32)
    xnat_ref[PAD + S:rows, :] = jnp.zeros((rows - PAD - S, C), F32)

    def fill(c, carry):
        r0 = pl.multiple_of(c * T, T)
        xnat_ref[pl.ds(r0 + PAD, T), :] = x_ref[0, pl.ds(r0, T), :].astype(F32)
        return carry

    lax.fori_loop(0, S // T, fill, 0)

    neg_lam = -lam_ref[0]
    softplus = jnp.maximum(neg_lam, 0.0) + jnp.log1p(jnp.exp(-jnp.abs(neg_lam)))
    half_coef2 = (-0.5 * RG_C * LOG2E) * softplus
    half_bias = 0.5 * bg_ref[0]
    cwb = [jnp.broadcast_to(cw_ref[k:k + 1, :], (SUBLANES, C)) for k in range(CONV_W)]
    cbb = jnp.broadcast_to(cb_ref[...], (SUBLANES, C))
    sub = lax.broadcasted_iota(jnp.int32, (SUBLANES, C), 0)
    steps_left = S - sub * seg

    def step_rows(t):
        return (pl.ds(PAD + t, SUBLANES, stride=seg), slice(None))

    def block_gates(t0, n, direction, conv):
        rows = pl.ds(pl.multiple_of(t0 * SUBLANES, SUBLANES), n * SUBLANES)
        if conv == "reuse":
            xc = xc_ref[rows, :]
        else:
            taps = [xnat_ref[step_rows(t0 + m - CONV_LEFT)] for m in range(n + CONV_W - 1)]
            xcs = []
            for j in range(n):
                acc = cbb + taps[j] * cwb[0]
                for k in range(1, CONV_W):
                    acc = acc + taps[j + k] * cwb[k]
                xcs.append(acc)
            xc = jnp.concatenate(xcs, axis=0)
            if conv == "keep":
                xc_ref[rows, :] = xc
        lo = direction * 2 * C
        gz = jnp.dot(xc.astype(BF16), wg_ref[0, :, lo:lo + 2 * C], preferred_element_type=F32)
        t = jnp.tanh(0.5 * gz + half_bias[:, lo:lo + 2 * C])
        half_coef = half_coef2[direction:direction + 1]
        a = jnp.exp2(half_coef * t[:, :C] + half_coef)
        y = 1.0 - a * a
        root = jnp.where(y > 0.0, y * lax.rsqrt(y), 0.0)
        half_xc = 0.5 * xc
        u = root * (half_xc * t[:, C:] + half_xc)
        return a, u

    def scan_block(t0, n, direction, carry, past_end=False, conv="own"):
        h, acc = carry
        a, u = block_gates(t0, n, direction, conv)
        order = range(n) if direction == 0 else range(n - 1, -1, -1)
        for j in order:
            aj = a[j * SUBLANES:(j + 1) * SUBLANES]
            uj = u[j * SUBLANES:(j + 1) * SUBLANES]
            if direction == 1 and past_end:
                uj = jnp.where(t0 + j < steps_left, uj, 0.0)
            h = aj * h + uj
            acc = aj * acc
            row = pl.multiple_of((t0 + j) * SUBLANES, SUBLANES)
            hloc_ref[direction, pl.ds(row, SUBLANES), :] = h
            acum_ref[direction, pl.ds(row, SUBLANES), :] = acc
        return h, acc

    def entry_states(h_tot, a_tot, direction):
        c = jnp.zeros((1, C), F32)
        out = jnp.zeros((SUBLANES, C), F32)
        order = range(SUBLANES) if direction == 0 else range(SUBLANES - 1, -1, -1)
        for s in order:
            out = jnp.where(sub == s, c, out)
            c = a_tot[s:s + 1] * c + h_tot[s:s + 1]
        return out

    def fix_block(t0, n, entries):
        for j in range(n):
            row = pl.multiple_of((t0 + j) * SUBLANES, SUBLANES)
            h = None
            for direction in (0, 1):
                part = (hloc_ref[direction, pl.ds(row, SUBLANES), :]
                        + acum_ref[direction, pl.ds(row, SUBLANES), :] * entries[direction])
                h = part if h is None else h + part
            hnat_ref[step_rows(t0 + j)] = h

    tail0 = n_main * TB
    init = (jnp.zeros((SUBLANES, C), F32), jnp.ones((SUBLANES, C), F32))

    def block_start(b):
        return b * TB if isinstance(b, int) else pl.multiple_of(b * TB, TB)

    def main(b, carry, past_end=False, conv="keep"):
        fwd = scan_block(block_start(b), TB, 0, carry[0], conv=conv)
        bwd = scan_block(block_start(n_main - 1 - b), TB, 1, carry[1], past_end, conv=conv)
        return fwd, bwd

    assert TB >= (SUBLANES - 1) * RNN_SEG_EXTRA and n_main >= 2 and n_main % 2 == 0
    bwd_tail = scan_block(tail0, RNN_SEG_EXTRA, 1, init, past_end=True)
    carry = main(0, (init, bwd_tail), past_end=True)
    carry = lax.fori_loop(1, n_main // 2, main, carry)
    fwd_tot, bwd_tot = lax.fori_loop(n_main // 2, n_main, functools.partial(main, conv="reuse"), carry)
    fwd_tot = scan_block(tail0, RNN_SEG_EXTRA, 0, fwd_tot)
    entries = (entry_states(*fwd_tot, 0), entry_states(*bwd_tot, 1))

    def fix(b, carry):
        fix_block(pl.multiple_of(b * TB, TB), TB, entries)
        return carry

    lax.fori_loop(0, n_main, fix, 0)
    fix_block(tail0, RNN_SEG_EXTRA, entries)

    def finish(c, carry):
        r0 = pl.multiple_of(c * T, T)
        gate = gate_ref[0, pl.ds(r0, T), :].astype(F32)
        o_ref[0, pl.ds(r0, T), :] = (hnat_ref[pl.ds(r0 + PAD, T), :] * gate).astype(BF16)
        return carry

    lax.fori_loop(0, S // T, finish, 0)


def _block_diag_pairs(w):
    per = RNN_CG // RG_BW
    w = w.reshape(RG_BLOCKS // per, per, RG_BW, RG_BW)
    rows = []
    for p in range(per):
        cols = [w[:, p] if q == p else jnp.zeros_like(w[:, p]) for q in range(per)]
        rows.append(jnp.concatenate(cols, axis=-1))
    return jnp.concatenate(rows, axis=1)


def _rglru(z, conv_w, conv_b, rg_wa, rg_ba, rg_wx, rg_bx, rg_lambda):
    B, S, _ = z.shape
    C = RNN_CG
    n_grp = D_RNN // C
    wg = jnp.concatenate([_block_diag_pairs(rg_wa[0]), _block_diag_pairs(rg_wx[0]),
                          _block_diag_pairs(rg_wa[1]), _block_diag_pairs(rg_wx[1])], axis=-1).astype(BF16)
    bg = jnp.concatenate([rg_ba[0].reshape(n_grp, 1, C), rg_bx[0].reshape(n_grp, 1, C),
                          rg_ba[1].reshape(n_grp, 1, C), rg_bx[1].reshape(n_grp, 1, C)], axis=-1)
    lam = rg_lambda.reshape(2, n_grp, C).transpose(1, 0, 2)
    step_rows = SUBLANES * _rnn_seg_len(S)
    nat_rows = step_rows + 2 * SUBLANES
    return pl.pallas_call(
        functools.partial(_rglru_kernel, S=S),
        out_shape=jax.ShapeDtypeStruct((B, S, D_RNN), BF16),
        grid=(B, n_grp),
        in_specs=[
            pl.BlockSpec((1, S, C), lambda b, c: (b, 0, c)),
            pl.BlockSpec((1, S, C), lambda b, c: (b, 0, n_grp + c)),
            pl.BlockSpec((CONV_W, C), lambda b, c: (0, c)),
            pl.BlockSpec((1, C), lambda b, c: (0, c)),
            pl.BlockSpec((1, C, 4 * C), lambda b, c: (c, 0, 0)),
            pl.BlockSpec((1, 1, 4 * C), lambda b, c: (c, 0, 0)),
            pl.BlockSpec((1, 2, C), lambda b, c: (c, 0, 0)),
        ],
        out_specs=pl.BlockSpec((1, S, C), lambda b, c: (b, 0, c)),
        scratch_shapes=[pltpu.VMEM((nat_rows, C), F32), pltpu.VMEM((nat_rows, C), F32),
                        pltpu.VMEM((2, step_rows, C), F32), pltpu.VMEM((2, step_rows, C), F32),
                        pltpu.VMEM((step_rows, C), F32)],
        compiler_params=_params("parallel", "parallel"),
        name="rglru",
    )(z, z, conv_w, conv_b.reshape(1, -1), wg, bg, lam)


def _attn_kernel(q_ref, k_ref, v_ref, o_ref, l_ref, cap_ref, *, L, n_res):
    BQ, KW = ATT_BQ, ATT_KW
    lane = lax.broadcasted_iota(jnp.int32, (BQ, LANES), 1)
    first = lane < HEAD_DIM
    first_kw = lax.broadcasted_iota(jnp.int32, (KW, LANES), 1) < HEAD_DIM
    ones_kw = jnp.concatenate([first_kw, jnp.logical_not(first_kw)], axis=0).astype(BF16)
    rel = lax.broadcasted_iota(jnp.int32, (BQ, KW), 0) - lax.broadcasted_iota(jnp.int32, (BQ, KW), 1)
    n_blocks = L // BQ
    total = n_res * n_blocks
    assert n_blocks & (n_blocks - 1) == 0 and total % ATT_UNROLL == 0

    for j in range(3):
        cap = jnp.where(jnp.abs(rel + j * RADIUS) <= RADIUS, jnp.inf, NEG_INF).astype(F32)
        cap_ref[j, 0:BQ, :] = cap
        cap_ref[j, BQ:2 * BQ, :] = cap

    def scores(idx):
        r = lax.shift_right_logical(idx, n_blocks.bit_length() - 1)
        q0 = pl.multiple_of(jnp.bitwise_and(idx, n_blocks - 1) * BQ, BQ)
        k0 = pl.multiple_of(jnp.clip(q0 - RADIUS, 0, L - KW), RADIUS)
        q = q_ref[0, r, pl.ds(q0, BQ), :]
        k = k_ref[0, r, pl.ds(k0, KW), :]
        zero = jnp.zeros_like(q)
        q2 = jnp.concatenate([jnp.where(first, q, zero), jnp.where(first, zero, q)], axis=0)
        s = lax.dot_general(q2, k, (((1,), (1,)), ((), ())), preferred_element_type=F32)
        return r, q0, k0, s

    def finish(r, q0, k0, s):
        cap = cap_ref[lax.shift_right_logical(q0 - k0, RADIUS.bit_length() - 1)]
        s = jnp.minimum(s, cap)
        m = jnp.max(s, axis=-1, keepdims=True)
        p = jnp.exp2(s - m).astype(BF16)
        v = v_ref[0, r, pl.ds(k0, KW), :]
        zero = jnp.zeros_like(v)
        values = jnp.concatenate([jnp.where(first_kw, v, zero), jnp.where(first_kw, zero, v)], axis=0)
        pv = jnp.dot(jnp.concatenate([p[:BQ], p[BQ:]], axis=1), jnp.concatenate([values, ones_kw], axis=1),
                     preferred_element_type=F32)
        num, den = pv[:, :LANES], pv[:, LANES:]
        top = jnp.where(first, m[:BQ], m[BQ:])
        o_ref[0, r, pl.ds(q0, BQ), :] = (num / den).astype(BF16)
        l_ref[0, r, pl.ds(q0, BQ), :] = top + jnp.log2(den)

    def body(it, carry):
        pending = [scores(it * ATT_UNROLL + j) for j in range(ATT_AHEAD)]
        for j in range(ATT_UNROLL):
            if j + ATT_AHEAD < ATT_UNROLL:
                pending.append(scores(it * ATT_UNROLL + j + ATT_AHEAD))
            finish(*pending.pop(0))
        return carry

    lax.fori_loop(0, total // ATT_UNROLL, body, 0)


def _attention(qkv):
    B, dil, L, _ = qkv.shape
    n_pair = GROUP_W // LANES
    spec = lambda off: pl.BlockSpec((1, dil, L, LANES), lambda b, p: (b, 0, 0, off + p))
    return pl.pallas_call(
        functools.partial(_attn_kernel, L=L, n_res=dil),
        out_shape=[jax.ShapeDtypeStruct((B, dil, L, GROUP_W), BF16),
                   jax.ShapeDtypeStruct((B, dil, L, GROUP_W), F32)],
        grid=(B, n_pair),
        in_specs=[spec(0), spec(n_pair), spec(2 * n_pair)],
        out_specs=[spec(0), spec(0)],
        scratch_shapes=[pltpu.VMEM((3, 2 * ATT_BQ, ATT_KW), F32)],
        compiler_params=_params("parallel", "parallel"),
        name="attn",
    )(qkv, qkv, qkv)


def _mixffn_kernel(x_ref, gt1_ref, sc_ref, sh_ref, gt2_ref, g2_ref, gf_ref, mg_ref, rnn_ref,
                   o0_ref, l0_ref, o1_ref, l1_ref, o2_ref, l2_ref,
                   wr_ref, wa_ref, wo_ref, wi_ref, wf_ref, y_ref,
                   fo1_ref, fl1_ref, fo2_ref, fl2_ref, *, tm):
    n_tiles = GROUP_W // LANES

    def unfold(src_ref, dst_ref, dil):
        pitch = _fold_pitch(dil)
        for r in range(dil):
            for s in range(n_tiles):
                dst_ref[s, pl.ds(r, tm // dil, stride=pitch), :] = src_ref[0, r, :, s * LANES:(s + 1) * LANES].astype(F32)
        if pitch == dil:
            return jnp.concatenate([dst_ref[s, 0:tm, :] for s in range(n_tiles)], axis=1)
        return jnp.concatenate(
            [jnp.concatenate([dst_ref[s, pitch * m:pitch * m + dil, :] for m in range(tm // dil)], axis=0)
             for s in range(n_tiles)], axis=1)

    o1 = unfold(o1_ref, fo1_ref, ATT_GROUPS[1][1])
    l1 = unfold(l1_ref, fl1_ref, ATT_GROUPS[1][1])
    o2 = unfold(o2_ref, fo2_ref, ATT_GROUPS[2][1])
    l2 = unfold(l2_ref, fl2_ref, ATT_GROUPS[2][1])
    lses = (l0_ref[0, 0], l1, l2)
    outs = (o0_ref[0, 0].astype(F32), o1, o2)

    top = jnp.maximum(jnp.maximum(lses[0], lses[1]), lses[2])
    es = [jnp.exp2(l - top) for l in lses]
    att = (es[0] * outs[0] + es[1] * outs[1] + es[2] * outs[2]) / (es[0] + es[1] + es[2])

    br_a = jnp.dot(att.astype(BF16), wa_ref[...], preferred_element_type=F32)
    br_r = jnp.dot(rnn_ref[0], wr_ref[...], preferred_element_type=F32)
    gate = _sigmoid(mg_ref[0].astype(F32))
    merged = gate[:, :D_MODEL] * br_r + gate[:, D_MODEL:] * br_a
    rows = tm // HEAD_PIECES
    pieces = [slice(r0, r0 + rows) for r0 in range(0, tm, rows)]
    merged = merged.astype(BF16)
    x1s, hs = [], []
    for p in pieces:
        mix = jnp.dot(merged[p], wo_ref[...], preferred_element_type=F32)
        x1s.append(x_ref[0, p, :] + gt1_ref[0] * mix)
        hs.append(_rms_mod(x1s[-1], g2_ref[...], sc_ref[0], sh_ref[0]).astype(BF16))
    h = jnp.concatenate(hs, axis=0)

    def up(lhs, c):
        lo = c * FF_CHUNK
        fg = jnp.dot(lhs, wi_ref[:, lo:lo + FF_CHUNK], preferred_element_type=F32)
        fu = jnp.dot(lhs, wi_ref[:, D_FF + lo:D_FF + lo + FF_CHUNK], preferred_element_type=F32)
        return ((fg * _sigmoid(fg)) * fu).astype(BF16)

    n_chunks = D_FF // FF_CHUNK
    ff = None
    for c in range(n_chunks - 1):
        act = jnp.concatenate([up(part, c) for part in hs], axis=0) if c == 0 else up(h, c)
        part = jnp.dot(act, wf_ref[c * FF_CHUNK:(c + 1) * FF_CHUNK, :], preferred_element_type=F32)
        ff = part if ff is None else ff + part
    act = up(h, n_chunks - 1)
    for p, x1 in zip(pieces, x1s):
        ffp = ff[p] + jnp.dot(act[p], wf_ref[(n_chunks - 1) * FF_CHUNK:, :], preferred_element_type=F32)
        x2 = x1 + gt2_ref[0] * ffp
        y = x2 * lax.rsqrt(jnp.mean(x2 * x2, axis=-1, keepdims=True) + EPS)
        y_ref[0, p, :] = y * gf_ref[...]


def _mixffn(x, mods, g2, gf, z, rnn, attn_outs, weights, tm):
    B, S, D = x.shape
    mod_spec = pl.BlockSpec((1, 1, D), lambda b, i: (b, 0, 0))
    in_specs = [pl.BlockSpec((1, tm, D), lambda b, i: (b, i, 0))] + [mod_spec] * len(mods)
    in_specs += [
        _resident((1, D)),
        _resident((1, D)),
        pl.BlockSpec((1, tm, 2 * D_MODEL), lambda b, i: (b, i, 1)),
        pl.BlockSpec((1, tm, D_RNN), lambda b, i: (b, i, 0)),
    ]
    args = [x, *mods, g2, gf, z, rnn]
    for (o, l), (_, d) in zip(attn_outs, ATT_GROUPS):
        blk = pl.BlockSpec((1, d, tm // d, GROUP_W), lambda b, i: (b, 0, i, 0))
        in_specs += [blk, blk]
        args += [o, l]
    in_specs += [_resident(w.shape) for w in weights]
    args += list(weights)
    return pl.pallas_call(
        functools.partial(_mixffn_kernel, tm=tm),
        out_shape=jax.ShapeDtypeStruct((B, S, D), F32),
        grid=(B, S // tm),
        in_specs=in_specs,
        out_specs=pl.BlockSpec((1, tm, D), lambda b, i: (b, i, 0)),
        scratch_shapes=[pltpu.VMEM((GROUP_W // LANES, _fold_pitch(d) * (tm // d), LANES), F32)
                        for _, d in ATT_GROUPS[1:] for _ in range(2)],
        compiler_params=_params("parallel", "parallel"),
        name="mixffn",
    )(*args)


def _encode(x, mod, w, tm_in, tm_mix):
    sh1, sc1, gt1, sh2, sc2, gt2 = (mod[:, k] for k in range(N_MOD))
    outs = _inproj(x, sc1, sh1, w["norm1_g"], w["w_in"], tm_in)
    z, qkvs = outs[0], outs[1:]
    rnn = _rglru(z, w["conv_w"], w["conv_b"], w["rg_wa"], w["rg_ba"], w["rg_wx"], w["rg_bx"], w["rg_lambda"])
    attn_outs = [_attention(qkv) for qkv in qkvs]
    weights = (w["w_br_rnn"], w["w_br_attn"], w["w_out"], w["w_ffn_in"], w["w_ffn_out"])
    return _mixffn(x, (gt1, sc2, sh2, gt2), w["norm2_g"], w["final_g"], z, rnn, attn_outs, weights, tm_mix)


def kernel(x_prompt, x_sample, c_prompt, c_sample, w_ada, b_ada, norm1_g, w_in, conv_w, conv_b, rg_wa, rg_ba, rg_wx, rg_bx, rg_lambda, w_br_rnn, w_br_attn, w_out, norm2_g, w_ffn_in, w_ffn_out, final_g):
    assert w_ada.shape[0] == 1, "single layer"
    w = dict(
        norm1_g=norm1_g[0].reshape(1, -1),
        w_in=w_in[0].astype(BF16), conv_w=conv_w[0], conv_b=conv_b[0],
        rg_wa=rg_wa[0], rg_ba=rg_ba[0], rg_wx=rg_wx[0], rg_bx=rg_bx[0], rg_lambda=rg_lambda[0],
        w_br_rnn=w_br_rnn[0].astype(BF16), w_br_attn=w_br_attn[0].astype(BF16), w_out=w_out[0].astype(BF16),
        norm2_g=norm2_g[0].reshape(1, -1), w_ffn_in=w_ffn_in[0].astype(BF16),
        w_ffn_out=w_ffn_out[0].astype(BF16), final_g=final_g.reshape(1, -1),
    )
    tiles = dict(tm_in=512, tm_mix=512)
    n_prompt = c_prompt.shape[0]
    mod = _modulation(jnp.concatenate([c_prompt, c_sample], axis=0), w_ada[0], b_ada[0])
    return (_encode(x_prompt, mod[:n_prompt], w, **tiles), _encode(x_sample, mod[n_prompt:], w, **tiles))
```

```python
import functools

import jax
import jax.numpy as jnp
from jax import lax
from jax.experimental import pallas as pl
from jax.experimental.pallas import tpu as pltpu

F32 = jnp.float32
BF16 = jnp.bfloat16

D_MODEL = 1024
D_RNN = 1024
RG_BLOCKS = 16
RG_BW = D_RNN // RG_BLOCKS
RG_C = 8.0
CONV_W = 4
CONV_LEFT = 2
HEAD_DIM = 64
HEADS_PER_GROUP = 8
ATT_GROUPS = ((128, 1), (512, 4), (2048, 16))
N_GROUPS = len(ATT_GROUPS)
GROUP_W = HEADS_PER_GROUP * HEAD_DIM
ATT_W = N_GROUPS * GROUP_W
ROT_DIM = HEAD_DIM // 4
ROPE_THETA = 500000.0
D_FF = 2816
N_MOD = 6
EPS = 1e-6
NEG_INF = -1e30
LOG2E = 1.4426950408889634
IN_COLS = 2 * D_RNN + 3 * ATT_W + 2 * D_MODEL
RADIUS = 64

LANES = 128
SUBLANES = 8
BF16_ROWS = 2 * SUBLANES
VMEM_LIMIT_BYTES = 56 * 1024 * 1024

COL_CHUNK = 512
RNN_CG = 128
RNN_T = 256
RNN_TB = 256
RNN_SEG_EXTRA = 4
ATT_BQ = 128
ATT_KW = ATT_BQ + 2 * RADIUS
ATT_UNROLL = 8
ATT_AHEAD = 2
FF_CHUNK = 256
HEAD_PIECES = 2


def _resident(shape):
    nd = len(shape)
    return pl.BlockSpec(shape, lambda *_: (0,) * nd, pipeline_mode=pl.Buffered(1))


def _params(*sem):
    return pltpu.CompilerParams(dimension_semantics=sem, vmem_limit_bytes=VMEM_LIMIT_BYTES)


def _rms_mod(x, g, sc, sh):
    y = x * lax.rsqrt(jnp.mean(x * x, axis=-1, keepdims=True) + EPS)
    return y * (g * (1.0 + sc)) + sh


def _sigmoid(x):
    return 1.0 / (1.0 + jnp.exp2(x * -LOG2E))


def _gelu_tanh(x):
    return 0.5 * x * (1.0 + jnp.tanh(0.7978845608028654 * (x + 0.044715 * (x * x * x))))


def _mod_kernel(c_ref, w_ref, b_ref, o_ref):
    c = c_ref[...]
    s = (c * _sigmoid(c)).astype(BF16)
    o_ref[...] = jnp.dot(s, w_ref[...].astype(BF16), preferred_element_type=F32) + b_ref[...]


def _modulation(c, w_ada, b_ada):
    B = c.shape[0]
    rows = -(-B // BF16_ROWS) * BF16_ROWS
    cp = jnp.pad(c, ((0, rows - B), (0, 0)))
    out = pl.pallas_call(
        _mod_kernel,
        out_shape=jax.ShapeDtypeStruct((rows, N_MOD * D_MODEL), F32),
        grid=(N_MOD,),
        in_specs=[
            pl.BlockSpec((rows, D_MODEL), lambda j: (0, 0)),
            pl.BlockSpec((D_MODEL, D_MODEL), lambda j: (0, j)),
            pl.BlockSpec((1, D_MODEL), lambda j: (0, j)),
        ],
        out_specs=pl.BlockSpec((rows, D_MODEL), lambda j: (0, j)),
        compiler_params=_params("parallel"),
        name="mod",
    )(cp, w_ada, b_ada.reshape(1, -1))
    return out[:B].reshape(B, N_MOD, 1, D_MODEL)


def _fold_pitch(dil):
    return dil if dil % (2 * SUBLANES) else dil + SUBLANES


def _inproj_kernel(x_ref, sc_ref, sh_ref, g_ref, w_ref, cos_ref, sa_ref, sb_ref,
                   z_ref, q0_ref, q1_ref, q2_ref, fold_ref, *, tm):
    rows = tm // HEAD_PIECES
    pieces = [_rms_mod(x_ref[0, r0:r0 + rows, :], g_ref[...], sc_ref[0], sh_ref[0]).astype(BF16)
              for r0 in range(0, tm, rows)]
    h = jnp.concatenate(pieces, axis=0)
    issued = []

    def proj(j):
        w = w_ref[:, j * COL_CHUNK:(j + 1) * COL_CHUNK]
        issued.append(j)
        if len(issued) == 1:
            return jnp.concatenate([jnp.dot(part, w, preferred_element_type=F32) for part in pieces], axis=0)
        return jnp.dot(h, w, preferred_element_type=F32)

    def rope(v):
        parts = []
        for s in range(COL_CHUNK // LANES):
            p = v[:, s * LANES:(s + 1) * LANES]
            up = pltpu.roll(p, LANES - ROT_DIM // 2, axis=1)
            dn = pltpu.roll(p, ROT_DIM // 2, axis=1)
            parts.append(p * cos_ref[...] + up * sa_ref[...] + dn * sb_ref[...])
        return jnp.concatenate(parts, axis=1)

    qkv_refs = (q0_ref, q1_ref, q2_ref)

    def emit_folded(val, g, off):
        dil = ATT_GROUPS[g][1]
        ref = qkv_refs[g]
        if dil == 1:
            ref[0, 0, :, off:off + COL_CHUNK] = val.astype(BF16)
            return
        pitch = _fold_pitch(dil)
        for s in range(COL_CHUNK // LANES):
            slab = val[:, s * LANES:(s + 1) * LANES]
            if pitch == dil:
                fold_ref[s, 0:tm, :] = slab
            else:
                for m in range(tm // dil):
                    fold_ref[s, pitch * m:pitch * m + dil, :] = slab[dil * m:dil * (m + 1)]
        for r in range(dil):
            for s in range(COL_CHUNK // LANES):
                lo = off + s * LANES
                ref[0, r, :, lo:lo + LANES] = fold_ref[s, pl.ds(r, tm // dil, stride=pitch), :].astype(BF16)

    def emit_z(j_w, j_z, fn):
        z_ref[0, :, j_z * COL_CHUNK:(j_z + 1) * COL_CHUNK] = fn(proj(j_w)).astype(BF16)

    n_x = D_RNN // COL_CHUNK
    n_rnn = 2 * n_x
    n_gate0 = n_rnn + 3 * N_GROUPS
    for j in range(2 * D_MODEL // COL_CHUNK):
        emit_z(n_gate0 + j, n_rnn + j, lambda v: v)
        if j < n_x:
            emit_z(n_x + j, n_x + j, _gelu_tanh)
    for g in range(N_GROUPS):
        emit_folded(rope(proj(n_rnn + g)) * (HEAD_DIM ** -0.5 * LOG2E), g, 0)
        emit_folded(rope(proj(n_rnn + N_GROUPS + g)), g, GROUP_W)
        emit_folded(proj(n_rnn + 2 * N_GROUPS + g), g, 2 * GROUP_W)
    for j in range(n_x):
        emit_z(j, j, lambda v: v)


def _rope_tables(S):
    half = ROT_DIM // 2
    inv = ROPE_THETA ** (-(jnp.arange(0, ROT_DIM, 2, dtype=F32) / ROT_DIM))
    ang = jnp.arange(S, dtype=F32)[:, None] * inv[None, :]
    cos, sin = jnp.cos(ang), jnp.sin(ang)
    zeros = jnp.zeros((S, HEAD_DIM - ROT_DIM), F32)
    z8 = jnp.zeros((S, half), F32)
    c = jnp.concatenate([cos, cos, zeros + 1.0], axis=1)
    sa = jnp.concatenate([-sin, z8, zeros], axis=1)
    sb = jnp.concatenate([z8, sin, zeros], axis=1)
    rep = LANES // HEAD_DIM
    return tuple(jnp.tile(t, (1, rep)) for t in (c, sa, sb))


def _inproj(x, sc, sh, g, w_in, tm):
    B, S, D = x.shape
    cos, sa, sb = _rope_tables(S)
    dils = [d for _, d in ATT_GROUPS]
    assert S % tm == 0 and all(tm % (d * BF16_ROWS) == 0 for d in dils) and tm % (HEAD_PIECES * BF16_ROWS) == 0
    row = lambda b, i: (b, 0, 0)
    out_shape = [jax.ShapeDtypeStruct((B, S, 2 * D_RNN + 2 * D_MODEL), BF16)]
    out_specs = [pl.BlockSpec((1, tm, 2 * D_RNN + 2 * D_MODEL), lambda b, i: (b, i, 0))]
    for d in dils:
        out_shape.append(jax.ShapeDtypeStruct((B, d, S // d, 3 * GROUP_W), BF16))
        out_specs.append(pl.BlockSpec((1, d, tm // d, 3 * GROUP_W), lambda b, i: (b, 0, i, 0)))
    tab = pl.BlockSpec((tm, LANES), lambda b, i: (i, 0))
    return pl.pallas_call(
        functools.partial(_inproj_kernel, tm=tm),
        out_shape=out_shape,
        grid=(B, S // tm),
        in_specs=[
            pl.BlockSpec((1, tm, D), lambda b, i: (b, i, 0)),
            pl.BlockSpec((1, 1, D), row),
            pl.BlockSpec((1, 1, D), row),
            _resident((1, D)),
            _resident((D, IN_COLS)),
            tab, tab, tab,
        ],
        out_specs=out_specs,
        scratch_shapes=[pltpu.VMEM((COL_CHUNK // LANES, max(_fold_pitch(d) * (tm // d) for d in dils), LANES), F32)],
        compiler_params=_params("parallel", "parallel"),
        name="inproj",
    )(x, sc, sh, g, w_in, cos, sa, sb)


def _rnn_seg_len(S):
    assert S % (8 * SUBLANES) == 0
    return S // SUBLANES + RNN_SEG_EXTRA


def _rglru_kernel(x_ref, gate_ref, cw_ref, cb_ref, wg_ref, bg_ref, lam_ref, o_ref,
                  xnat_ref, hnat_ref, hloc_ref, acum_ref, xc_ref, *, S):
    C = RNN_CG
    T = RNN_T
    TB = RNN_TB
    PAD = SUBLANES
    seg = _rnn_seg_len(S)
    n_main = (seg - RNN_SEG_EXTRA) // TB
    rows = xnat_ref.shape[0]

    xnat_ref[0:PAD, :] = jnp.zeros((PAD, C), F32)
    xnat_ref[PAD + S:rows, :] = jnp.zeros((rows - PAD - S, C), F32)

    def fill(c, carry):
        r0 = pl.multiple_of(c * T, T)
        xnat_ref[pl.ds(r0 + PAD, T), :] = x_ref[0, pl.ds(r0, T), :].astype(F32)
        return carry

    lax.fori_loop(0, S // T, fill, 0)

    neg_lam = -lam_ref[0]
    softplus = jnp.maximum(neg_lam, 0.0) + jnp.log1p(jnp.exp(-jnp.abs(neg_lam)))
    half_coef2 = (-0.5 * RG_C * LOG2E) * softplus
    half_bias = 0.5 * bg_ref[0]
    cwb = [jnp.broadcast_to(cw_ref[k:k + 1, :], (SUBLANES, C)) for k in range(CONV_W)]
    cbb = jnp.broadcast_to(cb_ref[...], (SUBLANES, C))
    sub = lax.broadcasted_iota(jnp.int32, (SUBLANES, C), 0)
    steps_left = S - sub * seg

    def step_rows(t):
        return (pl.ds(PAD + t, SUBLANES, stride=seg), slice(None))

    def block_gates(t0, n, direction, conv):
        rows = pl.ds(pl.multiple_of(t0 * SUBLANES, SUBLANES), n * SUBLANES)
        if conv == "reuse":
            xc = xc_ref[rows, :]
        else:
            taps = [xnat_ref[step_rows(t0 + m - CONV_LEFT)] for m in range(n + CONV_W - 1)]
            xcs = []
            for j in range(n):
                acc = cbb + taps[j] * cwb[0]
                for k in range(1, CONV_W):
                    acc = acc + taps[j + k] * cwb[k]
                xcs.append(acc)
            xc = jnp.concatenate(xcs, axis=0)
            if conv == "keep":
                xc_ref[rows, :] = xc
        lo = direction * 2 * C
        gz = jnp.dot(xc.astype(BF16), wg_ref[0, :, lo:lo + 2 * C], preferred_element_type=F32)
        t = jnp.tanh(0.5 * gz + half_bias[:, lo:lo + 2 * C])
        half_coef = half_coef2[direction:direction + 1]
        a = jnp.exp2(half_coef * t[:, :C] + half_coef)
        y = 1.0 - a * a
        root = jnp.where(y > 0.0, y * lax.rsqrt(y), 0.0)
        half_xc = 0.5 * xc
        u = root * (half_xc * t[:, C:] + half_xc)
        return a, u

    def scan_block(t0, n, direction, carry, past_end=False, conv="own"):
        h, acc = carry
        a, u = block_gates(t0, n, direction, conv)
        order = range(n) if direction == 0 else range(n - 1, -1, -1)
        for j in order:
            aj = a[j * SUBLANES:(j + 1) * SUBLANES]
            uj = u[j * SUBLANES:(j + 1) * SUBLANES]
            if direction == 1 and past_end:
                uj = jnp.where(t0 + j < steps_left, uj, 0.0)
            h = aj * h + uj
            acc = aj * acc
            row = pl.multiple_of((t0 + j) * SUBLANES, SUBLANES)
            hloc_ref[direction, pl.ds(row, SUBLANES), :] = h
            acum_ref[direction, pl.ds(row, SUBLANES), :] = acc
        return h, acc

    def entry_states(h_tot, a_tot, direction):
        c = jnp.zeros((1, C), F32)
        out = jnp.zeros((SUBLANES, C), F32)
        order = range(SUBLANES) if direction == 0 else range(SUBLANES - 1, -1, -1)
        for s in order:
            out = jnp.where(sub == s, c, out)
            c = a_tot[s:s + 1] * c + h_tot[s:s + 1]
        return out

    def fix_block(t0, n, entries):
        for j in range(n):
            row = pl.multiple_of((t0 + j) * SUBLANES, SUBLANES)
            h = None
            for direction in (0, 1):
                part = (hloc_ref[direction, pl.ds(row, SUBLANES), :]
                        + acum_ref[direction, pl.ds(row, SUBLANES), :] * entries[direction])
                h = part if h is None else h + part
            hnat_ref[step_rows(t0 + j)] = h

    tail0 = n_main * TB
    init = (jnp.zeros((SUBLANES, C), F32), jnp.ones((SUBLANES, C), F32))

    def block_start(b):
        return b * TB if isinstance(b, int) else pl.multiple_of(b * TB, TB)

    def main(b, carry, past_end=False, conv="keep"):
        fwd = scan_block(block_start(b), TB, 0, carry[0], conv=conv)
        bwd = scan_block(block_start(n_main - 1 - b), TB, 1, carry[1], past_end, conv=conv)
        return fwd, bwd

    assert TB >= (SUBLANES - 1) * RNN_SEG_EXTRA and n_main >= 2 and n_main % 2 == 0
    bwd_tail = scan_block(tail0, RNN_SEG_EXTRA, 1, init, past_end=True)
    carry = main(0, (init, bwd_tail), past_end=True)
    carry = lax.fori_loop(1, n_main // 2, main, carry)
    fwd_tot, bwd_tot = lax.fori_loop(n_main // 2, n_main, functools.partial(main, conv="reuse"), carry)
    fwd_tot = scan_block(tail0, RNN_SEG_EXTRA, 0, fwd_tot)
    entries = (entry_states(*fwd_tot, 0), entry_states(*bwd_tot, 1))

    def fix(b, carry):
        fix_block(pl.multiple_of(b * TB, TB), TB, entries)
        return carry

    lax.fori_loop(0, n_main, fix, 0)
    fix_block(tail0, RNN_SEG_EXTRA, entries)

    def finish(c, carry):
        r0 = pl.multiple_of(c * T, T)
        gate = gate_ref[0, pl.ds(r0, T), :].astype(F32)
        o_ref[0, pl.ds(r0, T), :] = (hnat_ref[pl.ds(r0 + PAD, T), :] * gate).astype(BF16)
        return carry

    lax.fori_loop(0, S // T, finish, 0)


def _block_diag_pairs(w):
    per = RNN_CG // RG_BW
    w = w.reshape(RG_BLOCKS // per, per, RG_BW, RG_BW)
    rows = []
    for p in range(per):
        cols = [w[:, p] if q == p else jnp.zeros_like(w[:, p]) for q in range(per)]
        rows.append(jnp.concatenate(cols, axis=-1))
    return jnp.concatenate(rows, axis=1)


def _rglru(z, conv_w, conv_b, rg_wa, rg_ba, rg_wx, rg_bx, rg_lambda):
    B, S, _ = z.shape
    C = RNN_CG
    n_grp = D_RNN // C
    wg = jnp.concatenate([_block_diag_pairs(rg_wa[0]), _block_diag_pairs(rg_wx[0]),
                          _block_diag_pairs(rg_wa[1]), _block_diag_pairs(rg_wx[1])], axis=-1).astype(BF16)
    bg = jnp.concatenate([rg_ba[0].reshape(n_grp, 1, C), rg_bx[0].reshape(n_grp, 1, C),
                          rg_ba[1].reshape(n_grp, 1, C), rg_bx[1].reshape(n_grp, 1, C)], axis=-1)
    lam = rg_lambda.reshape(2, n_grp, C).transpose(1, 0, 2)
    step_rows = SUBLANES * _rnn_seg_len(S)
    nat_rows = step_rows + 2 * SUBLANES
    return pl.pallas_call(
        functools.partial(_rglru_kernel, S=S),
        out_shape=jax.ShapeDtypeStruct((B, S, D_RNN), BF16),
        grid=(B, n_grp),
        in_specs=[
            pl.BlockSpec((1, S, C), lambda b, c: (b, 0, c)),
            pl.BlockSpec((1, S, C), lambda b, c: (b, 0, n_grp + c)),
            pl.BlockSpec((CONV_W, C), lambda b, c: (0, c)),
            pl.BlockSpec((1, C), lambda b, c: (0, c)),
            pl.BlockSpec((1, C, 4 * C), lambda b, c: (c, 0, 0)),
            pl.BlockSpec((1, 1, 4 * C), lambda b, c: (c, 0, 0)),
            pl.BlockSpec((1, 2, C), lambda b, c: (c, 0, 0)),
        ],
        out_specs=pl.BlockSpec((1, S, C), lambda b, c: (b, 0, c)),
        scratch_shapes=[pltpu.VMEM((nat_rows, C), F32), pltpu.VMEM((nat_rows, C), F32),
                        pltpu.VMEM((2, step_rows, C), F32), pltpu.VMEM((2, step_rows, C), F32),
                        pltpu.VMEM((step_rows, C), F32)],
        compiler_params=_params("parallel", "parallel"),
        name="rglru",
    )(z, z, conv_w, conv_b.reshape(1, -1), wg, bg, lam)


def _attn_kernel(q_ref, k_ref, v_ref, o_ref, l_ref, cap_ref, *, L, n_res):
    BQ, KW = ATT_BQ, ATT_KW
    lane = lax.broadcasted_iota(jnp.int32, (BQ, LANES), 1)
    first = lane < HEAD_DIM
    first_kw = lax.broadcasted_iota(jnp.int32, (KW, LANES), 1) < HEAD_DIM
    ones_kw = jnp.concatenate([first_kw, jnp.logical_not(first_kw)], axis=0).astype(BF16)
    rel = lax.broadcasted_iota(jnp.int32, (BQ, KW), 0) - lax.broadcasted_iota(jnp.int32, (BQ, KW), 1)
    n_blocks = L // BQ
    total = n_res * n_blocks
    assert n_blocks & (n_blocks - 1) == 0 and total % ATT_UNROLL == 0

    for j in range(3):
        cap = jnp.where(jnp.abs(rel + j * RADIUS) <= RADIUS, jnp.inf, NEG_INF).astype(F32)
        cap_ref[j, 0:BQ, :] = cap
        cap_ref[j, BQ:2 * BQ, :] = cap

    def scores(idx):
        r = lax.shift_right_logical(idx, n_blocks.bit_length() - 1)
        q0 = pl.multiple_of(jnp.bitwise_and(idx, n_blocks - 1) * BQ, BQ)
        k0 = pl.multiple_of(jnp.clip(q0 - RADIUS, 0, L - KW), RADIUS)
        q = q_ref[0, r, pl.ds(q0, BQ), :]
        k = k_ref[0, r, pl.ds(k0, KW), :]
        zero = jnp.zeros_like(q)
        q2 = jnp.concatenate([jnp.where(first, q, zero), jnp.where(first, zero, q)], axis=0)
        s = lax.dot_general(q2, k, (((1,), (1,)), ((), ())), preferred_element_type=F32)
        return r, q0, k0, s

    def finish(r, q0, k0, s):
        cap = cap_ref[lax.shift_right_logical(q0 - k0, RADIUS.bit_length() - 1)]
        s = jnp.minimum(s, cap)
        m = jnp.max(s, axis=-1, keepdims=True)
        p = jnp.exp2(s - m).astype(BF16)
        v = v_ref[0, r, pl.ds(k0, KW), :]
        zero = jnp.zeros_like(v)
        values = jnp.concatenate([jnp.where(first_kw, v, zero), jnp.where(first_kw, zero, v)], axis=0)
        pv = jnp.dot(jnp.concatenate([p[:BQ], p[BQ:]], axis=1), jnp.concatenate([values, ones_kw], axis=1),
                     preferred_element_type=F32)
        num, den = pv[:, :LANES], pv[:, LANES:]
        top = jnp.where(first, m[:BQ], m[BQ:])
        o_ref[0, r, pl.ds(q0, BQ), :] = (num / den).astype(BF16)
        l_ref[0, r, pl.ds(q0, BQ), :] = top + jnp.log2(den)

    def body(it, carry):
        pending = [scores(it * ATT_UNROLL + j) for j in range(ATT_AHEAD)]
        for j in range(ATT_UNROLL):
            if j + ATT_AHEAD < ATT_UNROLL:
                pending.append(scores(it * ATT_UNROLL + j + ATT_AHEAD))
            finish(*pending.pop(0))
        return carry

    lax.fori_loop(0, total // ATT_UNROLL, body, 0)


def _attention(qkv):
    B, dil, L, _ = qkv.shape
    n_pair = GROUP_W // LANES
    spec = lambda off: pl.BlockSpec((1, dil, L, LANES), lambda b, p: (b, 0, 0, off + p))
    return pl.pallas_call(
        functools.partial(_attn_kernel, L=L, n_res=dil),
        out_shape=[jax.ShapeDtypeStruct((B, dil, L, GROUP_W), BF16),
                   jax.ShapeDtypeStruct((B, dil, L, GROUP_W), F32)],
        grid=(B, n_pair),
        in_specs=[spec(0), spec(n_pair), spec(2 * n_pair)],
        out_specs=[spec(0), spec(0)],
        scratch_shapes=[pltpu.VMEM((3, 2 * ATT_BQ, ATT_KW), F32)],
        compiler_params=_params("parallel", "parallel"),
        name="attn",
    )(qkv, qkv, qkv)


def _mixffn_kernel(x_ref, gt1_ref, sc_ref, sh_ref, gt2_ref, g2_ref, gf_ref, mg_ref, rnn_ref,
                   o0_ref, l0_ref, o1_ref, l1_ref, o2_ref, l2_ref,
                   wr_ref, wa_ref, wo_ref, wi_ref, wf_ref, y_ref,
                   fo1_ref, fl1_ref, fo2_ref, fl2_ref, *, tm):
    n_tiles = GROUP_W // LANES

    def unfold(src_ref, dst_ref, dil):
        pitch = _fold_pitch(dil)
        for r in range(dil):
            for s in range(n_tiles):
                dst_ref[s, pl.ds(r, tm // dil, stride=pitch), :] = src_ref[0, r, :, s * LANES:(s + 1) * LANES].astype(F32)
        if pitch == dil:
            return jnp.concatenate([dst_ref[s, 0:tm, :] for s in range(n_tiles)], axis=1)
        return jnp.concatenate(
            [jnp.concatenate([dst_ref[s, pitch * m:pitch * m + dil, :] for m in range(tm // dil)], axis=0)
             for s in range(n_tiles)], axis=1)

    o1 = unfold(o1_ref, fo1_ref, ATT_GROUPS[1][1])
    l1 = unfold(l1_ref, fl1_ref, ATT_GROUPS[1][1])
    o2 = unfold(o2_ref, fo2_ref, ATT_GROUPS[2][1])
    l2 = unfold(l2_ref, fl2_ref, ATT_GROUPS[2][1])
    lses = (l0_ref[0, 0], l1, l2)
    outs = (o0_ref[0, 0].astype(F32), o1, o2)

    top = jnp.maximum(jnp.maximum(lses[0], lses[1]), lses[2])
    es = [jnp.exp2(l - top) for l in lses]
    att = (es[0] * outs[0] + es[1] * outs[1] + es[2] * outs[2]) / (es[0] + es[1] + es[2])

    br_a = jnp.dot(att.astype(BF16), wa_ref[...], preferred_element_type=F32)
    br_r = jnp.dot(rnn_ref[0], wr_ref[...], preferred_element_type=F32)
    gate = _sigmoid(mg_ref[0].astype(F32))
    merged = gate[:, :D_MODEL] * br_r + gate[:, D_MODEL:] * br_a
    rows = tm // HEAD_PIECES
    pieces = [slice(r0, r0 + rows) for r0 in range(0, tm, rows)]
    merged = merged.astype(BF16)
    x1s, hs = [], []
    for p in pieces:
        mix = jnp.dot(merged[p], wo_ref[...], preferred_element_type=F32)
        x1s.append(x_ref[0, p, :] + gt1_ref[0] * mix)
        hs.append(_rms_mod(x1s[-1], g2_ref[...], sc_ref[0], sh_ref[0]).astype(BF16))
    h = jnp.concatenate(hs, axis=0)

    def up(lhs, c):
        lo = c * FF_CHUNK
        fg = jnp.dot(lhs, wi_ref[:, lo:lo + FF_CHUNK], preferred_element_type=F32)
        fu = jnp.dot(lhs, wi_ref[:, D_FF + lo:D_FF + lo + FF_CHUNK], preferred_element_type=F32)
        return ((fg * _sigmoid(fg)) * fu).astype(BF16)

    n_chunks = D_FF // FF_CHUNK
    ff = None
    for c in range(n_chunks - 1):
        act = jnp.concatenate([up(part, c) for part in hs], axis=0) if c == 0 else up(h, c)
        part = jnp.dot(act, wf_ref[c * FF_CHUNK:(c + 1) * FF_CHUNK, :], preferred_element_type=F32)
        ff = part if ff is None else ff + part
    act = up(h, n_chunks - 1)
    for p, x1 in zip(pieces, x1s):
        ffp = ff[p] + jnp.dot(act[p], wf_ref[(n_chunks - 1) * FF_CHUNK:, :], preferred_element_type=F32)
        x2 = x1 + gt2_ref[0] * ffp
        y = x2 * lax.rsqrt(jnp.mean(x2 * x2, axis=-1, keepdims=True) + EPS)
        y_ref[0, p, :] = y * gf_ref[...]


def _mixffn(x, mods, g2, gf, z, rnn, attn_outs, weights, tm):
    B, S, D = x.shape
    mod_spec = pl.BlockSpec((1, 1, D), lambda b, i: (b, 0, 0))
    in_specs = [pl.BlockSpec((1, tm, D), lambda b, i: (b, i, 0))] + [mod_spec] * len(mods)
    in_specs += [
        _resident((1, D)),
        _resident((1, D)),
        pl.BlockSpec((1, tm, 2 * D_MODEL), lambda b, i: (b, i, 1)),
        pl.BlockSpec((1, tm, D_RNN), lambda b, i: (b, i, 0)),
    ]
    args = [x, *mods, g2, gf, z, rnn]
    for (o, l), (_, d) in zip(attn_outs, ATT_GROUPS):
        blk = pl.BlockSpec((1, d, tm // d, GROUP_W), lambda b, i: (b, 0, i, 0))
        in_specs += [blk, blk]
        args += [o, l]
    in_specs += [_resident(w.shape) for w in weights]
    args += list(weights)
    return pl.pallas_call(
        functools.partial(_mixffn_kernel, tm=tm),
        out_shape=jax.ShapeDtypeStruct((B, S, D), F32),
        grid=(B, S // tm),
        in_specs=in_specs,
        out_specs=pl.BlockSpec((1, tm, D), lambda b, i: (b, i, 0)),
        scratch_shapes=[pltpu.VMEM((GROUP_W // LANES, _fold_pitch(d) * (tm // d), LANES), F32)
                        for _, d in ATT_GROUPS[1:] for _ in range(2)],
        compiler_params=_params("parallel", "parallel"),
        name="mixffn",
    )(*args)


def _encode(x, mod, w, tm_in, tm_mix):
    sh1, sc1, gt1, sh2, sc2, gt2 = (mod[:, k] for k in range(N_MOD))
    outs = _inproj(x, sc1, sh1, w["norm1_g"], w["w_in"], tm_in)
    z, qkvs = outs[0], outs[1:]
    rnn = _rglru(z, w["conv_w"], w["conv_b"], w["rg_wa"], w["rg_ba"], w["rg_wx"], w["rg_bx"], w["rg_lambda"])
    attn_outs = [_attention(qkv) for qkv in qkvs]
    weights = (w["w_br_rnn"], w["w_br_attn"], w["w_out"], w["w_ffn_in"], w["w_ffn_out"])
    return _mixffn(x, (gt1, sc2, sh2, gt2), w["norm2_g"], w["final_g"], z, rnn, attn_outs, weights, tm_mix)


def kernel(x_prompt, x_sample, c_prompt, c_sample, w_ada, b_ada, norm1_g, w_in, conv_w, conv_b, rg_wa, rg_ba, rg_wx, rg_bx, rg_lambda, w_br_rnn, w_br_attn, w_out, norm2_g, w_ffn_in, w_ffn_out, final_g):
    assert w_ada.shape[0] == 1, "single layer"
    w = dict(
        norm1_g=norm1_g[0].reshape(1, -1),
        w_in=w_in[0].astype(BF16), conv_w=conv_w[0], conv_b=conv_b[0],
        rg_wa=rg_wa[0], rg_ba=rg_ba[0], rg_wx=rg_wx[0], rg_bx=rg_bx[0], rg_lambda=rg_lambda[0],
        w_br_rnn=w_br_rnn[0].astype(BF16), w_br_attn=w_br_attn[0].astype(BF16), w_out=w_out[0].astype(BF16),
        norm2_g=norm2_g[0].reshape(1, -1), w_ffn_in=w_ffn_in[0].astype(BF16),
        w_ffn_out=w_ffn_out[0].astype(BF16), final_g=final_g.reshape(1, -1),
    )
    tiles = dict(tm_in=512, tm_mix=512)
    n_prompt = c_prompt.shape[0]
    mod = _modulation(jnp.concatenate([c_prompt, c_sample], axis=0), w_ada[0], b_ada[0])
    return (_encode(x_prompt, mod[:n_prompt], w, **tiles), _encode(x_sample, mod[n_prompt:], w, **tiles))
```
